```python
import math, functools
import jax, jax.numpy as jnp
from jax import lax
import numpy as np

D_MODEL = 2048
BATCH = 2
SEQ = 8192
DEPTH = 1
DEC_BATCH = 32
DEC_SEQ = 1
PAST_LEN = 16384
PAGE_SIZE = 128

N_HEADS = 16
N_KV_HEADS = 4
HEAD_DIM = D_MODEL // N_HEADS
D_ATTN = N_HEADS * HEAD_DIM
D_KV = N_KV_HEADS * HEAD_DIM
ROPE_THETA = 10000.0
MOBA_BLOCK = 256
MOBA_TOPK = 3
MOBA_Q_CHUNK = 64
D_LRU = D_MODEL
LRU_HEADS = 16
LRU_BLOCK = D_LRU // LRU_HEADS
CONV_W = 4
LRU_C = 8.0
PEER_HEADS = 8
PEER_N_KEYS = 128
PEER_N_EXPERTS = PEER_N_KEYS * PEER_N_KEYS
PEER_D_KEY = 256
PEER_HALF = PEER_D_KEY // 2
PEER_TOPK = 16
PEER_TOK_CHUNK = 256
N_BRANCH = 2
N_MOD = 6
IN_SPLITS = [D_ATTN, D_ATTN + D_KV, D_ATTN + 2 * D_KV, D_ATTN + 2 * D_KV + D_LRU, D_ATTN + 2 * D_KV + 2 * D_LRU]
IN_COLS = D_ATTN + 2 * D_KV + 2 * D_LRU + N_BRANCH * D_MODEL
EPS = 1e-6

kernel_name = "hybrid_rglru_moba_peer_adaln_step"


def rms_norm(x, g):
    xf = x.astype(jnp.float32)
    y = xf * lax.rsqrt(jnp.mean(xf * xf, axis=-1, keepdims=True) + EPS)
    return (y * g.astype(jnp.float32)).astype(x.dtype)


def modulate(h, shift, scale):
    return h * (1 + scale[:, None, :]) + shift[:, None, :]


def rope(x, pos):
    half = HEAD_DIM // 2
    inv = jnp.exp(-math.log(ROPE_THETA) * jnp.arange(half, dtype=jnp.float32) / half)
    ang = pos.astype(jnp.float32)[:, None] * inv[None, :]
    cos, sin = jnp.cos(ang), jnp.sin(ang)
    xf = x.astype(jnp.float32)
    x1, x2 = xf[..., :half], xf[..., half:]
    return jnp.concatenate([x1 * cos - x2 * sin, x1 * sin + x2 * cos], axis=-1).astype(x.dtype)


def causal_conv(x_ext, w, b):
    t = x_ext.shape[1] - (CONV_W - 1)
    out = b
    for j in range(CONV_W):
        out = out + x_ext[:, j:j + t, :] * w[j]
    return out


def rg_lru(xc, h0, w_a, b_a, w_x, b_x, lam):
    b, t, _ = xc.shape
    xh = xc.reshape(b, t, LRU_HEADS, LRU_BLOCK)
    r = jax.nn.sigmoid(jnp.einsum("bthi,hij->bthj", xh, w_a).reshape(b, t, D_LRU) + b_a)
    gi = jax.nn.sigmoid(jnp.einsum("bthi,hij->bthj", xh, w_x).reshape(b, t, D_LRU) + b_x)
    log_a = (-LRU_C * r.astype(jnp.float32)) * jax.nn.softplus(-lam.astype(jnp.float32))
    a = jnp.exp(log_a)
    bx = jnp.sqrt(-jnp.expm1(2.0 * log_a)) * (gi * xc).astype(jnp.float32)
    bx = bx.at[:, 0].add(a[:, 0] * h0.astype(jnp.float32))

    def combine(e1, e2):
        a1, b1 = e1
        a2, b2 = e2
        return a1 * a2, a2 * b1 + b2

    _, h = lax.associative_scan(combine, (a, bx), axis=1)
    return h.astype(xc.dtype), h[:, -1].astype(xc.dtype)


def to_blocks(k, v):
    b, hk, t, hd = k.shape
    nb = -(-t // MOBA_BLOCK)
    pad = nb * MOBA_BLOCK - t
    k = jnp.pad(k, ((0, 0), (0, 0), (0, pad), (0, 0)))
    v = jnp.pad(v, ((0, 0), (0, 0), (0, pad), (0, 0)))
    kb = k.reshape(b, hk, nb, MOBA_BLOCK, hd)
    vb = v.reshape(b, hk, nb, MOBA_BLOCK, hd)
    kmean = jnp.mean(kb.astype(jnp.float32), axis=3).astype(k.dtype)
    return kb, vb, kmean


def moba_chunk(q, kb, vb, kmean, q_pos, own_blk):
    b, hq, qc, hd = q.shape
    nb = kb.shape[2]
    grp = hq // N_KV_HEADS
    qg = q.reshape(b, N_KV_HEADS, grp, qc, hd)
    gate = jnp.einsum("bkgqd,bknd->bkgqn", qg.astype(jnp.float32), kmean.astype(jnp.float32))
    gate = jnp.where(jnp.arange(nb) < own_blk, gate, -jnp.inf)
    n_sel = min(MOBA_TOPK, nb)
    _, sel = lax.top_k(gate, n_sel)
    sel_ok = sel < own_blk
    take = jax.vmap(jax.vmap(lambda blocks, idx: blocks[idx]))
    k_sel = take(kb, sel)
    v_sel = take(vb, sel)
    scale = HEAD_DIM ** -0.5
    s_past = jnp.einsum("bkgqd,bkgqjsd->bkgqjs", qg, k_sel).astype(jnp.float32) * scale
    s_past = jnp.where(sel_ok[..., None], s_past, -jnp.inf)
    k_own = lax.dynamic_index_in_dim(kb, own_blk, axis=2, keepdims=False)
    v_own = lax.dynamic_index_in_dim(vb, own_blk, axis=2, keepdims=False)
    s_own = jnp.einsum("bkgqd,bksd->bkgqs", qg, k_own).astype(jnp.float32) * scale
    own_pos = own_blk * MOBA_BLOCK + jnp.arange(MOBA_BLOCK)
    s_own = jnp.where(own_pos[None, :] <= q_pos[:, None], s_own, -jnp.inf)
    n_past = n_sel * MOBA_BLOCK
    s = jnp.concatenate([s_past.reshape(b, N_KV_HEADS, grp, qc, n_past), s_own], axis=-1)
    p = jax.nn.softmax(s, axis=-1).astype(q.dtype)
    p_past = p[..., :n_past].reshape(b, N_KV_HEADS, grp, qc, n_sel, MOBA_BLOCK)
    o = (jnp.einsum("bkgqjs,bkgqjsd->bkgqd", p_past, v_sel)
         + jnp.einsum("bkgqs,bksd->bkgqd", p[..., n_past:], v_own))
    return o.reshape(b, hq, qc, hd)


def moba_prompt(q, k, v):
    b, hq, s, hd = q.shape
    kb, vb, kmean = to_blocks(k, v)
    n_chunks = s // MOBA_Q_CHUNK
    qc = q.reshape(b, hq, n_chunks, MOBA_Q_CHUNK, hd).transpose(2, 0, 1, 3, 4)
    starts = jnp.arange(n_chunks, dtype=jnp.int32) * MOBA_Q_CHUNK

    def body(args):
        qi, st = args
        pos = st + jnp.arange(MOBA_Q_CHUNK, dtype=jnp.int32)
        return moba_chunk(qi, kb, vb, kmean, pos, st // MOBA_BLOCK)

    o = lax.map(body, (qc, starts))
    return o.transpose(1, 2, 0, 3, 4).reshape(b, hq, s, hd)


def moba_sample(q, k, v, past_k, past_v):
    past = past_k.shape[2]
    t = q.shape[2]
    k_all = jnp.concatenate([past_k.astype(k.dtype), k], axis=2)
    v_all = jnp.concatenate([past_v.astype(v.dtype), v], axis=2)
    kb, vb, kmean = to_blocks(k_all, v_all)
    pos = past + jnp.arange(t, dtype=jnp.int32)
    return moba_chunk(q, kb, vb, kmean, pos, past // MOBA_BLOCK)


def gather_pages(pool, page_table):
    db, n_pages = page_table.shape
    g = pool[page_table]
    return g.reshape(db, n_pages * PAGE_SIZE, N_KV_HEADS, HEAD_DIM).transpose(0, 2, 1, 3)


def peer(h, w_q, sub_keys, u_tab, v_tab):
    n = h.shape[0]
    chunk = min(PEER_TOK_CHUNK, n)
    n_pad = -(-n // chunk) * chunk
    hp = jnp.pad(h, ((0, n_pad - n), (0, 0)))

    def body(hc):
        q = (hc @ w_q).reshape(chunk, PEER_HEADS, 2, PEER_HALF)
        s = jnp.einsum("thpd,hpnd->thpn", q, sub_keys).astype(jnp.float32)
        sv, si = lax.top_k(s, PEER_TOPK)
        cand = (sv[:, :, 0, :, None] + sv[:, :, 1, None, :]).reshape(chunk, PEER_HEADS, PEER_TOPK * PEER_TOPK)
        cidx = (si[:, :, 0, :, None] * PEER_N_KEYS + si[:, :, 1, None, :]).reshape(chunk, PEER_HEADS, PEER_TOPK * PEER_TOPK)
        fv, fi = lax.top_k(cand, PEER_TOPK)
        eidx = jnp.take_along_axis(cidx, fi, axis=-1)
        g = jax.nn.softmax(fv, axis=-1)
        u = u_tab[eidx]
        act = jax.nn.gelu(jnp.einsum("td,thkd->thk", hc, u).astype(jnp.float32), approximate=False)
        v = v_tab[eidx]
        return jnp.einsum("thk,thkd->td", (g * act).astype(hc.dtype), v)

    out = lax.map(body, hp.reshape(n_pad // chunk, chunk, D_MODEL))
    return out.reshape(n_pad, D_MODEL)[:n]


def layer(x, c, pos, conv_prev, h_prev, attend, w):
    b, t, _ = x.shape
    sh1, sc1, gt1, sh2, sc2, gt2 = jnp.split(c @ w["w_ada"] + w["b_ada"], N_MOD, axis=-1)
    h = modulate(rms_norm(x, w["norm1_g"]), sh1, sc1)
    q, k, v, xl, gl, gb = jnp.split(h @ w["w_in"], IN_SPLITS, axis=-1)
    q = rope(q.reshape(b, t, N_HEADS, HEAD_DIM).transpose(0, 2, 1, 3), pos)
    k = rope(k.reshape(b, t, N_KV_HEADS, HEAD_DIM).transpose(0, 2, 1, 3), pos)
    vh = v.reshape(b, t, N_KV_HEADS, HEAD_DIM).transpose(0, 2, 1, 3)
    o_attn = attend(q, k, vh).transpose(0, 2, 1, 3).reshape(b, t, D_ATTN)
    x_ext = jnp.concatenate([conv_prev.astype(xl.dtype), xl], axis=1)
    xc = causal_conv(x_ext, w["conv_w"], w["conv_b"])
    hs, h_last = rg_lru(xc, h_prev, w["lru_wa"], w["lru_ba"], w["lru_wx"], w["lru_bx"], w["lru_lam"])
    o_lru = hs * jax.nn.gelu(gl, approximate=False)
    gates = jax.nn.sigmoid(gb.reshape(b, t, N_BRANCH, D_MODEL))
    merged = gates[:, :, 0] * (o_lru @ w["w_br_lru"]) + gates[:, :, 1] * (o_attn @ w["w_br_attn"])
    x = x + gt1[:, None, :] * (merged @ w["w_out"])
    h2 = modulate(rms_norm(x, w["norm2_g"]), sh2, sc2)
    ff = peer(h2.reshape(b * t, D_MODEL), w["peer_wq"], w["peer_subkeys"], w["peer_u"], w["peer_v"]).reshape(b, t, D_MODEL)
    x = x + gt2[:, None, :] * ff
    k_rows = k.transpose(0, 2, 1, 3)
    v_rows = v.reshape(b, t, N_KV_HEADS, HEAD_DIM)
    return x, k_rows, v_rows, h_last, x_ext[:, -(CONV_W - 1):]


def setup_inputs(seed: int = 0) -> dict:
    key = jax.random.key(seed)
    ks = iter(jax.random.split(key, 40))

    def nrm(shape, s):
        return jax.random.normal(next(ks), shape, jnp.float32) * s

    n_pages = PAST_LEN // PAGE_SIZE
    n_used = DEC_BATCH * n_pages
    n_pool = n_used + (n_used + 3) // 4
    perm = jax.random.permutation(next(ks), n_pool)
    page_table = perm[:n_used].reshape(DEC_BATCH, n_pages).astype(jnp.int32)
    u = jax.random.uniform(next(ks), (DEPTH, D_LRU), jnp.float32, 0.9, 0.999)
    root = u ** (1.0 / LRU_C)
    lru_lam = jnp.log(root) - jnp.log1p(-root)
    return {
        "x_prompt": nrm((BATCH, SEQ, D_MODEL), 1.0),
        "x_sample": nrm((DEC_BATCH, DEC_SEQ, D_MODEL), 1.0),
        "c_prompt": nrm((BATCH, D_MODEL), 1.0),
        "c_sample": nrm((DEC_BATCH, D_MODEL), 1.0),
        "cache_k": nrm((DEPTH, n_pool, PAGE_SIZE, N_KV_HEADS, HEAD_DIM), 1.0),
        "cache_v": nrm((DEPTH, n_pool, PAGE_SIZE, N_KV_HEADS, HEAD_DIM), 1.0),
        "state_lru": nrm((DEPTH, DEC_BATCH, D_LRU), 0.5),
        "state_conv": nrm((DEPTH, DEC_BATCH, CONV_W - 1, D_LRU), 1.0),
        "page_table": page_table,
        "norm1_g": 1.0 + nrm((DEPTH, D_MODEL), 0.02),
        "w_ada": nrm((DEPTH, D_MODEL, N_MOD * D_MODEL), 0.2 * D_MODEL ** -0.5),
        "b_ada": nrm((DEPTH, N_MOD * D_MODEL), 0.02),
        "w_in": nrm((DEPTH, D_MODEL, IN_COLS), D_MODEL ** -0.5),
        "conv_w": nrm((DEPTH, CONV_W, D_LRU), CONV_W ** -0.5),
        "conv_b": nrm((DEPTH, D_LRU), 0.02),
        "lru_wa": nrm((DEPTH, LRU_HEADS, LRU_BLOCK, LRU_BLOCK), LRU_BLOCK ** -0.5),
        "lru_ba": nrm((DEPTH, D_LRU), 0.02),
        "lru_wx": nrm((DEPTH, LRU_HEADS, LRU_BLOCK, LRU_BLOCK), LRU_BLOCK ** -0.5),
        "lru_bx": nrm((DEPTH, D_LRU), 0.02),
        "lru_lam": lru_lam,
        "w_br_lru": nrm((DEPTH, D_LRU, D_MODEL), D_LRU ** -0.5),
        "w_br_attn": nrm((DEPTH, D_ATTN, D_MODEL), D_ATTN ** -0.5),
        "w_out": nrm((DEPTH, D_MODEL, D_MODEL), D_MODEL ** -0.5),
        "norm2_g": 1.0 + nrm((DEPTH, D_MODEL), 0.02),
        "peer_wq": nrm((DEPTH, D_MODEL, PEER_HEADS * PEER_D_KEY), D_MODEL ** -0.5),
        "peer_subkeys": nrm((DEPTH, PEER_HEADS, 2, PEER_N_KEYS, PEER_HALF), PEER_HALF ** -0.5),
        "peer_u": nrm((DEPTH, PEER_N_EXPERTS, D_MODEL), D_MODEL ** -0.5),
        "peer_v": nrm((DEPTH, PEER_N_EXPERTS, D_MODEL), PEER_HEADS ** -0.5),
        "final_g": 1.0 + nrm((D_MODEL,), 0.02),
    }


def reference(x_prompt, x_sample, c_prompt, c_sample, cache_k, cache_v, state_lru, state_conv, page_table,
              norm1_g, w_ada, b_ada, w_in, conv_w, conv_b, lru_wa, lru_ba, lru_wx, lru_bx, lru_lam,
              w_br_lru, w_br_attn, w_out, norm2_g, peer_wq, peer_subkeys, peer_u, peer_v, final_g):
    b, s, _ = x_prompt.shape
    t_dec = x_sample.shape[1]
    past_len = page_table.shape[1] * PAGE_SIZE
    pos_p = jnp.arange(s, dtype=jnp.int32)
    pos_s = past_len + jnp.arange(t_dec, dtype=jnp.int32)
    xp, xs = x_prompt, x_sample
    kp_l, vp_l, hp_l, cp_l, ks_l, vs_l, hs_l, cs_l = [], [], [], [], [], [], [], []
    for l in range(DEPTH):
        w = {
            "norm1_g": norm1_g[l], "w_ada": w_ada[l], "b_ada": b_ada[l], "w_in": w_in[l],
            "conv_w": conv_w[l], "conv_b": conv_b[l], "lru_wa": lru_wa[l], "lru_ba": lru_ba[l],
            "lru_wx": lru_wx[l], "lru_bx": lru_bx[l], "lru_lam": lru_lam[l],
            "w_br_lru": w_br_lru[l], "w_br_attn": w_br_attn[l], "w_out": w_out[l], "norm2_g": norm2_g[l],
            "peer_wq": peer_wq[l], "peer_subkeys": peer_subkeys[l], "peer_u": peer_u[l], "peer_v": peer_v[l],
        }
        conv0 = jnp.zeros((b, CONV_W - 1, D_LRU), xp.dtype)
        h0 = jnp.zeros((b, D_LRU), xp.dtype)
        xp, kp, vp, hp, cp = layer(xp, c_prompt, pos_p, conv0, h0, moba_prompt, w)
        attend_s = functools.partial(moba_sample, past_k=gather_pages(cache_k[l], page_table),
                                     past_v=gather_pages(cache_v[l], page_table))
        xs, ksr, vsr, hsr, csr = layer(xs, c_sample, pos_s, state_conv[l], state_lru[l], attend_s, w)
        kp_l.append(kp); vp_l.append(vp); hp_l.append(hp); cp_l.append(cp)
        ks_l.append(ksr); vs_l.append(vsr); hs_l.append(hsr); cs_l.append(csr)
    y_prompt = rms_norm(xp, final_g)
    y_sample = rms_norm(xs, final_g)
    return (y_prompt, y_sample, jnp.stack(kp_l), jnp.stack(vp_l), jnp.stack(hp_l), jnp.stack(cp_l),
            jnp.stack(ks_l), jnp.stack(vs_l), jnp.stack(hs_l), jnp.stack(cs_l))
```

```python
import functools
import math

import jax
import jax.numpy as jnp
from jax import lax
from jax.experimental import pallas as pl
from jax.experimental.pallas import tpu as pltpu

D_MODEL = 2048
PAGE_SIZE = 128
N_HEADS = 16
N_KV_HEADS = 4
HEAD_DIM = 128
GROUP = N_HEADS // N_KV_HEADS
D_ATTN = N_HEADS * HEAD_DIM
D_KV = N_KV_HEADS * HEAD_DIM
ROPE_THETA = 10000.0
MOBA_BLOCK = 256
MOBA_TOPK = 3
D_LRU = D_MODEL
LRU_HEADS = 16
LRU_BLOCK = D_LRU // LRU_HEADS
CONV_W = 4
LRU_C = 8.0
PEER_HEADS = 8
PEER_N_KEYS = 128
PEER_HALF = 128
PEER_TOPK = 16
PEER_SEL = PEER_HEADS * PEER_TOPK
N_MOD = 6
EPS = 1e-6

COL_Q = 0
COL_K = D_ATTN
COL_V = D_ATTN + D_KV
COL_XL = D_ATTN + 2 * D_KV
COL_GL = COL_XL + D_LRU
COL_GB = COL_GL + D_LRU
IN_COLS = COL_GB + 2 * D_MODEL

VREG_SUBLANES = 8
LANES = 128
ROW_SUB = D_MODEL // LANES
NEG = -1e30
BF16 = jnp.bfloat16
F32 = jnp.float32
VMEM_LIMIT = 52 * 1024 * 1024


def _cparams(sem):
    return pltpu.CompilerParams(dimension_semantics=sem, vmem_limit_bytes=VMEM_LIMIT)


def _gelu(x):
    return 0.5 * x * (1.0 + lax.erf(x * (1.0 / math.sqrt(2.0))))


def _mod_spec(mod, tm, tn, tiled_cols=True):
    col = (lambda j: j) if tiled_cols else (lambda j: 0)
    if mod.shape[1] == 1:
        return pl.BlockSpec((1, 1, tn), lambda b, i, j: (b, 0, col(j)))
    return pl.BlockSpec((1, tm, tn), lambda b, i, j: (b, i, col(j)))


def _ada_kernel(c_ref, w_ref, b_ref, o_ref):
    o_ref[...] = jnp.dot(c_ref[...].astype(BF16), w_ref[...].astype(BF16),
                         preferred_element_type=F32) + b_ref[...]


def ada_project(c_all, w_ada, b_ada):
    m, d = c_all.shape
    n = w_ada.shape[1]
    tn = 1024
    return pl.pallas_call(
        _ada_kernel,
        out_shape=jax.ShapeDtypeStruct((m, n), F32),
        grid=(n // tn,),
        in_specs=[pl.BlockSpec((m, d), lambda j: (0, 0)),
                  pl.BlockSpec((d, tn), lambda j: (0, j)),
                  pl.BlockSpec((1, tn), lambda j: (0, j))],
        out_specs=pl.BlockSpec((m, tn), lambda j: (0, j)),
        compiler_params=_cparams(("arbitrary",)),
        name="ada_project",
    )(c_all, w_ada, b_ada.reshape(1, n))


def _norm_proj_kernel(x_ref, sh_ref, sc_ref, g_ref, w_ref, cos_ref, sin_ref, *rest, rope_tiles, emit_h):
    if emit_h:
        o_ref, h_ref, hs_ref = rest
    else:
        o_ref, hs_ref = rest
    j = pl.program_id(2)

    @pl.when(j == 0)
    def _():
        x = x_ref[0]
        y = x * lax.rsqrt(jnp.mean(x * x, axis=-1, keepdims=True) + EPS) * g_ref[...]
        h = y * (1.0 + sc_ref[0]) + sh_ref[0]
        hs_ref[...] = h.astype(BF16)
        if emit_h:
            h_ref[0] = h

    acc = jnp.dot(hs_ref[...], w_ref[...], preferred_element_type=F32)

    if rope_tiles:
        @pl.when(j < rope_tiles)
        def _():
            cos = cos_ref[...]
            sin = sin_ref[...]
            parts = []
            for hh in range(acc.shape[1] // HEAD_DIM):
                a = acc[:, hh * HEAD_DIM:(hh + 1) * HEAD_DIM]
                parts.append(a * cos + pltpu.roll(a, HEAD_DIM // 2, 1) * sin)
            o_ref[0] = jnp.concatenate(parts, axis=1)

        @pl.when(j >= rope_tiles)
        def _():
            o_ref[0] = acc
    else:
        o_ref[0] = acc


def norm_proj(x, sh, sc, g, w_bf, cos, sin, *, rope_cols, emit_h):
    b, s, d = x.shape
    n = w_bf.shape[1]
    tm = min(512 if emit_h else 1024, s)
    tn = 512
    assert s % tm == 0 and n % tn == 0 and rope_cols % tn == 0
    out_shape = [jax.ShapeDtypeStruct((b, s, n), F32)]
    out_specs = [pl.BlockSpec((1, tm, tn), lambda bb, i, j: (bb, i, j))]
    if emit_h:
        out_shape.append(jax.ShapeDtypeStruct((b, s, d), F32))
        out_specs.append(pl.BlockSpec((1, tm, d), lambda bb, i, j: (bb, i, 0)))
    res = pl.pallas_call(
        functools.partial(_norm_proj_kernel, rope_tiles=rope_cols // tn, emit_h=emit_h),
        out_shape=out_shape,
        grid=(b, s // tm, n // tn),
        in_specs=[pl.BlockSpec((1, tm, d), lambda bb, i, j: (bb, i, 0)),
                  _mod_spec(sh, tm, d, False), _mod_spec(sc, tm, d, False),
                  pl.BlockSpec((1, d), lambda bb, i, j: (0, 0)),
                  pl.BlockSpec((d, tn), lambda bb, i, j: (0, j)),
                  pl.BlockSpec((tm, HEAD_DIM), lambda bb, i, j: (i, 0)),
                  pl.BlockSpec((tm, HEAD_DIM), lambda bb, i, j: (i, 0))],
        out_specs=out_specs,
        scratch_shapes=[pltpu.VMEM((tm, d), BF16)],
        compiler_params=_cparams(("arbitrary", "arbitrary", "arbitrary")),
        name="norm_proj",
    )(x, sh, sc, g.reshape(1, d), w_bf, cos, sin)
    return res


def rope_tables(pos):
    half = HEAD_DIM // 2
    inv = jnp.exp(-math.log(ROPE_THETA) * jnp.arange(half, dtype=F32) / half)
    ang = pos.astype(F32)[:, None] * inv[None, :]
    cos, sin = jnp.cos(ang), jnp.sin(ang)
    return jnp.concatenate([cos, cos], axis=1), jnp.concatenate([-sin, sin], axis=1)


def _kmean_kernel(k_ref, o_ref):
    o_ref[0, 0] = jnp.mean(k_ref[0], axis=0, keepdims=True)


def moba_block_means(proj):
    b, s, _ = proj.shape
    nb = s // MOBA_BLOCK
    out = pl.pallas_call(
        _kmean_kernel,
        out_shape=jax.ShapeDtypeStruct((b, nb, 1, D_KV), F32),
        grid=(b, nb),
        in_specs=[pl.BlockSpec((1, MOBA_BLOCK, D_KV), lambda bb, i: (bb, i, COL_K // D_KV))],
        out_specs=pl.BlockSpec((1, 1, 1, D_KV), lambda bb, i: (bb, i, 0, 0)),
        compiler_params=_cparams(("arbitrary", "arbitrary")),
        name="moba_block_means",
    )(proj)
    return out.reshape(b, nb, D_KV)


def _top3_bias(gate, own_blk, nb):
    blk = lax.broadcasted_iota(jnp.int32, gate.shape, 0)
    past = blk < own_blk
    work = jnp.where(past, gate, -jnp.inf)
    sel = jnp.zeros(gate.shape, jnp.bool_)
    for _ in range(min(MOBA_TOPK, nb)):
        m = jnp.max(work, axis=0, keepdims=True)
        idx = jnp.min(jnp.where(work == m, blk, nb), axis=0, keepdims=True)
        pick = blk == idx
        sel = jnp.logical_or(sel, jnp.logical_and(pick, past))
        work = jnp.where(pick, -jnp.inf, work)
    return jnp.where(sel, 0.0, NEG).astype(F32)


def _moba_kernel(q_ref, k_ref, v_ref, km_ref, o_ref, kb_ref, vt_ref, bias_ref, m_ref, l_ref, acc_ref):
    qi = pl.program_id(2)
    nb = km_ref.shape[1]
    tq = MOBA_BLOCK
    rows = GROUP * tq

    @pl.when(qi == 0)
    def _():
        def cp(jb, c):
            st = pl.multiple_of(jb * MOBA_BLOCK, MOBA_BLOCK)
            kb_ref[pl.ds(st, MOBA_BLOCK), :] = k_ref[0, pl.ds(st, MOBA_BLOCK), :].astype(BF16)
            vt_ref[jb] = v_ref[0, pl.ds(st, MOBA_BLOCK), :].T.astype(BF16)
            return c
        lax.fori_loop(0, nb, cp, 0)

    q = q_ref[0]
    qcat = jnp.concatenate([q[:, g * HEAD_DIM:(g + 1) * HEAD_DIM] for g in range(GROUP)], axis=0)
    qt = qcat.T
    gate = jnp.dot(km_ref[0], qt, precision=lax.Precision.HIGHEST, preferred_element_type=F32)
    bias_ref[...] = _top3_bias(gate, qi, nb)
    qts = (qt * (HEAD_DIM ** -0.5)).astype(BF16)

    st = pl.multiple_of(qi * MOBA_BLOCK, MOBA_BLOCK)
    s = jnp.dot(kb_ref[pl.ds(st, MOBA_BLOCK), :], qts, preferred_element_type=F32)
    key_t = lax.broadcasted_iota(jnp.int32, (MOBA_BLOCK, rows), 0)
    q_t = lax.broadcasted_iota(jnp.int32, (MOBA_BLOCK, rows), 1) % tq
    s = jnp.where(key_t <= q_t, s, NEG)
    m0 = jnp.max(s, axis=0, keepdims=True)
    p = jnp.exp(s - m0)
    m_ref[...] = m0
    l_ref[...] = jnp.sum(p, axis=0, keepdims=True)
    acc_ref[...] = jnp.dot(vt_ref[qi], p.astype(BF16), preferred_element_type=F32)

    def body(jb, c):
        stj = pl.multiple_of(jb * MOBA_BLOCK, MOBA_BLOCK)
        sj = jnp.dot(kb_ref[pl.ds(stj, MOBA_BLOCK), :], qts, preferred_element_type=F32)
        sj = sj + bias_ref[pl.ds(jb, 1), :]
        m_old = m_ref[...]
        m_new = jnp.maximum(m_old, jnp.max(sj, axis=0, keepdims=True))
        alpha = jnp.exp(m_old - m_new)
        pj = jnp.exp(sj - m_new)
        l_ref[...] = alpha * l_ref[...] + jnp.sum(pj, axis=0, keepdims=True)
        acc_ref[...] = alpha * acc_ref[...] + jnp.dot(vt_ref[jb], pj.astype(BF16), preferred_element_type=F32)
        m_ref[...] = m_new
        return c

    lax.fori_loop(0, qi, body, 0)

    o = (acc_ref[...] / l_ref[...]).T
    for g in range(GROUP):
        o_ref[0, :, g * HEAD_DIM:(g + 1) * HEAD_DIM] = o[g * tq:(g + 1) * tq].astype(o_ref.dtype)


def moba_prompt_attention(proj, kmean):
    b, s, _ = proj.shape
    nb = s // MOBA_BLOCK
    rows = GROUP * MOBA_BLOCK
    gw = GROUP * HEAD_DIM
    return pl.pallas_call(
        _moba_kernel,
        out_shape=jax.ShapeDtypeStruct((b, s, D_ATTN), BF16),
        grid=(b, N_KV_HEADS, nb),
        in_specs=[pl.BlockSpec((1, MOBA_BLOCK, gw), lambda bb, hk, qi: (bb, qi, hk)),
                  pl.BlockSpec((1, s, HEAD_DIM), lambda bb, hk, qi: (bb, 0, COL_K // HEAD_DIM + hk)),
                  pl.BlockSpec((1, s, HEAD_DIM), lambda bb, hk, qi: (bb, 0, COL_V // HEAD_DIM + hk)),
                  pl.BlockSpec((1, nb, HEAD_DIM), lambda bb, hk, qi: (bb, 0, hk))],
        out_specs=pl.BlockSpec((1, MOBA_BLOCK, gw), lambda bb, hk, qi: (bb, qi, hk)),
        scratch_shapes=[pltpu.VMEM((s, HEAD_DIM), BF16),
                        pltpu.VMEM((nb, HEAD_DIM, MOBA_BLOCK), BF16),
                        pltpu.VMEM((nb, rows), F32),
                        pltpu.VMEM((1, rows), F32),
                        pltpu.VMEM((1, rows), F32),
                        pltpu.VMEM((HEAD_DIM, rows), F32)],
        compiler_params=_cparams(("arbitrary", "arbitrary", "arbitrary")),
        name="moba_prompt_attention",
    )(proj, proj, proj, kmean)


def _page_sum_kernel(pt_ref, *refs, pages_per_step):
    del pt_ref
    o_ref = refs[pages_per_step]
    p = pl.program_id(1)
    acc = None
    for r in range(0, pages_per_step, 2):
        sblk = jnp.sum(refs[r][0], axis=0, keepdims=True) + jnp.sum(refs[r + 1][0], axis=0, keepdims=True)
        sblk = sblk * (1.0 / MOBA_BLOCK)
        acc = sblk if acc is None else jnp.concatenate([acc, sblk], axis=0)
    del p
    o_ref[0, 0] = acc


def sample_block_means(cache_k2, page_table):
    db, n_pages = page_table.shape
    pps = 8 if n_pages % 8 == 0 else 2
    assert n_pages % pps == 0
    bps = pps // 2
    nbp = n_pages // 2

    def page_spec(r):
        return pl.BlockSpec((1, PAGE_SIZE, D_KV), lambda bb, p, pt: (pt[bb, p * pps + r], 0, 0))

    out = pl.pallas_call(
        functools.partial(_page_sum_kernel, pages_per_step=pps),
        out_shape=jax.ShapeDtypeStruct((db, n_pages // pps, bps, D_KV), F32),
        grid_spec=pltpu.PrefetchScalarGridSpec(
            num_scalar_prefetch=1,
            grid=(db, n_pages // pps),
            in_specs=[page_spec(r) for r in range(pps)],
            out_specs=pl.BlockSpec((1, 1, bps, D_KV), lambda bb, p, pt: (bb, p, 0, 0)),
        ),
        compiler_params=_cparams(("arbitrary", "arbitrary")),
        name="sample_block_means",
    )(page_table, *([cache_k2] * pps))
    return out.reshape(db, nbp, D_KV)


def _sample_select_kernel(q_ref, km_ref, o_ref):
    nbp = km_ref.shape[1]
    q = q_ref[0]
    head_kv = lax.broadcasted_iota(jnp.int32, (N_HEADS, nbp), 0) // GROUP
    gate = jnp.zeros((N_HEADS, nbp), F32)
    for hk in range(N_KV_HEADS):
        km = km_ref[0, :, hk * HEAD_DIM:(hk + 1) * HEAD_DIM]
        gk = lax.dot_general(q, km, (((1,), (1,)), ((), ())), precision=lax.Precision.HIGHEST,
                             preferred_element_type=F32)
        gate = jnp.where(head_kv == hk, gk, gate)
    blk = lax.broadcasted_iota(jnp.int32, gate.shape, 1)
    lane = lax.broadcasted_iota(jnp.int32, (N_HEADS, LANES), 1)
    out = jnp.zeros((N_HEADS, LANES), jnp.int32)
    work = gate
    for r in range(MOBA_TOPK):
        m = jnp.max(work, axis=1, keepdims=True)
        idx = jnp.min(jnp.where(work == m, blk, nbp), axis=1, keepdims=True)
        out = jnp.where(lane == r, idx, out)
        work = jnp.where(blk == idx, -jnp.inf, work)
    o_ref[0] = out


def sample_select(q_s, kmean_s):
    db = q_s.shape[0]
    nbp = kmean_s.shape[1]
    return pl.pallas_call(
        _sample_select_kernel,
        out_shape=jax.ShapeDtypeStruct((db, N_HEADS, LANES), jnp.int32),
        grid=(db,),
        in_specs=[pl.BlockSpec((1, N_HEADS, HEAD_DIM), lambda bb: (bb, 0, 0)),
                  pl.BlockSpec((1, nbp, D_KV), lambda bb: (bb, 0, 0))],
        out_specs=pl.BlockSpec((1, N_HEADS, LANES), lambda bb: (bb, 0, 0)),
        compiler_params=_cparams(("arbitrary",)),
        name="sample_select",
    )(q_s, kmean_s)


def _sample_attn_kernel(pt_ref, sel_ref, q_ref, kp_ref, vp_ref, kn_ref, vn_ref, o_ref, m_ref, l_ref, acc_ref):
    del pt_ref, sel_ref
    h = pl.program_id(1)
    sp = pl.program_id(2)
    hk = h // GROUP
    scale = HEAD_DIM ** -0.5
    qh = q_ref[0, pl.ds(h, 1), :] * scale
    q8 = jnp.broadcast_to(qh, (VREG_SUBLANES, HEAD_DIM))
    lane_kv = lax.broadcasted_iota(jnp.int32, (1, D_KV), 1) // HEAD_DIM

    @pl.when(sp == 0)
    def _():
        kn = kn_ref[0]
        s_all = jnp.where(lane_kv == hk, kn * jnp.concatenate([qh] * N_KV_HEADS, axis=1), 0.0)
        m_ref[...] = jnp.sum(s_all, axis=1, keepdims=True)
        l_ref[...] = jnp.ones_like(l_ref)
        acc_ref[...] = jnp.where(lane_kv == hk, vn_ref[0], 0.0)

    kp = kp_ref[0]
    s = lax.dot_general(q8.astype(BF16), kp.astype(BF16), (((1,), (1,)), ((), ())), preferred_element_type=F32)[0:1]
    row_kv = lax.broadcasted_iota(jnp.int32, s.shape, 1) % N_KV_HEADS
    s = jnp.where(row_kv == hk, s, NEG)
    m_old = m_ref[...]
    m_new = jnp.maximum(m_old, jnp.max(s, axis=1, keepdims=True))
    alpha = jnp.exp(m_old - m_new)
    p = jnp.exp(s - m_new)
    l_ref[...] = alpha * l_ref[...] + jnp.sum(p, axis=1, keepdims=True)
    p8 = jnp.broadcast_to(p, (VREG_SUBLANES, p.shape[1])).astype(BF16)
    pv = jnp.dot(p8, vp_ref[0].astype(BF16), preferred_element_type=F32)[0:1]
    acc_ref[...] = alpha * acc_ref[...] + jnp.where(lane_kv == hk, jnp.concatenate([pv] * N_KV_HEADS, axis=1), 0.0)
    m_ref[...] = m_new

    @pl.when(sp == pl.num_programs(2) - 1)
    def _():
        a = acc_ref[...] / l_ref[...]
        o = a[:, 0:HEAD_DIM]
        for kk in range(1, N_KV_HEADS):
            o = o + a[:, kk * HEAD_DIM:(kk + 1) * HEAD_DIM]
        o_ref[0, 0] = o


def sample_attention(q_s, k_new, v_new, cache_k3, cache_v3, page_table, sel):
    db = q_s.shape[0]
    pages_per_blk = MOBA_BLOCK // PAGE_SIZE
    steps = MOBA_TOPK * pages_per_blk
    prow = PAGE_SIZE * N_KV_HEADS

    def page_idx(bb, h, sp, pt, sl):
        return (pt[bb, sl[bb, h * MOBA_TOPK + sp // pages_per_blk] * pages_per_blk + sp % pages_per_blk], 0, 0)

    out = pl.pallas_call(
        _sample_attn_kernel,
        out_shape=jax.ShapeDtypeStruct((db, N_HEADS, 1, HEAD_DIM), F32),
        grid_spec=pltpu.PrefetchScalarGridSpec(
            num_scalar_prefetch=2,
            grid=(db, N_HEADS, steps),
            in_specs=[pl.BlockSpec((1, N_HEADS, HEAD_DIM), lambda bb, h, sp, pt, sl: (bb, 0, 0)),
                      pl.BlockSpec((1, prow, HEAD_DIM), page_idx),
                      pl.BlockSpec((1, prow, HEAD_DIM), page_idx),
                      pl.BlockSpec((1, 1, D_KV), lambda bb, h, sp, pt, sl: (bb, 0, 0)),
                      pl.BlockSpec((1, 1, D_KV), lambda bb, h, sp, pt, sl: (bb, 0, 0))],
            out_specs=pl.BlockSpec((1, 1, 1, HEAD_DIM), lambda bb, h, sp, pt, sl: (bb, h, 0, 0)),
            scratch_shapes=[pltpu.VMEM((1, 1), F32), pltpu.VMEM((1, 1), F32), pltpu.VMEM((1, D_KV), F32)],
        ),
        compiler_params=_cparams(("arbitrary", "arbitrary", "arbitrary")),
        name="sample_attention",
    )(page_table, sel, q_s, cache_k3, cache_v3, k_new, v_new)
    return out.reshape(db, D_ATTN)


def _lru_gates(xc, wa_ref, wx_ref, ba, bx, lam):
    nh = xc.shape[1] // LRU_BLOCK
    xb = xc.astype(BF16)
    ra, rx = [], []
    for hh in range(nh):
        xs = xb[:, hh * LRU_BLOCK:(hh + 1) * LRU_BLOCK]
        ra.append(jnp.dot(xs, wa_ref[hh], preferred_element_type=F32))
        rx.append(jnp.dot(xs, wx_ref[hh], preferred_element_type=F32))
    r = jax.nn.sigmoid(jnp.concatenate(ra, axis=1) + ba)
    gi = jax.nn.sigmoid(jnp.concatenate(rx, axis=1) + bx)
    log_a = (-LRU_C * r) * jax.nn.softplus(-lam)
    a = jnp.exp(log_a)
    bxs = jnp.sqrt(1.0 - jnp.exp(2.0 * log_a)) * (gi * xc)
    return a, bxs


def _lru_kernel(xl_ref, gl_ref, cprev_ref, h0_ref, cw_ref, cb_ref, wa_ref, wx_ref, ba_ref, bx_ref, lam_ref,
                o_ref, hl_ref, xprev_ref, hc_ref, a_ref, b_ref):
    ti = pl.program_id(2)
    tt = xl_ref.shape[1]
    sub = VREG_SUBLANES

    @pl.when(ti == 0)
    def _():
        xprev_ref[...] = cprev_ref[0]
        hc_ref[...] = jnp.broadcast_to(h0_ref[0], hc_ref.shape)

    x = xl_ref[0]
    cw = cw_ref[...]
    xc = cb_ref[...] + x * cw[CONV_W - 1:CONV_W]
    first = jnp.concatenate([xprev_ref[...], x[0:sub]], axis=0)
    xc_first = cb_ref[...] + x[0:sub] * cw[CONV_W - 1:CONV_W]
    for d in range(1, CONV_W):
        wj = cw[CONV_W - 1 - d:CONV_W - d]
        xc = xc + pltpu.roll(x, d, 0) * wj
        xc_first = xc_first + pltpu.roll(first, d, 0)[sub:2 * sub] * wj
    xc = jnp.concatenate([xc_first, xc[sub:]], axis=0)
    xprev_ref[...] = x[tt - sub:tt]

    a, bxs = _lru_gates(xc, wa_ref, wx_ref, ba_ref[...], bx_ref[...], lam_ref[...])

    rowm = lax.broadcasted_iota(jnp.int32, a.shape, 0) % sub
    for d in (1, 2, 4):
        ok = rowm >= d
        a_sh = pltpu.roll(a, d, 0)
        b_sh = pltpu.roll(bxs, d, 0)
        bxs = jnp.where(ok, a * b_sh + bxs, bxs)
        a = jnp.where(ok, a * a_sh, a)
    a_ref[...] = a
    b_ref[...] = bxs

    def grp(gidx, hprev):
        st = pl.multiple_of(gidx * sub, sub)
        hg = a_ref[pl.ds(st, sub), :] * hprev + b_ref[pl.ds(st, sub), :]
        b_ref[pl.ds(st, sub), :] = hg
        return jnp.broadcast_to(hg[sub - 1:sub], hprev.shape)

    hlast = lax.fori_loop(0, tt // sub, grp, hc_ref[...])
    hc_ref[...] = hlast
    o_ref[0] = (b_ref[...] * _gelu(gl_ref[0])).astype(o_ref.dtype)

    @pl.when(ti == pl.num_programs(2) - 1)
    def _():
        hl_ref[0] = hlast[0:1]


def lru_prompt(proj, conv_prev8, h0, conv_w, conv_b, wa_bf, wx_bf, ba, bx, lam):
    b, s, _ = proj.shape
    c = D_LRU
    tc = 512
    tt = min(512, s)
    assert s % tt == 0
    hpt = tc // LRU_BLOCK
    vec = lambda: pl.BlockSpec((1, tc), lambda bb, ci, ti: (0, ci))
    return pl.pallas_call(
        _lru_kernel,
        out_shape=[jax.ShapeDtypeStruct((b, s, c), BF16), jax.ShapeDtypeStruct((b, 1, c), F32)],
        grid=(b, c // tc, s // tt),
        in_specs=[pl.BlockSpec((1, tt, tc), lambda bb, ci, ti: (bb, ti, COL_XL // tc + ci)),
                  pl.BlockSpec((1, tt, tc), lambda bb, ci, ti: (bb, ti, COL_GL // tc + ci)),
                  pl.BlockSpec((1, VREG_SUBLANES, tc), lambda bb, ci, ti: (bb, 0, ci)),
                  pl.BlockSpec((1, 1, tc), lambda bb, ci, ti: (bb, 0, ci)),
                  pl.BlockSpec((CONV_W, tc), lambda bb, ci, ti: (0, ci)),
                  vec(),
                  pl.BlockSpec((hpt, LRU_BLOCK, LRU_BLOCK), lambda bb, ci, ti: (ci, 0, 0)),
                  pl.BlockSpec((hpt, LRU_BLOCK, LRU_BLOCK), lambda bb, ci, ti: (ci, 0, 0)),
                  vec(), vec(), vec()],
        out_specs=[pl.BlockSpec((1, tt, tc), lambda bb, ci, ti: (bb, ti, ci)),
                   pl.BlockSpec((1, 1, tc), lambda bb, ci, ti: (bb, 0, ci))],
        scratch_shapes=[pltpu.VMEM((VREG_SUBLANES, tc), F32), pltpu.VMEM((VREG_SUBLANES, tc), F32),
                        pltpu.VMEM((tt, tc), F32), pltpu.VMEM((tt, tc), F32)],
        compiler_params=_cparams(("arbitrary", "arbitrary", "arbitrary")),
        name="lru_prompt",
    )(proj, proj, conv_prev8, h0, conv_w, conv_b.reshape(1, c), wa_bf, wx_bf,
      ba.reshape(1, c), bx.reshape(1, c), lam.reshape(1, c))


def _lru_step_kernel(xl_ref, gl_ref, cprev_ref, h0_ref, cw_ref, cb_ref, wa_ref, wx_ref, ba_ref, bx_ref, lam_ref,
                     o_ref, hl_ref):
    x = xl_ref[0]
    cw = cw_ref[...]
    xc = cb_ref[...] + x * cw[CONV_W - 1:CONV_W]
    for j in range(CONV_W - 1):
        xc = xc + cprev_ref[j] * cw[j:j + 1]
    a, bxs = _lru_gates(xc, wa_ref, wx_ref, ba_ref[...], bx_ref[...], lam_ref[...])
    h = a * h0_ref[...] + bxs
    hl_ref[...] = h
    o_ref[...] = (h * _gelu(gl_ref[0])).astype(o_ref.dtype)


def lru_step(proj_s, conv_prev_t, h0, conv_w, conv_b, wa_bf, wx_bf, ba, bx, lam):
    _, db, _ = proj_s.shape
    c = D_LRU
    tc = 1024
    hpt = tc // LRU_BLOCK
    vec = lambda: pl.BlockSpec((1, tc), lambda ci: (0, ci))
    return pl.pallas_call(
        _lru_step_kernel,
        out_shape=[jax.ShapeDtypeStruct((db, c), BF16), jax.ShapeDtypeStruct((db, c), F32)],
        grid=(c // tc,),
        in_specs=[pl.BlockSpec((1, db, tc), lambda ci: (0, 0, COL_XL // tc + ci)),
                  pl.BlockSpec((1, db, tc), lambda ci: (0, 0, COL_GL // tc + ci)),
                  pl.BlockSpec((CONV_W - 1, db, tc), lambda ci: (0, 0, ci)),
                  pl.BlockSpec((db, tc), lambda ci: (0, ci)),
                  pl.BlockSpec((CONV_W, tc), lambda ci: (0, ci)),
                  vec(),
                  pl.BlockSpec((hpt, LRU_BLOCK, LRU_BLOCK), lambda ci: (ci, 0, 0)),
                  pl.BlockSpec((hpt, LRU_BLOCK, LRU_BLOCK), lambda ci: (ci, 0, 0)),
                  vec(), vec(), vec()],
        out_specs=[pl.BlockSpec((db, tc), lambda ci: (0, ci)), pl.BlockSpec((db, tc), lambda ci: (0, ci))],
        compiler_params=_cparams(("arbitrary",)),
        name="lru_step",
    )(proj_s, proj_s, conv_prev_t, h0, conv_w, conv_b.reshape(1, c), wa_bf, wx_bf,
      ba.reshape(1, c), bx.reshape(1, c), lam.reshape(1, c))


def _merge_kernel(ol_ref, oa_ref, g0_ref, g1_ref, wl_ref, wa_ref, o_ref):
    yl = jnp.dot(ol_ref[0], wl_ref[...], preferred_element_type=F32)
    ya = jnp.dot(oa_ref[0], wa_ref[...], preferred_element_type=F32)
    o_ref[0] = (jax.nn.sigmoid(g0_ref[0]) * yl + jax.nn.sigmoid(g1_ref[0]) * ya).astype(o_ref.dtype)


def branch_merge(o_lru, o_attn, proj, wl_bf, wa_bf):
    b, s, d = o_lru.shape
    tm = min(1024, s)
    tn = 512
    return pl.pallas_call(
        _merge_kernel,
        out_shape=jax.ShapeDtypeStruct((b, s, d), BF16),
        grid=(b, s // tm, d // tn),
        in_specs=[pl.BlockSpec((1, tm, d), lambda bb, i, j: (bb, i, 0)),
                  pl.BlockSpec((1, tm, d), lambda bb, i, j: (bb, i, 0)),
                  pl.BlockSpec((1, tm, tn), lambda bb, i, j: (bb, i, COL_GB // tn + j)),
                  pl.BlockSpec((1, tm, tn), lambda bb, i, j: (bb, i, (COL_GB + D_MODEL) // tn + j)),
                  pl.BlockSpec((d, tn), lambda bb, i, j: (0, j)),
                  pl.BlockSpec((d, tn), lambda bb, i, j: (0, j))],
        out_specs=pl.BlockSpec((1, tm, tn), lambda bb, i, j: (bb, i, j)),
        compiler_params=_cparams(("arbitrary", "arbitrary", "arbitrary")),
        name="branch_merge",
    )(o_lru, o_attn, proj, proj, wl_bf, wa_bf)


def _out_proj_kernel(m_ref, x_ref, gt_ref, w_ref, o_ref):
    o_ref[0] = x_ref[0] + gt_ref[0] * jnp.dot(m_ref[0], w_ref[...], preferred_element_type=F32)


def out_project(merged, x, gt, w_bf):
    b, s, d = x.shape
    tm = min(1024, s)
    tn = 512
    return pl.pallas_call(
        _out_proj_kernel,
        out_shape=jax.ShapeDtypeStruct((b, s, d), F32),
        grid=(b, s // tm, d // tn),
        in_specs=[pl.BlockSpec((1, tm, d), lambda bb, i, j: (bb, i, 0)),
                  pl.BlockSpec((1, tm, tn), lambda bb, i, j: (bb, i, j)),
                  _mod_spec(gt, tm, tn),
                  pl.BlockSpec((d, tn), lambda bb, i, j: (0, j))],
        out_specs=pl.BlockSpec((1, tm, tn), lambda bb, i, j: (bb, i, j)),
        compiler_params=_cparams(("arbitrary", "arbitrary", "arbitrary")),
        name="out_project",
    )(merged, x, gt, w_bf)


def _topk_rows(s, k, n):
    row = lax.broadcasted_iota(jnp.int32, s.shape, 0)
    slot = lax.broadcasted_iota(jnp.int32, (k, s.shape[1]), 0)
    vals = jnp.zeros((k, s.shape[1]), F32)
    idxs = jnp.zeros((k, s.shape[1]), jnp.int32)
    for r in range(k):
        m = jnp.max(s, axis=0, keepdims=True)
        am = jnp.min(jnp.where(s == m, row, n), axis=0, keepdims=True)
        vals = jnp.where(slot == r, m, vals)
        idxs = jnp.where(slot == r, am, idxs)
        s = jnp.where(row == am, -jnp.inf, s)
    return vals, idxs


def _peer_topk_kernel(q_ref, sk_ref, e_ref, g_ref):
    tt = q_ref.shape[0]
    kk = PEER_TOPK
    for hh in range(PEER_HEADS):
        sv, si = [], []
        for p in range(2):
            c0 = (hh * 2 + p) * PEER_HALF
            qc = q_ref[:, c0:c0 + PEER_HALF]
            s = lax.dot_general(sk_ref[hh, p], qc, (((1,), (1,)), ((), ())), precision=lax.Precision.HIGHEST,
                                preferred_element_type=F32)
            v, i = _topk_rows(s, kk, PEER_N_KEYS)
            sv.append(v)
            si.append(i)
        cand = jnp.concatenate([sv[0][a:a + 1] + sv[1] for a in range(kk)], axis=0)
        cidx = jnp.concatenate([si[0][a:a + 1] * PEER_N_KEYS + si[1] for a in range(kk)], axis=0)
        row = lax.broadcasted_iota(jnp.int32, cand.shape, 0)
        slot = lax.broadcasted_iota(jnp.int32, (kk, tt), 0)
        fv = jnp.zeros((kk, tt), F32)
        eid = jnp.zeros((kk, tt), jnp.int32)
        for r in range(kk):
            m = jnp.max(cand, axis=0, keepdims=True)
            am = jnp.min(jnp.where(cand == m, row, kk * kk), axis=0, keepdims=True)
            pick = row == am
            fv = jnp.where(slot == r, m, fv)
            eid = jnp.where(slot == r, jnp.max(jnp.where(pick, cidx, -1), axis=0, keepdims=True), eid)
            cand = jnp.where(pick, -jnp.inf, cand)
        ex = jnp.exp(fv - fv[0:1])
        g_ref[hh * kk:(hh + 1) * kk, :] = ex / jnp.sum(ex, axis=0, keepdims=True)
        e_ref[hh * kk:(hh + 1) * kk, :] = eid


def peer_topk(qp, sub_keys):
    n = qp.shape[0]
    tt = 256 if n % 256 == 0 else n
    return pl.pallas_call(
        _peer_topk_kernel,
        out_shape=[jax.ShapeDtypeStruct((PEER_SEL, n), jnp.int32), jax.ShapeDtypeStruct((PEER_SEL, n), F32)],
        grid=(n // tt,),
        in_specs=[pl.BlockSpec((tt, qp.shape[1]), lambda i: (i, 0)),
                  pl.BlockSpec(sub_keys.shape, lambda i: (0, 0, 0, 0))],
        out_specs=[pl.BlockSpec((PEER_SEL, tt), lambda i: (0, i)), pl.BlockSpec((PEER_SEL, tt), lambda i: (0, i))],
        compiler_params=_cparams(("arbitrary",)),
        name="peer_topk",
    )(qp, sub_keys)


PEER_TOK_TILE = 8


def _peer_gather_kernel(e_ref, h_ref, g_ref, u_hbm, v_hbm, o_ref, ubuf, vbuf, r_ref, wb_ref, sem):
    nrow = PEER_TOK_TILE * PEER_SEL

    def issue(r, c):
        t = r // PEER_SEL
        k = r % PEER_SEL
        e = e_ref[t, k]
        pltpu.make_async_copy(u_hbm.at[e], ubuf.at[r], sem.at[0]).start()
        pltpu.make_async_copy(v_hbm.at[e], vbuf.at[r], sem.at[1]).start()
        return c

    lax.fori_loop(0, nrow, issue, 0, unroll=8)
    pltpu.make_async_copy(u_hbm.at[pl.ds(0, nrow)], ubuf, sem.at[0]).wait()

    eye = (lax.broadcasted_iota(jnp.int32, (PEER_SEL, PEER_SEL), 0)
           == lax.broadcasted_iota(jnp.int32, (PEER_SEL, PEER_SEL), 1))
    for t in range(PEER_TOK_TILE):
        ht = h_ref[t]

        def dots(k, c):
            pr = ubuf[t * PEER_SEL + k] * ht
            p = pr[0:VREG_SUBLANES] + pr[VREG_SUBLANES:2 * VREG_SUBLANES]
            r_ref[pl.ds(k, 1), :] = jnp.sum(p, axis=0, keepdims=True)
            return c

        lax.fori_loop(0, PEER_SEL, dots, 0, unroll=8)
        act = jnp.sum(r_ref[...], axis=1, keepdims=True)
        gcol = jnp.sum(jnp.where(eye, g_ref[pl.ds(t, 1), :], 0.0), axis=1, keepdims=True)
        w = gcol * _gelu(act)
        wb_ref[t] = jnp.broadcast_to(w, (PEER_SEL, LANES))

    pltpu.make_async_copy(v_hbm.at[pl.ds(0, nrow)], vbuf, sem.at[1]).wait()
    for t in range(PEER_TOK_TILE):
        def wsum(k, acc):
            wk = wb_ref[t, pl.ds(k, 1), :]
            return acc + wk * vbuf[t * PEER_SEL + k]

        o_ref[t] = lax.fori_loop(0, PEER_SEL, wsum, jnp.zeros((ROW_SUB, LANES), F32), unroll=8)


def peer_gather(eidx, h3, gates, u3, v3):
    n = eidx.shape[0]
    tk = PEER_TOK_TILE
    nrow = tk * PEER_SEL
    return pl.pallas_call(
        _peer_gather_kernel,
        out_shape=jax.ShapeDtypeStruct((n, ROW_SUB, LANES), F32),
        grid=(n // tk,),
        in_specs=[pl.BlockSpec((tk, PEER_SEL), lambda i: (i, 0), memory_space=pltpu.SMEM),
                  pl.BlockSpec((tk, ROW_SUB, LANES), lambda i: (i, 0, 0)),
                  pl.BlockSpec((tk, PEER_SEL), lambda i: (i, 0)),
                  pl.BlockSpec(memory_space=pl.ANY),
                  pl.BlockSpec(memory_space=pl.ANY)],
        out_specs=pl.BlockSpec((tk, ROW_SUB, LANES), lambda i: (i, 0, 0)),
        scratch_shapes=[pltpu.VMEM((nrow, ROW_SUB, LANES), F32), pltpu.VMEM((nrow, ROW_SUB, LANES), F32),
                        pltpu.VMEM((PEER_SEL, LANES), F32), pltpu.VMEM((tk, PEER_SEL, LANES), F32),
                        pltpu.SemaphoreType.DMA((2,))],
        compiler_params=_cparams(("arbitrary",)),
        name="peer_gather",
    )(eidx, h3, gates, u3, v3)


def _final_kernel(x_ref, ff_ref, gt_ref, g_ref, o_ref):
    x = x_ref[0] + gt_ref[0] * ff_ref[0]
    o_ref[0] = x * lax.rsqrt(jnp.mean(x * x, axis=-1, keepdims=True) + EPS) * g_ref[...]


def final_norm(x1, ff, gt, g):
    b, s, d = x1.shape
    tm = min(512, s)
    gspec = (pl.BlockSpec((1, 1, d), lambda bb, i: (bb, 0, 0)) if gt.shape[1] == 1
             else pl.BlockSpec((1, tm, d), lambda bb, i: (bb, i, 0)))
    return pl.pallas_call(
        _final_kernel,
        out_shape=jax.ShapeDtypeStruct((b, s, d), F32),
        grid=(b, s // tm),
        in_specs=[pl.BlockSpec((1, tm, d), lambda bb, i: (bb, i, 0)),
                  pl.BlockSpec((1, tm, d), lambda bb, i: (bb, i, 0)),
                  gspec,
                  pl.BlockSpec((1, d), lambda bb, i: (0, 0))],
        out_specs=pl.BlockSpec((1, tm, d), lambda bb, i: (bb, i, 0)),
        compiler_params=_cparams(("arbitrary", "arbitrary")),
        name="final_norm",
    )(x1, ff, gt, g.reshape(1, d))


def _peer_block(x1, sh2, sc2, gt2, norm2_g, wq_bf, sub_keys, u3, v3, final_g, cos, sin):
    b, s, d = x1.shape
    n = b * s
    qp, h2 = norm_proj(x1, sh2, sc2, norm2_g, wq_bf, cos, sin, rope_cols=0, emit_h=True)
    qp2 = qp.reshape(n, -1)
    n_pad = -(-n // LANES) * LANES
    if n_pad != n:
        qp2 = jnp.pad(qp2, ((0, n_pad - n), (0, 0)))
    e_t, g_t = peer_topk(qp2, sub_keys)
    eidx = e_t.T[:n]
    gates = g_t.T[:n]
    ff = peer_gather(eidx, h2.reshape(n, ROW_SUB, LANES), gates, u3, v3)
    return final_norm(x1, ff.reshape(b, s, d), gt2, final_g)


def kernel(x_prompt, x_sample, c_prompt, c_sample, cache_k, cache_v, state_lru, state_conv, page_table, norm1_g, w_ada, b_ada, w_in, conv_w, conv_b, lru_wa, lru_ba, lru_wx, lru_bx, lru_lam, w_br_lru, w_br_attn, w_out, norm2_g, peer_wq, peer_subkeys, peer_u, peer_v, final_g):
    assert w_ada.shape[0] == 1, "single layer"
    b, s, d = x_prompt.shape
    db = x_sample.shape[0]
    n_pages = page_table.shape[1]
    past_len = n_pages * PAGE_SIZE
    assert s % MOBA_BLOCK == 0 and past_len % MOBA_BLOCK == 0 and x_sample.shape[1] == 1

    w_in_bf = w_in[0].astype(BF16)
    wl_bf, wa_bf, wo_bf = w_br_lru[0].astype(BF16), w_br_attn[0].astype(BF16), w_out[0].astype(BF16)
    wq_bf = peer_wq[0].astype(BF16)
    lwa_bf, lwx_bf = lru_wa[0].astype(BF16), lru_wx[0].astype(BF16)
    u3 = peer_u[0].reshape(-1, ROW_SUB, LANES)
    v3 = peer_v[0].reshape(-1, ROW_SUB, LANES)

    mod = ada_project(jnp.concatenate([c_prompt, c_sample], axis=0), w_ada[0], b_ada[0])
    mod_p = [m.reshape(b, 1, d) for m in jnp.split(mod[:b], N_MOD, axis=-1)]
    mod_s = [m.reshape(1, db, d) for m in jnp.split(mod[b:], N_MOD, axis=-1)]

    cos_p, sin_p = rope_tables(jnp.arange(s, dtype=jnp.int32))
    (proj,) = norm_proj(x_prompt, mod_p[0], mod_p[1], norm1_g[0], w_in_bf, cos_p, sin_p,
                        rope_cols=D_ATTN + D_KV, emit_h=False)
    kmean = moba_block_means(proj)
    o_attn = moba_prompt_attention(proj, kmean)
    o_lru, h_last_p = lru_prompt(proj, jnp.zeros((b, VREG_SUBLANES, D_LRU), F32), jnp.zeros((b, 1, D_LRU), F32),
                                 conv_w[0], conv_b[0], lwa_bf, lwx_bf, lru_ba[0], lru_bx[0], lru_lam[0])
    merged = branch_merge(o_lru, o_attn, proj, wl_bf, wa_bf)
    x1 = out_project(merged, x_prompt, mod_p[2], wo_bf)
    y_prompt = _peer_block(x1, mod_p[3], mod_p[4], mod_p[5], norm2_g[0], wq_bf, peer_subkeys[0], u3, v3,
                           final_g, cos_p, sin_p)
    k_prompt = proj[:, :, COL_K:COL_K + D_KV].reshape(1, b, s, N_KV_HEADS, HEAD_DIM)
    v_prompt = proj[:, :, COL_V:COL_V + D_KV].reshape(1, b, s, N_KV_HEADS, HEAD_DIM)
    conv_prompt = proj[:, s - (CONV_W - 1):, COL_XL:COL_XL + D_LRU].reshape(1, b, CONV_W - 1, D_LRU)

    xs = x_sample.reshape(1, db, d)
    cos_s, sin_s = rope_tables(jnp.full((db,), past_len, jnp.int32))
    (proj_s,) = norm_proj(xs, mod_s[0], mod_s[1], norm1_g[0], w_in_bf, cos_s, sin_s,
                          rope_cols=D_ATTN + D_KV, emit_h=False)
    q_s = proj_s[0, :, COL_Q:COL_Q + D_ATTN].reshape(db, N_HEADS, HEAD_DIM)
    k_new = proj_s[0, :, COL_K:COL_K + D_KV].reshape(db, 1, D_KV)
    v_new = proj_s[0, :, COL_V:COL_V + D_KV].reshape(db, 1, D_KV)
    n_pool = cache_k.shape[1]
    kmean_s = sample_block_means(cache_k[0].reshape(n_pool, PAGE_SIZE, D_KV), page_table)
    sel = sample_select(q_s, kmean_s)[:, :, :MOBA_TOPK].reshape(db, N_HEADS * MOBA_TOPK)
    o_attn_s = sample_attention(q_s, k_new, v_new,
                                cache_k[0].reshape(n_pool, PAGE_SIZE * N_KV_HEADS, HEAD_DIM),
                                cache_v[0].reshape(n_pool, PAGE_SIZE * N_KV_HEADS, HEAD_DIM),
                                page_table, sel)
    o_lru_s, h_last_s = lru_step(proj_s, jnp.transpose(state_conv[0], (1, 0, 2)), state_lru[0],
                                 conv_w[0], conv_b[0], lwa_bf, lwx_bf, lru_ba[0], lru_bx[0], lru_lam[0])
    merged_s = branch_merge(o_lru_s.reshape(1, db, d), o_attn_s.astype(BF16).reshape(1, db, d), proj_s, wl_bf, wa_bf)
    x1_s = out_project(merged_s, xs, mod_s[2], wo_bf)
    y_sample = _peer_block(x1_s, mod_s[3], mod_s[4], mod_s[5], norm2_g[0], wq_bf, peer_subkeys[0], u3, v3,
                           final_g, cos_s, sin_s)
    xl_s = proj_s[0, :, COL_XL:COL_XL + D_LRU]
    conv_sample = jnp.concatenate([state_conv[0][:, 1:], xl_s[:, None, :]], axis=1)[None]

    return (y_prompt, y_sample.reshape(db, 1, d), k_prompt, v_prompt,
            h_last_p.reshape(1, b, D_LRU), conv_prompt,
            k_new.reshape(1, db, 1, N_KV_HEADS, HEAD_DIM), v_new.reshape(1, db, 1, N_KV_HEADS, HEAD_DIM),
            h_last_s.reshape(1, db, D_LRU), conv_sample)
```

```python
import functools
import math

import jax
import jax.numpy as jnp
from jax import lax
from jax.experimental import pallas as pl
from jax.experimental.pallas import tpu as pltpu

D_MODEL = 2048
PAGE_SIZE = 128
N_HEADS = 16
N_KV_HEADS = 4
HEAD_DIM = 128
GROUP = N_HEADS // N_KV_HEADS
D_ATTN = N_HEADS * HEAD_DIM
D_KV = N_KV_HEADS * HEAD_DIM
ROPE_THETA = 10000.0
MOBA_BLOCK = 256
MOBA_TOPK = 3
D_LRU = D_MODEL
LRU_HEADS = 16
LRU_BLOCK = D_LRU // LRU_HEADS
CONV_W = 4
LRU_C = 8.0
PEER_HEADS = 8
PEER_N_KEYS = 128
PEER_HALF = 128
PEER_TOPK = 16
PEER_SEL = PEER_HEADS * PEER_TOPK
N_MOD = 6
EPS = 1e-6

COL_Q = 0
COL_K = D_ATTN
COL_V = D_ATTN + D_KV
COL_XL = D_ATTN + 2 * D_KV
COL_GL = COL_XL + D_LRU
COL_GB = COL_GL + D_LRU
IN_COLS = COL_GB + 2 * D_MODEL

VREG_SUBLANES = 8
LANES = 128
ROW_SUB = D_MODEL // LANES
NEG = -1e30
BF16 = jnp.bfloat16
F32 = jnp.float32
VMEM_LIMIT = 52 * 1024 * 1024


def _cparams(sem):
    return pltpu.CompilerParams(dimension_semantics=sem, vmem_limit_bytes=VMEM_LIMIT)


def _gelu(x):
    return 0.5 * x * (1.0 + lax.erf(x * (1.0 / math.sqrt(2.0))))


def _mod_spec(mod, tm, tn, tiled_cols=True):
    col = (lambda j: j) if tiled_cols else (lambda j: 0)
    if mod.shape[1] == 1:
        return pl.BlockSpec((1, 1, tn), lambda b, i, j: (b, 0, col(j)))
    return pl.BlockSpec((1, tm, tn), lambda b, i, j: (b, i, col(j)))


def _ada_kernel(c_ref, w_ref, b_ref, o_ref):
    o_ref[...] = jnp.dot(c_ref[...].astype(BF16), w_ref[...].astype(BF16),
                         preferred_element_type=F32) + b_ref[...]


def ada_project(c_all, w_ada, b_ada):
    m, d = c_all.shape
    n = w_ada.shape[1]
    tn = 1024
    return pl.pallas_call(
        _ada_kernel,
        out_shape=jax.ShapeDtypeStruct((m, n), F32),
        grid=(n // tn,),
        in_specs=[pl.BlockSpec((m, d), lambda j: (0, 0)),
                  pl.BlockSpec((d, tn), lambda j: (0, j)),
                  pl.BlockSpec((1, tn), lambda j: (0, j))],
        out_specs=pl.BlockSpec((m, tn), lambda j: (0, j)),
        compiler_params=_cparams(("arbitrary",)),
        name="ada_project",
    )(c_all, w_ada, b_ada.reshape(1, n))


def _norm_proj_kernel(x_ref, sh_ref, sc_ref, g_ref, w_ref, cos_ref, sin_ref, *rest, rope_tiles, emit_h):
    if emit_h:
        o_ref, h_ref, hs_ref = rest
    else:
        o_ref, hs_ref = rest
    j = pl.program_id(2)

    @pl.when(j == 0)
    def _():
        x = x_ref[0]
        y = x * lax.rsqrt(jnp.mean(x * x, axis=-1, keepdims=True) + EPS) * g_ref[...]
        h = y * (1.0 + sc_ref[0]) + sh_ref[0]
        hs_ref[...] = h.astype(BF16)
        if emit_h:
            h_ref[0] = h

    acc = jnp.dot(hs_ref[...], w_ref[...], preferred_element_type=F32)

    if rope_tiles:
        @pl.when(j < rope_tiles)
        def _():
            cos = cos_ref[...]
            sin = sin_ref[...]
            parts = []
            for hh in range(acc.shape[1] // HEAD_DIM):
                a = acc[:, hh * HEAD_DIM:(hh + 1) * HEAD_DIM]
                parts.append(a * cos + pltpu.roll(a, HEAD_DIM // 2, 1) * sin)
            o_ref[0] = jnp.concatenate(parts, axis=1)

        @pl.when(j >= rope_tiles)
        def _():
            o_ref[0] = acc
    else:
        o_ref[0] = acc


def norm_proj(x, sh, sc, g, w_bf, cos, sin, *, rope_cols, emit_h):
    b, s, d = x.shape
    n = w_bf.shape[1]
    tm = min(512 if emit_h else 1024, s)
    tn = 512
    assert s % tm == 0 and n % tn == 0 and rope_cols % tn == 0
    out_shape = [jax.ShapeDtypeStruct((b, s, n), F32)]
    out_specs = [pl.BlockSpec((1, tm, tn), lambda bb, i, j: (bb, i, j))]
    if emit_h:
        out_shape.append(jax.ShapeDtypeStruct((b, s, d), F32))
        out_specs.append(pl.BlockSpec((1, tm, d), lambda bb, i, j: (bb, i, 0)))
    res = pl.pallas_call(
        functools.partial(_norm_proj_kernel, rope_tiles=rope_cols // tn, emit_h=emit_h),
        out_shape=out_shape,
        grid=(b, s // tm, n // tn),
        in_specs=[pl.BlockSpec((1, tm, d), lambda bb, i, j: (bb, i, 0)),
                  _mod_spec(sh, tm, d, False), _mod_spec(sc, tm, d, False),
                  pl.BlockSpec((1, d), lambda bb, i, j: (0, 0)),
                  pl.BlockSpec((d, tn), lambda bb, i, j: (0, j)),
                  pl.BlockSpec((tm, HEAD_DIM), lambda bb, i, j: (i, 0)),
                  pl.BlockSpec((tm, HEAD_DIM), lambda bb, i, j: (i, 0))],
        out_specs=out_specs,
        scratch_shapes=[pltpu.VMEM((tm, d), BF16)],
        compiler_params=_cparams(("arbitrary", "arbitrary", "arbitrary")),
        name="norm_proj",
    )(x, sh, sc, g.reshape(1, d), w_bf, cos, sin)
    return res


def rope_tables(pos):
    half = HEAD_DIM // 2
    inv = jnp.exp(-math.log(ROPE_THETA) * jnp.arange(half, dtype=F32) / half)
    ang = pos.astype(F32)[:, None] * inv[None, :]
    cos, sin = jnp.cos(ang), jnp.sin(ang)
    return jnp.concatenate([cos, cos], axis=1), jnp.concatenate([-sin, sin], axis=1)


def _kmean_kernel(k_ref, o_ref):
    o_ref[0, 0] = jnp.mean(k_ref[0], axis=0, keepdims=True)


def moba_block_means(proj):
    b, s, _ = proj.shape
    nb = s // MOBA_BLOCK
    out = pl.pallas_call(
        _kmean_kernel,
        out_shape=jax.ShapeDtypeStruct((b, nb, 1, D_KV), F32),
        grid=(b, nb),
        in_specs=[pl.BlockSpec((1, MOBA_BLOCK, D_KV), lambda bb, i: (bb, i, COL_K // D_KV))],
        out_specs=pl.BlockSpec((1, 1, 1, D_KV), lambda bb, i: (bb, i, 0, 0)),
        compiler_params=_cparams(("arbitrary", "arbitrary")),
        name="moba_block_means",
    )(proj)
    return out.reshape(b, nb, D_KV)


def _top3_bias(gate, own_blk, nb):
    blk = lax.broadcasted_iota(jnp.int32, gate.shape, 0)
    past = blk < own_blk
    work = jnp.where(past, gate, -jnp.inf)
    sel = jnp.zeros(gate.shape, jnp.bool_)
    for _ in range(min(MOBA_TOPK, nb)):
        m = jnp.max(work, axis=0, keepdims=True)
        idx = jnp.min(jnp.where(work == m, blk, nb), axis=0, keepdims=True)
        pick = blk == idx
        sel = jnp.logical_or(sel, jnp.logical_and(pick, past))
        work = jnp.where(pick, -jnp.inf, work)
    return jnp.where(sel, 0.0, NEG).astype(F32)


def _moba_kernel(q_ref, k_ref, v_ref, km_ref, o_ref, kb_ref, vt_ref, bias_ref, m_ref, l_ref, acc_ref):
    qi = pl.program_id(2)
    nb = km_ref.shape[1]
    tq = MOBA_BLOCK
    rows = GROUP * tq

    @pl.when(qi == 0)
    def _():
        def cp(jb, c):
            st = pl.multiple_of(jb * MOBA_BLOCK, MOBA_BLOCK)
            kb_ref[pl.ds(st, MOBA_BLOCK), :] = k_ref[0, pl.ds(st, MOBA_BLOCK), :].astype(BF16)
            vt_ref[jb] = v_ref[0, pl.ds(st, MOBA_BLOCK), :].T.astype(BF16)
            return c
        lax.fori_loop(0, nb, cp, 0)

    q = q_ref[0]
    qcat = jnp.concatenate([q[:, g * HEAD_DIM:(g + 1) * HEAD_DIM] for g in range(GROUP)], axis=0)
    qt = qcat.T
    gate = jnp.dot(km_ref[0], qt, precision=lax.Precision.HIGHEST, preferred_element_type=F32)
    bias_ref[...] = _top3_bias(gate, qi, nb)
    qts = (qt * (HEAD_DIM ** -0.5)).astype(BF16)

    st = pl.multiple_of(qi * MOBA_BLOCK, MOBA_BLOCK)
    s = jnp.dot(kb_ref[pl.ds(st, MOBA_BLOCK), :], qts, preferred_element_type=F32)
    key_t = lax.broadcasted_iota(jnp.int32, (MOBA_BLOCK, rows), 0)
    q_t = lax.broadcasted_iota(jnp.int32, (MOBA_BLOCK, rows), 1) % tq
    s = jnp.where(key_t <= q_t, s, NEG)
    m0 = jnp.max(s, axis=0, keepdims=True)
    p = jnp.exp(s - m0)
    m_ref[...] = m0
    l_ref[...] = jnp.sum(p, axis=0, keepdims=True)
    acc_ref[...] = jnp.dot(vt_ref[qi], p.astype(BF16), preferred_element_type=F32)

    def past_blocks(j0, nblk):
        stj = pl.multiple_of(j0 * MOBA_BLOCK, MOBA_BLOCK)
        s_all = jnp.dot(kb_ref[pl.ds(stj, nblk * MOBA_BLOCK), :], qts, preferred_element_type=F32)
        ss = [s_all[r * MOBA_BLOCK:(r + 1) * MOBA_BLOCK] + bias_ref[pl.ds(j0 + r, 1), :] for r in range(nblk)]
        m_old = m_ref[...]
        m_new = m_old
        for sr in ss:
            m_new = jnp.maximum(m_new, jnp.max(sr, axis=0, keepdims=True))
        alpha = jnp.exp(m_old - m_new)
        l_new = alpha * l_ref[...]
        acc_new = alpha * acc_ref[...]
        for r, sr in enumerate(ss):
            pr = jnp.exp(sr - m_new)
            l_new = l_new + jnp.sum(pr, axis=0, keepdims=True)
            acc_new = acc_new + jnp.dot(vt_ref[j0 + r], pr.astype(BF16), preferred_element_type=F32)
        l_ref[...] = l_new
        acc_ref[...] = acc_new
        m_ref[...] = m_new

    def pair(jp, c):
        past_blocks(2 * jp, 2)
        return c

    lax.fori_loop(0, qi // 2, pair, 0)

    @pl.when(qi % 2 == 1)
    def _():
        past_blocks(qi - 1, 1)

    o = (acc_ref[...] / l_ref[...]).T
    for g in range(GROUP):
        o_ref[0, :, g * HEAD_DIM:(g + 1) * HEAD_DIM] = o[g * tq:(g + 1) * tq].astype(o_ref.dtype)


def moba_prompt_attention(proj, kmean):
    b, s, _ = proj.shape
    nb = s // MOBA_BLOCK
    rows = GROUP * MOBA_BLOCK
    gw = GROUP * HEAD_DIM
    return pl.pallas_call(
        _moba_kernel,
        out_shape=jax.ShapeDtypeStruct((b, s, D_ATTN), BF16),
        grid=(b, N_KV_HEADS, nb),
        in_specs=[pl.BlockSpec((1, MOBA_BLOCK, gw), lambda bb, hk, qi: (bb, qi, hk)),
                  pl.BlockSpec((1, s, HEAD_DIM), lambda bb, hk, qi: (bb, 0, COL_K // HEAD_DIM + hk)),
                  pl.BlockSpec((1, s, HEAD_DIM), lambda bb, hk, qi: (bb, 0, COL_V // HEAD_DIM + hk)),
                  pl.BlockSpec((1, nb, HEAD_DIM), lambda bb, hk, qi: (bb, 0, hk))],
        out_specs=pl.BlockSpec((1, MOBA_BLOCK, gw), lambda bb, hk, qi: (bb, qi, hk)),
        scratch_shapes=[pltpu.VMEM((s, HEAD_DIM), BF16),
                        pltpu.VMEM((nb, HEAD_DIM, MOBA_BLOCK), BF16),
                        pltpu.VMEM((nb, rows), F32),
                        pltpu.VMEM((1, rows), F32),
                        pltpu.VMEM((1, rows), F32),
                        pltpu.VMEM((HEAD_DIM, rows), F32)],
        compiler_params=_cparams(("arbitrary", "arbitrary", "arbitrary")),
        name="moba_prompt_attention",
    )(proj, proj, proj, kmean)


def _page_sum_kernel(pt_ref, *refs, pages_per_step):
    del pt_ref
    o_ref = refs[pages_per_step]
    sub = VREG_SUBLANES
    for r in range(0, pages_per_step, 2):
        grp = (jnp.sum(refs[r][0].reshape(-1, sub, HEAD_DIM), axis=0)
               + jnp.sum(refs[r + 1][0].reshape(-1, sub, HEAD_DIM), axis=0))
        o_ref[0, 0, r // 2] = (grp[0:N_KV_HEADS] + grp[N_KV_HEADS:sub]) * (1.0 / MOBA_BLOCK)


def sample_block_means(cache_k3, page_table):
    db, n_pages = page_table.shape
    pps = 8 if n_pages % 8 == 0 else 2
    assert n_pages % pps == 0
    bps = pps // 2
    nbp = n_pages // 2
    prow = PAGE_SIZE * N_KV_HEADS

    def page_spec(r):
        return pl.BlockSpec((1, prow, HEAD_DIM), lambda bb, p, pt: (pt[bb, p * pps + r], 0, 0))

    out = pl.pallas_call(
        functools.partial(_page_sum_kernel, pages_per_step=pps),
        out_shape=jax.ShapeDtypeStruct((db, n_pages // pps, bps, N_KV_HEADS, HEAD_DIM), F32),
        grid_spec=pltpu.PrefetchScalarGridSpec(
            num_scalar_prefetch=1,
            grid=(db, n_pages // pps),
            in_specs=[page_spec(r) for r in range(pps)],
            out_specs=pl.BlockSpec((1, 1, bps, N_KV_HEADS, HEAD_DIM), lambda bb, p, pt: (bb, p, 0, 0, 0)),
        ),
        compiler_params=_cparams(("arbitrary", "arbitrary")),
        name="sample_block_means",
    )(page_table, *([cache_k3] * pps))
    return out.reshape(db, nbp, D_KV)


def _sample_select_kernel(q_ref, km_ref, o_ref):
    nbp = km_ref.shape[1]
    q = q_ref[0]
    head_kv = lax.broadcasted_iota(jnp.int32, (N_HEADS, nbp), 0) // GROUP
    gate = jnp.zeros((N_HEADS, nbp), F32)
    for hk in range(N_KV_HEADS):
        km = km_ref[0, :, hk * HEAD_DIM:(hk + 1) * HEAD_DIM]
        gk = lax.dot_general(q, km, (((1,), (1,)), ((), ())), precision=lax.Precision.HIGHEST,
                             preferred_element_type=F32)
        gate = jnp.where(head_kv == hk, gk, gate)
    blk = lax.broadcasted_iota(jnp.int32, gate.shape, 1)
    lane = lax.broadcasted_iota(jnp.int32, (N_HEADS, LANES), 1)
    out = jnp.zeros((N_HEADS, LANES), jnp.int32)
    work = gate
    for r in range(MOBA_TOPK):
        m = jnp.max(work, axis=1, keepdims=True)
        idx = jnp.min(jnp.where(work == m, blk, nbp), axis=1, keepdims=True)
        out = jnp.where(lane == r, idx, out)
        work = jnp.where(blk == idx, -jnp.inf, work)
    o_ref[0] = out


def sample_select(q_s, kmean_s):
    db = q_s.shape[0]
    nbp = kmean_s.shape[1]
    return pl.pallas_call(
        _sample_select_kernel,
        out_shape=jax.ShapeDtypeStruct((db, N_HEADS, LANES), jnp.int32),
        grid=(db,),
        in_specs=[pl.BlockSpec((1, N_HEADS, HEAD_DIM), lambda bb: (bb, 0, 0)),
                  pl.BlockSpec((1, nbp, D_KV), lambda bb: (bb, 0, 0))],
        out_specs=pl.BlockSpec((1, N_HEADS, LANES), lambda bb: (bb, 0, 0)),
        compiler_params=_cparams(("arbitrary",)),
        name="sample_select",
    )(q_s, kmean_s)


def _sample_attn_kernel(pt_ref, sel_ref, q_ref, *refs, n_pages):
    del pt_ref, sel_ref
    kp_refs = refs[:n_pages]
    vp_refs = refs[n_pages:2 * n_pages]
    kn_ref, vn_ref, o_ref, s_ref = refs[2 * n_pages:]
    h = pl.program_id(1)
    hk = h // GROUP
    prow = PAGE_SIZE * N_KV_HEADS
    qh = q_ref[0, pl.ds(h, 1), :] * (HEAD_DIM ** -0.5)
    qb = jnp.broadcast_to(qh, (HEAD_DIM, HEAD_DIM)).astype(BF16)

    def pick_kv(row):
        out = jnp.zeros((1, HEAD_DIM), F32)
        for kk in range(N_KV_HEADS):
            out = out + jnp.where(hk == kk, row[:, kk * HEAD_DIM:(kk + 1) * HEAD_DIM], 0.0)
        return out

    s_self = jnp.sum(pick_kv(kn_ref[0]) * qh, axis=1, keepdims=True)
    row_kv = lax.broadcasted_iota(jnp.int32, (prow, HEAD_DIM), 0) % N_KV_HEADS
    m = jnp.broadcast_to(s_self, (1, HEAD_DIM))
    for j in range(n_pages):
        sj = lax.dot_general(kp_refs[j][0].astype(BF16), qb, (((1,), (1,)), ((), ())), preferred_element_type=F32)
        sj = jnp.where(row_kv == hk, sj, NEG)
        s_ref[j] = sj
        m = jnp.maximum(m, jnp.max(sj, axis=0, keepdims=True))
    p_self = jnp.exp(s_self - m)
    l = p_self
    acc = p_self * pick_kv(vn_ref[0])
    for j in range(n_pages):
        pj = jnp.exp(s_ref[j] - m)
        l = l + jnp.sum(pj, axis=0, keepdims=True)
        acc = acc + jnp.sum(pj * vp_refs[j][0], axis=0, keepdims=True)
    o_ref[0, 0] = acc / l


def sample_attention(q_s, k_new, v_new, cache_k3, cache_v3, page_table, sel):
    db = q_s.shape[0]
    pages_per_blk = MOBA_BLOCK // PAGE_SIZE
    n_pages = MOBA_TOPK * pages_per_blk
    prow = PAGE_SIZE * N_KV_HEADS

    def page_spec(j):
        def idx(bb, h, pt, sl):
            return (pt[bb, sl[bb, h * MOBA_TOPK + j // pages_per_blk] * pages_per_blk + j % pages_per_blk], 0, 0)
        return pl.BlockSpec((1, prow, HEAD_DIM), idx)

    out = pl.pallas_call(
        functools.partial(_sample_attn_kernel, n_pages=n_pages),
        out_shape=jax.ShapeDtypeStruct((db, N_HEADS, 1, HEAD_DIM), F32),
        grid_spec=pltpu.PrefetchScalarGridSpec(
            num_scalar_prefetch=2,
            grid=(db, N_HEADS),
            in_specs=([pl.BlockSpec((1, N_HEADS, HEAD_DIM), lambda bb, h, pt, sl: (bb, 0, 0))]
                      + [page_spec(j) for j in range(n_pages)] + [page_spec(j) for j in range(n_pages)]
                      + [pl.BlockSpec((1, 1, D_KV), lambda bb, h, pt, sl: (bb, 0, 0)),
                         pl.BlockSpec((1, 1, D_KV), lambda bb, h, pt, sl: (bb, 0, 0))]),
            out_specs=pl.BlockSpec((1, 1, 1, HEAD_DIM), lambda bb, h, pt, sl: (bb, h, 0, 0)),
            scratch_shapes=[pltpu.VMEM((n_pages, prow, HEAD_DIM), F32)],
        ),
        compiler_params=_cparams(("arbitrary", "arbitrary")),
        name="sample_attention",
    )(page_table, sel, q_s, *([cache_k3] * n_pages), *([cache_v3] * n_pages), k_new, v_new)
    return out.reshape(db, D_ATTN)


def _lru_gates(xc, wa_ref, wx_ref, ba, bx, lam):
    nh = xc.shape[1] // LRU_BLOCK
    xb = xc.astype(BF16)
    ra, rx = [], []
    for hh in range(nh):
        xs = xb[:, hh * LRU_BLOCK:(hh + 1) * LRU_BLOCK]
        ra.append(jnp.dot(xs, wa_ref[hh], preferred_element_type=F32))
        rx.append(jnp.dot(xs, wx_ref[hh], preferred_element_type=F32))
    r = jax.nn.sigmoid(jnp.concatenate(ra, axis=1) + ba)
    gi = jax.nn.sigmoid(jnp.concatenate(rx, axis=1) + bx)
    log_a = (-LRU_C * r) * jax.nn.softplus(-lam)
    a = jnp.exp(log_a)
    bxs = jnp.sqrt(1.0 - jnp.exp(2.0 * log_a)) * (gi * xc)
    return a, bxs


def _lru_kernel(xl_ref, gl_ref, cprev_ref, h0_ref, cw_ref, cb_ref, wa_ref, wx_ref, ba_ref, bx_ref, lam_ref,
                o_ref, hl_ref, xprev_ref, hc_ref, a_ref, b_ref):
    ti = pl.program_id(2)
    tt = xl_ref.shape[1]
    sub = VREG_SUBLANES

    @pl.when(ti == 0)
    def _():
        xprev_ref[...] = cprev_ref[0]
        hc_ref[...] = jnp.broadcast_to(h0_ref[0], hc_ref.shape)

    x = xl_ref[0]
    cw = cw_ref[...]
    xc = cb_ref[...] + x * cw[CONV_W - 1:CONV_W]
    first = jnp.concatenate([xprev_ref[...], x[0:sub]], axis=0)
    xc_first = cb_ref[...] + x[0:sub] * cw[CONV_W - 1:CONV_W]
    for d in range(1, CONV_W):
        wj = cw[CONV_W - 1 - d:CONV_W - d]
        xc = xc + pltpu.roll(x, d, 0) * wj
        xc_first = xc_first + pltpu.roll(first, d, 0)[sub:2 * sub] * wj
    xc = jnp.concatenate([xc_first, xc[sub:]], axis=0)
    xprev_ref[...] = x[tt - sub:tt]

    a, bxs = _lru_gates(xc, wa_ref, wx_ref, ba_ref[...], bx_ref[...], lam_ref[...])

    rowm = lax.broadcasted_iota(jnp.int32, a.shape, 0) % sub
    for d in (1, 2, 4):
        ok = rowm >= d
        a_sh = pltpu.roll(a, d, 0)
        b_sh = pltpu.roll(bxs, d, 0)
        bxs = jnp.where(ok, a * b_sh + bxs, bxs)
        a = jnp.where(ok, a * a_sh, a)
    a_ref[...] = a
    b_ref[...] = bxs

    def grp(gidx, hprev):
        st = pl.multiple_of(gidx * sub, sub)
        hg = a_ref[pl.ds(st, sub), :] * hprev + b_ref[pl.ds(st, sub), :]
        b_ref[pl.ds(st, sub), :] = hg
        return jnp.broadcast_to(hg[sub - 1:sub], hprev.shape)

    hlast = lax.fori_loop(0, tt // sub, grp, hc_ref[...])
    hc_ref[...] = hlast
    o_ref[0] = (b_ref[...] * _gelu(gl_ref[0])).astype(o_ref.dtype)

    @pl.when(ti == pl.num_programs(2) - 1)
    def _():
        hl_ref[0] = hlast[0:1]


def lru_prompt(proj, conv_prev8, h0, conv_w, conv_b, wa_bf, wx_bf, ba, bx, lam):
    b, s, _ = proj.shape
    c = D_LRU
    tc = 512
    tt = min(512, s)
    assert s % tt == 0
    hpt = tc // LRU_BLOCK
    vec = lambda: pl.BlockSpec((1, tc), lambda bb, ci, ti: (0, ci))
    return pl.pallas_call(
        _lru_kernel,
        out_shape=[jax.ShapeDtypeStruct((b, s, c), BF16), jax.ShapeDtypeStruct((b, 1, c), F32)],
        grid=(b, c // tc, s // tt),
        in_specs=[pl.BlockSpec((1, tt, tc), lambda bb, ci, ti: (bb, ti, COL_XL // tc + ci)),
                  pl.BlockSpec((1, tt, tc), lambda bb, ci, ti: (bb, ti, COL_GL // tc + ci)),
                  pl.BlockSpec((1, VREG_SUBLANES, tc), lambda bb, ci, ti: (bb, 0, ci)),
                  pl.BlockSpec((1, 1, tc), lambda bb, ci, ti: (bb, 0, ci)),
                  pl.BlockSpec((CONV_W, tc), lambda bb, ci, ti: (0, ci)),
                  vec(),
                  pl.BlockSpec((hpt, LRU_BLOCK, LRU_BLOCK), lambda bb, ci, ti: (ci, 0, 0)),
                  pl.BlockSpec((hpt, LRU_BLOCK, LRU_BLOCK), lambda bb, ci, ti: (ci, 0, 0)),
                  vec(), vec(), vec()],
        out_specs=[pl.BlockSpec((1, tt, tc), lambda bb, ci, ti: (bb, ti, ci)),
                   pl.BlockSpec((1, 1, tc), lambda bb, ci, ti: (bb, 0, ci))],
        scratch_shapes=[pltpu.VMEM((VREG_SUBLANES, tc), F32), pltpu.VMEM((VREG_SUBLANES, tc), F32),
                        pltpu.VMEM((tt, tc), F32), pltpu.VMEM((tt, tc), F32)],
        compiler_params=_cparams(("arbitrary", "arbitrary", "arbitrary")),
        name="lru_prompt",
    )(proj, proj, conv_prev8, h0, conv_w, conv_b.reshape(1, c), wa_bf, wx_bf,
      ba.reshape(1, c), bx.reshape(1, c), lam.reshape(1, c))


def _lru_step_kernel(xl_ref, gl_ref, cprev_ref, h0_ref, cw_ref, cb_ref, wa_ref, wx_ref, ba_ref, bx_ref, lam_ref,
                     o_ref, hl_ref):
    x = xl_ref[0]
    cw = cw_ref[...]
    xc = cb_ref[...] + x * cw[CONV_W - 1:CONV_W]
    for j in range(CONV_W - 1):
        xc = xc + cprev_ref[j] * cw[j:j + 1]
    a, bxs = _lru_gates(xc, wa_ref, wx_ref, ba_ref[...], bx_ref[...], lam_ref[...])
    h = a * h0_ref[...] + bxs
    hl_ref[...] = h
    o_ref[...] = (h * _gelu(gl_ref[0])).astype(o_ref.dtype)


def lru_step(proj_s, conv_prev_t, h0, conv_w, conv_b, wa_bf, wx_bf, ba, bx, lam):
    _, db, _ = proj_s.shape
    c = D_LRU
    tc = 1024
    hpt = tc // LRU_BLOCK
    vec = lambda: pl.BlockSpec((1, tc), lambda ci: (0, ci))
    return pl.pallas_call(
        _lru_step_kernel,
        out_shape=[jax.ShapeDtypeStruct((db, c), BF16), jax.ShapeDtypeStruct((db, c), F32)],
        grid=(c // tc,),
        in_specs=[pl.BlockSpec((1, db, tc), lambda ci: (0, 0, COL_XL // tc + ci)),
                  pl.BlockSpec((1, db, tc), lambda ci: (0, 0, COL_GL // tc + ci)),
                  pl.BlockSpec((CONV_W - 1, db, tc), lambda ci: (0, 0, ci)),
                  pl.BlockSpec((db, tc), lambda ci: (0, ci)),
                  pl.BlockSpec((CONV_W, tc), lambda ci: (0, ci)),
                  vec(),
                  pl.BlockSpec((hpt, LRU_BLOCK, LRU_BLOCK), lambda ci: (ci, 0, 0)),
                  pl.BlockSpec((hpt, LRU_BLOCK, LRU_BLOCK), lambda ci: (ci, 0, 0)),
                  vec(), vec(), vec()],
        out_specs=[pl.BlockSpec((db, tc), lambda ci: (0, ci)), pl.BlockSpec((db, tc), lambda ci: (0, ci))],
        compiler_params=_cparams(("arbitrary",)),
        name="lru_step",
    )(proj_s, proj_s, conv_prev_t, h0, conv_w, conv_b.reshape(1, c), wa_bf, wx_bf,
      ba.reshape(1, c), bx.reshape(1, c), lam.reshape(1, c))


def _merge_kernel(ol_ref, oa_ref, g0_ref, g1_ref, wl_ref, wa_ref, o_ref):
    yl = jnp.dot(ol_ref[0], wl_ref[...], preferred_element_type=F32)
    ya = jnp.dot(oa_ref[0], wa_ref[...], preferred_element_type=F32)
    o_ref[0] = (jax.nn.sigmoid(g0_ref[0]) * yl + jax.nn.sigmoid(g1_ref[0]) * ya).astype(o_ref.dtype)


def branch_merge(o_lru, o_attn, proj, wl_bf, wa_bf):
    b, s, d = o_lru.shape
    tm = min(1024, s)
    tn = 512
    return pl.pallas_call(
        _merge_kernel,
        out_shape=jax.ShapeDtypeStruct((b, s, d), BF16),
        grid=(b, s // tm, d // tn),
        in_specs=[pl.BlockSpec((1, tm, d), lambda bb, i, j: (bb, i, 0)),
                  pl.BlockSpec((1, tm, d), lambda bb, i, j: (bb, i, 0)),
                  pl.BlockSpec((1, tm, tn), lambda bb, i, j: (bb, i, COL_GB // tn + j)),
                  pl.BlockSpec((1, tm, tn), lambda bb, i, j: (bb, i, (COL_GB + D_MODEL) // tn + j)),
                  pl.BlockSpec((d, tn), lambda bb, i, j: (0, j)),
                  pl.BlockSpec((d, tn), lambda bb, i, j: (0, j))],
        out_specs=pl.BlockSpec((1, tm, tn), lambda bb, i, j: (bb, i, j)),
        compiler_params=_cparams(("arbitrary", "arbitrary", "arbitrary")),
        name="branch_merge",
    )(o_lru, o_attn, proj, proj, wl_bf, wa_bf)


def _out_proj_kernel(m_ref, x_ref, gt_ref, w_ref, o_ref):
    o_ref[0] = x_ref[0] + gt_ref[0] * jnp.dot(m_ref[0], w_ref[...], preferred_element_type=F32)


def out_project(merged, x, gt, w_bf):
    b, s, d = x.shape
    tm = min(1024, s)
    tn = 512
    return pl.pallas_call(
        _out_proj_kernel,
        out_shape=jax.ShapeDtypeStruct((b, s, d), F32),
        grid=(b, s // tm, d // tn),
        in_specs=[pl.BlockSpec((1, tm, d), lambda bb, i, j: (bb, i, 0)),
                  pl.BlockSpec((1, tm, tn), lambda bb, i, j: (bb, i, j)),
                  _mod_spec(gt, tm, tn),
                  pl.BlockSpec((d, tn), lambda bb, i, j: (0, j))],
        out_specs=pl.BlockSpec((1, tm, tn), lambda bb, i, j: (bb, i, j)),
        compiler_params=_cparams(("arbitrary", "arbitrary", "arbitrary")),
        name="out_project",
    )(merged, x, gt, w_bf)


def _topk_rows(s, k, n):
    row = lax.broadcasted_iota(jnp.int32, s.shape, 0)
    slot = lax.broadcasted_iota(jnp.int32, (k, s.shape[1]), 0)
    vals = jnp.zeros((k, s.shape[1]), F32)
    idxs = jnp.zeros((k, s.shape[1]), jnp.int32)
    for r in range(k):
        m = jnp.max(s, axis=0, keepdims=True)
        am = jnp.min(jnp.where(s == m, row, n), axis=0, keepdims=True)
        vals = jnp.where(slot == r, m, vals)
        idxs = jnp.where(slot == r, am, idxs)
        s = jnp.where(row == am, -jnp.inf, s)
    return vals, idxs


def _peer_topk_kernel(q_ref, sk_ref, e_ref, g_ref):
    tt = q_ref.shape[0]
    kk = PEER_TOPK
    for hh in range(PEER_HEADS):
        sv, si = [], []
        for p in range(2):
            c0 = (hh * 2 + p) * PEER_HALF
            qc = q_ref[:, c0:c0 + PEER_HALF]
            s = lax.dot_general(sk_ref[hh, p], qc, (((1,), (1,)), ((), ())), precision=lax.Precision.HIGHEST,
                                preferred_element_type=F32)
            v, i = _topk_rows(s, kk, PEER_N_KEYS)
            sv.append(v)
            si.append(i)
        half = kk // 2
        cand = jnp.concatenate([sv[0][0:1] + sv[1]]
                               + [sv[0][a:a + 1] + sv[1][0:half] for a in range(1, half)]
                               + [sv[0][half:kk] + sv[1][0:1]], axis=0)
        cidx = jnp.concatenate([si[0][0:1] * PEER_N_KEYS + si[1]]
                               + [si[0][a:a + 1] * PEER_N_KEYS + si[1][0:half] for a in range(1, half)]
                               + [si[0][half:kk] * PEER_N_KEYS + si[1][0:1]], axis=0)
        ncand = cand.shape[0]
        row = lax.broadcasted_iota(jnp.int32, cand.shape, 0)
        slot = lax.broadcasted_iota(jnp.int32, (kk, tt), 0)
        fv = jnp.zeros((kk, tt), F32)
        eid = jnp.zeros((kk, tt), jnp.int32)
        for r in range(kk):
            m = jnp.max(cand, axis=0, keepdims=True)
            am = jnp.min(jnp.where(cand == m, row, ncand), axis=0, keepdims=True)
            pick = row == am
            fv = jnp.where(slot == r, m, fv)
            eid = jnp.where(slot == r, jnp.max(jnp.where(pick, cidx, -1), axis=0, keepdims=True), eid)
            cand = jnp.where(pick, -jnp.inf, cand)
        ex = jnp.exp(fv - fv[0:1])
        g_ref[hh * kk:(hh + 1) * kk, :] = ex / jnp.sum(ex, axis=0, keepdims=True)
        e_ref[hh * kk:(hh + 1) * kk, :] = eid


def peer_topk(qp, sub_keys):
    n = qp.shape[0]
    tt = 256 if n % 256 == 0 else n
    return pl.pallas_call(
        _peer_topk_kernel,
        out_shape=[jax.ShapeDtypeStruct((PEER_SEL, n), jnp.int32), jax.ShapeDtypeStruct((PEER_SEL, n), F32)],
        grid=(n // tt,),
        in_specs=[pl.BlockSpec((tt, qp.shape[1]), lambda i: (i, 0)),
                  pl.BlockSpec(sub_keys.shape, lambda i: (0, 0, 0, 0))],
        out_specs=[pl.BlockSpec((PEER_SEL, tt), lambda i: (0, i)), pl.BlockSpec((PEER_SEL, tt), lambda i: (0, i))],
        compiler_params=_cparams(("arbitrary",)),
        name="peer_topk",
    )(qp, sub_keys)


PEER_TOK_TILE = 8


def _peer_gather_kernel(e_ref, en_ref, h_ref, g_ref, uv_hbm, o_ref, buf, r_ref, wb_ref, sem, *, n):
    i = pl.program_id(0)
    slot = i % 2
    nrow = PEER_TOK_TILE * PEER_SEL

    def row_copy(idx_ref, t, k, dst_slot):
        r = t * PEER_SEL + k
        return pltpu.make_async_copy(uv_hbm.at[idx_ref[r]], buf.at[dst_slot, r], sem.at[dst_slot])

    @pl.when(i == 0)
    def _():
        for t in range(PEER_TOK_TILE):
            def first(k, c):
                row_copy(e_ref, t, k, 0).start()
                return c
            lax.fori_loop(0, PEER_SEL, first, 0, unroll=8)

    pltpu.make_async_copy(uv_hbm.at[pl.ds(0, nrow)], buf.at[slot], sem.at[slot]).wait()

    eye = (lax.broadcasted_iota(jnp.int32, (PEER_SEL, PEER_SEL), 0)
           == lax.broadcasted_iota(jnp.int32, (PEER_SEL, PEER_SEL), 1))

    for t in range(PEER_TOK_TILE):
        ht = jnp.concatenate([h_ref[t:t + 1, s * LANES:(s + 1) * LANES] for s in range(ROW_SUB)], axis=0)

        def dots(k, c):
            row_copy(en_ref, t, k, 1 - slot).start()
            u = buf[slot, t * PEER_SEL + k, pl.ds(0, ROW_SUB), :].astype(F32)
            pr = u * ht
            p = pr[0:VREG_SUBLANES] + pr[VREG_SUBLANES:2 * VREG_SUBLANES]
            r_ref[pl.ds(k, 1), :] = jnp.sum(p, axis=0, keepdims=True)
            return c

        lax.fori_loop(0, PEER_SEL, dots, 0, unroll=8)
        act = jnp.sum(r_ref[...], axis=1, keepdims=True)
        gcol = jnp.sum(jnp.where(eye, g_ref[pl.ds(t, 1), :], 0.0), axis=1, keepdims=True)
        wb_ref[...] = jnp.broadcast_to(gcol * _gelu(act), (PEER_SEL, LANES))

        def wsum(k, acc):
            wk = wb_ref[pl.ds(k, 1), :]
            v = buf[slot, t * PEER_SEL + k, pl.ds(ROW_SUB, ROW_SUB), :].astype(F32)
            return acc + wk * v

        acc = lax.fori_loop(0, PEER_SEL, wsum, jnp.zeros((ROW_SUB, LANES), F32), unroll=8)
        for s in range(ROW_SUB):
            o_ref[t:t + 1, s * LANES:(s + 1) * LANES] = acc[s:s + 1]

    @pl.when(i == n - 1)
    def _():
        pltpu.make_async_copy(uv_hbm.at[pl.ds(0, nrow)], buf.at[1 - slot], sem.at[1 - slot]).wait()


def peer_gather(eidx, h2, gates, uv):
    n, d = h2.shape
    tk = PEER_TOK_TILE
    nrow = tk * PEER_SEL
    nt = n // tk
    return pl.pallas_call(
        functools.partial(_peer_gather_kernel, n=nt),
        out_shape=jax.ShapeDtypeStruct((n, d), F32),
        grid=(nt,),
        in_specs=[pl.BlockSpec((nrow,), lambda i: (i,), memory_space=pltpu.SMEM),
                  pl.BlockSpec((nrow,), lambda i: (jnp.minimum(i + 1, nt - 1),), memory_space=pltpu.SMEM),
                  pl.BlockSpec((tk, d), lambda i: (i, 0)),
                  pl.BlockSpec((tk, PEER_SEL), lambda i: (i, 0)),
                  pl.BlockSpec(memory_space=pl.ANY)],
        out_specs=pl.BlockSpec((tk, d), lambda i: (i, 0)),
        scratch_shapes=[pltpu.VMEM((2, nrow, 2 * ROW_SUB, LANES), BF16),
                        pltpu.VMEM((PEER_SEL, LANES), F32), pltpu.VMEM((PEER_SEL, LANES), F32),
                        pltpu.SemaphoreType.DMA((2,))],
        compiler_params=_cparams(("arbitrary",)),
        name="peer_gather",
    )(eidx.reshape(-1), eidx.reshape(-1), h2, gates, uv)


def _final_kernel(x_ref, ff_ref, gt_ref, g_ref, o_ref):
    x = x_ref[0] + gt_ref[0] * ff_ref[0]
    o_ref[0] = x * lax.rsqrt(jnp.mean(x * x, axis=-1, keepdims=True) + EPS) * g_ref[...]


def final_norm(x1, ff, gt, g):
    b, s, d = x1.shape
    tm = min(512, s)
    gspec = (pl.BlockSpec((1, 1, d), lambda bb, i: (bb, 0, 0)) if gt.shape[1] == 1
             else pl.BlockSpec((1, tm, d), lambda bb, i: (bb, i, 0)))
    return pl.pallas_call(
        _final_kernel,
        out_shape=jax.ShapeDtypeStruct((b, s, d), F32),
        grid=(b, s // tm),
        in_specs=[pl.BlockSpec((1, tm, d), lambda bb, i: (bb, i, 0)),
                  pl.BlockSpec((1, tm, d), lambda bb, i: (bb, i, 0)),
                  gspec,
                  pl.BlockSpec((1, d), lambda bb, i: (0, 0))],
        out_specs=pl.BlockSpec((1, tm, d), lambda bb, i: (bb, i, 0)),
        compiler_params=_cparams(("arbitrary", "arbitrary")),
        name="final_norm",
    )(x1, ff, gt, g.reshape(1, d))


def _peer_block(x1, sh2, sc2, gt2, norm2_g, wq_bf, sub_keys, uv, final_g, cos, sin):
    b, s, d = x1.shape
    n = b * s
    qp, h2 = norm_proj(x1, sh2, sc2, norm2_g, wq_bf, cos, sin, rope_cols=0, emit_h=True)
    qp2 = qp.reshape(n, -1)
    n_pad = -(-n // LANES) * LANES
    if n_pad != n:
        qp2 = jnp.pad(qp2, ((0, n_pad - n), (0, 0)))
    e_t, g_t = peer_topk(qp2, sub_keys)
    eidx = e_t.T[:n]
    gates = g_t.T[:n]
    ff = peer_gather(eidx, h2.reshape(n, d), gates, uv)
    return final_norm(x1, ff.reshape(b, s, d), gt2, final_g)


def kernel(x_prompt, x_sample, c_prompt, c_sample, cache_k, cache_v, state_lru, state_conv, page_table, norm1_g, w_ada, b_ada, w_in, conv_w, conv_b, lru_wa, lru_ba, lru_wx, lru_bx, lru_lam, w_br_lru, w_br_attn, w_out, norm2_g, peer_wq, peer_subkeys, peer_u, peer_v, final_g):
    assert w_ada.shape[0] == 1, "single layer"
    b, s, d = x_prompt.shape
    db = x_sample.shape[0]
    n_pages = page_table.shape[1]
    past_len = n_pages * PAGE_SIZE
    assert s % MOBA_BLOCK == 0 and past_len % MOBA_BLOCK == 0 and x_sample.shape[1] == 1

    w_in_bf = w_in[0].astype(BF16)
    wl_bf, wa_bf, wo_bf = w_br_lru[0].astype(BF16), w_br_attn[0].astype(BF16), w_out[0].astype(BF16)
    wq_bf = peer_wq[0].astype(BF16)
    lwa_bf, lwx_bf = lru_wa[0].astype(BF16), lru_wx[0].astype(BF16)
    uv = jnp.concatenate([peer_u[0].reshape(-1, ROW_SUB, LANES), peer_v[0].reshape(-1, ROW_SUB, LANES)],
                         axis=1).astype(BF16)

    mod = ada_project(jnp.concatenate([c_prompt, c_sample], axis=0), w_ada[0], b_ada[0])
    mod_p = [m.reshape(b, 1, d) for m in jnp.split(mod[:b], N_MOD, axis=-1)]
    mod_s = [m.reshape(1, db, d) for m in jnp.split(mod[b:], N_MOD, axis=-1)]

    cos_p, sin_p = rope_tables(jnp.arange(s, dtype=jnp.int32))
    (proj,) = norm_proj(x_prompt, mod_p[0], mod_p[1], norm1_g[0], w_in_bf, cos_p, sin_p,
                        rope_cols=D_ATTN + D_KV, emit_h=False)
    kmean = moba_block_means(proj)
    o_attn = moba_prompt_attention(proj, kmean)
    o_lru, h_last_p = lru_prompt(proj, jnp.zeros((b, VREG_SUBLANES, D_LRU), F32), jnp.zeros((b, 1, D_LRU), F32),
                                 conv_w[0], conv_b[0], lwa_bf, lwx_bf, lru_ba[0], lru_bx[0], lru_lam[0])
    merged = branch_merge(o_lru, o_attn, proj, wl_bf, wa_bf)
    x1 = out_project(merged, x_prompt, mod_p[2], wo_bf)
    y_prompt = _peer_block(x1, mod_p[3], mod_p[4], mod_p[5], norm2_g[0], wq_bf, peer_subkeys[0], uv,
                           final_g, cos_p, sin_p)
    k_prompt = proj[:, :, COL_K:COL_K + D_KV].reshape(1, b, s, N_KV_HEADS, HEAD_DIM)
    v_prompt = proj[:, :, COL_V:COL_V + D_KV].reshape(1, b, s, N_KV_HEADS, HEAD_DIM)
    conv_prompt = proj[:, s - (CONV_W - 1):, COL_XL:COL_XL + D_LRU].reshape(1, b, CONV_W - 1, D_LRU)

    xs = x_sample.reshape(1, db, d)
    cos_s, sin_s = rope_tables(jnp.full((db,), past_len, jnp.int32))
    (proj_s,) = norm_proj(xs, mod_s[0], mod_s[1], norm1_g[0], w_in_bf, cos_s, sin_s,
                          rope_cols=D_ATTN + D_KV, emit_h=False)
    q_s = proj_s[0, :, COL_Q:COL_Q + D_ATTN].reshape(db, N_HEADS, HEAD_DIM)
    k_new = proj_s[0, :, COL_K:COL_K + D_KV].reshape(db, 1, D_KV)
    v_new = proj_s[0, :, COL_V:COL_V + D_KV].reshape(db, 1, D_KV)
    n_pool = cache_k.shape[1]
    cache_k3 = cache_k[0].reshape(n_pool, PAGE_SIZE * N_KV_HEADS, HEAD_DIM)
    cache_v3 = cache_v[0].reshape(n_pool, PAGE_SIZE * N_KV_HEADS, HEAD_DIM)
    kmean_s = sample_block_means(cache_k3, page_table)
    sel = sample_select(q_s, kmean_s)[:, :, :MOBA_TOPK].reshape(db, N_HEADS * MOBA_TOPK)
    o_attn_s = sample_attention(q_s, k_new, v_new, cache_k3, cache_v3, page_table, sel)
    o_lru_s, h_last_s = lru_step(proj_s, jnp.transpose(state_conv[0], (1, 0, 2)), state_lru[0],
                                 conv_w[0], conv_b[0], lwa_bf, lwx_bf, lru_ba[0], lru_bx[0], lru_lam[0])
    merged_s = branch_merge(o_lru_s.reshape(1, db, d), o_attn_s.astype(BF16).reshape(1, db, d), proj_s, wl_bf, wa_bf)
    x1_s = out_project(merged_s, xs, mod_s[2], wo_bf)
    y_sample = _peer_block(x1_s, mod_s[3], mod_s[4], mod_s[5], norm2_g[0], wq_bf, peer_subkeys[0], uv,
                           final_g, cos_s, sin_s)
    xl_s = proj_s[0, :, COL_XL:COL_XL + D_LRU]
    conv_sample = jnp.concatenate([state_conv[0][:, 1:], xl_s[:, None, :]], axis=1)[None]

    return (y_prompt, y_sample.reshape(db, 1, d), k_prompt, v_prompt,
            h_last_p.reshape(1, b, D_LRU), conv_prompt,
            k_new.reshape(1, db, 1, N_KV_HEADS, HEAD_DIM), v_new.reshape(1, db, 1, N_KV_HEADS, HEAD_DIM),
            h_last_s.reshape(1, db, D_LRU), conv_sample)
```

```python
import functools
import math

import jax
import jax.numpy as jnp
from jax import lax
from jax.experimental import pallas as pl
from jax.experimental.pallas import tpu as pltpu

D_MODEL = 2048
PAGE_SIZE = 128
N_HEADS = 16
N_KV_HEADS = 4
HEAD_DIM = 128
GROUP = N_HEADS // N_KV_HEADS
D_ATTN = N_HEADS * HEAD_DIM
D_KV = N_KV_HEADS * HEAD_DIM
ROPE_THETA = 10000.0
MOBA_BLOCK = 256
MOBA_TOPK = 3
D_LRU = D_MODEL
LRU_HEADS = 16
LRU_BLOCK = D_LRU // LRU_HEADS
CONV_W = 4
LRU_C = 8.0
PEER_HEADS = 8
PEER_N_KEYS = 128
PEER_HALF = 128
PEER_TOPK = 16
PEER_SEL = PEER_HEADS * PEER_TOPK
N_MOD = 6
EPS = 1e-6

COL_Q = 0
COL_K = D_ATTN
COL_V = D_ATTN + D_KV
COL_XL = D_ATTN + 2 * D_KV
COL_GL = COL_XL + D_LRU
COL_GB = COL_GL + D_LRU
IN_COLS = COL_GB + 2 * D_MODEL

VREG_SUBLANES = 8
LANES = 128
ROW_SUB = D_MODEL // LANES
NEG = -1e30
BF16 = jnp.bfloat16
F32 = jnp.float32
VMEM_LIMIT = 52 * 1024 * 1024


def _cparams(sem):
    return pltpu.CompilerParams(dimension_semantics=sem, vmem_limit_bytes=VMEM_LIMIT)


def _gelu(x):
    return 0.5 * x * (1.0 + lax.erf(x * (1.0 / math.sqrt(2.0))))


def _mod_spec(mod, tm, tn, tiled_cols=True):
    col = (lambda j: j) if tiled_cols else (lambda j: 0)
    if mod.shape[1] == 1:
        return pl.BlockSpec((1, 1, tn), lambda b, i, j: (b, 0, col(j)))
    return pl.BlockSpec((1, tm, tn), lambda b, i, j: (b, i, col(j)))


def _ada_kernel(c_ref, w_ref, b_ref, o_ref):
    o_ref[...] = jnp.dot(c_ref[...].astype(BF16), w_ref[...].astype(BF16),
                         preferred_element_type=F32) + b_ref[...]


def ada_project(c_all, w_ada, b_ada):
    m, d = c_all.shape
    n = w_ada.shape[1]
    tn = 1024
    return pl.pallas_call(
        _ada_kernel,
        out_shape=jax.ShapeDtypeStruct((m, n), F32),
        grid=(n // tn,),
        in_specs=[pl.BlockSpec((m, d), lambda j: (0, 0)),
                  pl.BlockSpec((d, tn), lambda j: (0, j)),
                  pl.BlockSpec((1, tn), lambda j: (0, j))],
        out_specs=pl.BlockSpec((m, tn), lambda j: (0, j)),
        compiler_params=_cparams(("arbitrary",)),
        name="ada_project",
    )(c_all, w_ada, b_ada.reshape(1, n))


def _norm_proj_kernel(x_ref, sh_ref, sc_ref, g_ref, w_ref, cos_ref, sin_ref, *rest, rope_tiles, emit_h):
    if emit_h:
        o_ref, h_ref, hs_ref = rest
    else:
        o_ref, hs_ref = rest
    j = pl.program_id(2)

    @pl.when(j == 0)
    def _():
        x = x_ref[0]
        y = x * lax.rsqrt(jnp.mean(x * x, axis=-1, keepdims=True) + EPS) * g_ref[...]
        h = y * (1.0 + sc_ref[0]) + sh_ref[0]
        hs_ref[...] = h.astype(BF16)
        if emit_h:
            h_ref[0] = h

    acc = jnp.dot(hs_ref[...], w_ref[...], preferred_element_type=F32)

    if rope_tiles:
        @pl.when(j < rope_tiles)
        def _():
            cos = cos_ref[...]
            sin = sin_ref[...]
            parts = []
            for hh in range(acc.shape[1] // HEAD_DIM):
                a = acc[:, hh * HEAD_DIM:(hh + 1) * HEAD_DIM]
                parts.append(a * cos + pltpu.roll(a, HEAD_DIM // 2, 1) * sin)
            o_ref[0] = jnp.concatenate(parts, axis=1)

        @pl.when(j >= rope_tiles)
        def _():
            o_ref[0] = acc
    else:
        o_ref[0] = acc


def norm_proj(x, sh, sc, g, w_bf, cos, sin, *, rope_cols, emit_h):
    b, s, d = x.shape
    n = w_bf.shape[1]
    tm = min(512 if emit_h else 1024, s)
    tn = 512
    assert s % tm == 0 and n % tn == 0 and rope_cols % tn == 0
    out_shape = [jax.ShapeDtypeStruct((b, s, n), F32)]
    out_specs = [pl.BlockSpec((1, tm, tn), lambda bb, i, j: (bb, i, j))]
    if emit_h:
        out_shape.append(jax.ShapeDtypeStruct((b, s, d), F32))
        out_specs.append(pl.BlockSpec((1, tm, d), lambda bb, i, j: (bb, i, 0)))
    res = pl.pallas_call(
        functools.partial(_norm_proj_kernel, rope_tiles=rope_cols // tn, emit_h=emit_h),
        out_shape=out_shape,
        grid=(b, s // tm, n // tn),
        in_specs=[pl.BlockSpec((1, tm, d), lambda bb, i, j: (bb, i, 0)),
                  _mod_spec(sh, tm, d, False), _mod_spec(sc, tm, d, False),
                  pl.BlockSpec((1, d), lambda bb, i, j: (0, 0)),
                  pl.BlockSpec((d, tn), lambda bb, i, j: (0, j)),
                  pl.BlockSpec((tm, HEAD_DIM), lambda bb, i, j: (i, 0)),
                  pl.BlockSpec((tm, HEAD_DIM), lambda bb, i, j: (i, 0))],
        out_specs=out_specs,
        scratch_shapes=[pltpu.VMEM((tm, d), BF16)],
        compiler_params=_cparams(("arbitrary", "arbitrary", "arbitrary")),
        name="norm_proj",
    )(x, sh, sc, g.reshape(1, d), w_bf, cos, sin)
    return res


def rope_tables(pos):
    half = HEAD_DIM // 2
    inv = jnp.exp(-math.log(ROPE_THETA) * jnp.arange(half, dtype=F32) / half)
    ang = pos.astype(F32)[:, None] * inv[None, :]
    cos, sin = jnp.cos(ang), jnp.sin(ang)
    return jnp.concatenate([cos, cos], axis=1), jnp.concatenate([-sin, sin], axis=1)


def _kmean_kernel(k_ref, o_ref):
    o_ref[0, 0] = jnp.mean(k_ref[0], axis=0, keepdims=True)


def moba_block_means(proj):
    b, s, _ = proj.shape
    nb = s // MOBA_BLOCK
    out = pl.pallas_call(
        _kmean_kernel,
        out_shape=jax.ShapeDtypeStruct((b, nb, 1, D_KV), F32),
        grid=(b, nb),
        in_specs=[pl.BlockSpec((1, MOBA_BLOCK, D_KV), lambda bb, i: (bb, i, COL_K // D_KV))],
        out_specs=pl.BlockSpec((1, 1, 1, D_KV), lambda bb, i: (bb, i, 0, 0)),
        compiler_params=_cparams(("arbitrary", "arbitrary")),
        name="moba_block_means",
    )(proj)
    return out.reshape(b, nb, D_KV)


def _top3_bias(gate, own_blk, nb):
    blk = lax.broadcasted_iota(jnp.int32, gate.shape, 0)
    past = blk < own_blk
    work = jnp.where(past, gate, -jnp.inf)
    sel = jnp.zeros(gate.shape, jnp.bool_)
    for _ in range(min(MOBA_TOPK, nb)):
        m = jnp.max(work, axis=0, keepdims=True)
        idx = jnp.min(jnp.where(work == m, blk, nb), axis=0, keepdims=True)
        pick = blk == idx
        sel = jnp.logical_or(sel, jnp.logical_and(pick, past))
        work = jnp.where(pick, -jnp.inf, work)
    return jnp.where(sel, 0.0, NEG).astype(F32)


def _moba_kernel(q_ref, k_ref, v_ref, km_ref, o_ref, kb_ref, vt_ref, bias_ref, m_ref, l_ref, acc_ref):
    qi = pl.program_id(2)
    nb = km_ref.shape[1]
    tq = MOBA_BLOCK
    rows = GROUP * tq

    @pl.when(qi == 0)
    def _():
        def cp(jb, c):
            st = pl.multiple_of(jb * MOBA_BLOCK, MOBA_BLOCK)
            kb_ref[pl.ds(st, MOBA_BLOCK), :] = k_ref[0, pl.ds(st, MOBA_BLOCK), :].astype(BF16)
            vt_ref[jb] = v_ref[0, pl.ds(st, MOBA_BLOCK), :].T.astype(BF16)
            return c
        lax.fori_loop(0, nb, cp, 0)

    q = q_ref[0]
    qcat = jnp.concatenate([q[:, g * HEAD_DIM:(g + 1) * HEAD_DIM] for g in range(GROUP)], axis=0)
    qt = qcat.T
    gate = jnp.dot(km_ref[0], qt, precision=lax.Precision.HIGHEST, preferred_element_type=F32)
    bias_ref[...] = _top3_bias(gate, qi, nb)
    qts = (qt * (HEAD_DIM ** -0.5)).astype(BF16)

    st = pl.multiple_of(qi * MOBA_BLOCK, MOBA_BLOCK)
    s = jnp.dot(kb_ref[pl.ds(st, MOBA_BLOCK), :], qts, preferred_element_type=F32)
    key_t = lax.broadcasted_iota(jnp.int32, (MOBA_BLOCK, rows), 0)
    q_t = lax.broadcasted_iota(jnp.int32, (MOBA_BLOCK, rows), 1) % tq
    s = jnp.where(key_t <= q_t, s, NEG)
    m0 = jnp.max(s, axis=0, keepdims=True)
    p = jnp.exp(s - m0)
    m_ref[...] = m0
    l_ref[...] = jnp.sum(p, axis=0, keepdims=True)
    acc_ref[...] = jnp.dot(vt_ref[qi], p.astype(BF16), preferred_element_type=F32)

    def past_blocks(j0, nblk):
        stj = pl.multiple_of(j0 * MOBA_BLOCK, MOBA_BLOCK)
        s_all = jnp.dot(kb_ref[pl.ds(stj, nblk * MOBA_BLOCK), :], qts, preferred_element_type=F32)
        ss = [s_all[r * MOBA_BLOCK:(r + 1) * MOBA_BLOCK] + bias_ref[pl.ds(j0 + r, 1), :] for r in range(nblk)]
        m_old = m_ref[...]
        m_new = m_old
        for sr in ss:
            m_new = jnp.maximum(m_new, jnp.max(sr, axis=0, keepdims=True))
        alpha = jnp.exp(m_old - m_new)
        l_new = alpha * l_ref[...]
        acc_new = alpha * acc_ref[...]
        for r, sr in enumerate(ss):
            pr = jnp.exp(sr - m_new)
            l_new = l_new + jnp.sum(pr, axis=0, keepdims=True)
            acc_new = acc_new + jnp.dot(vt_ref[j0 + r], pr.astype(BF16), preferred_element_type=F32)
        l_ref[...] = l_new
        acc_ref[...] = acc_new
        m_ref[...] = m_new

    def pair(jp, c):
        past_blocks(2 * jp, 2)
        return c

    lax.fori_loop(0, qi // 2, pair, 0)

    @pl.when(qi % 2 == 1)
    def _():
        past_blocks(qi - 1, 1)

    o = (acc_ref[...] / l_ref[...]).T
    for g in range(GROUP):
        o_ref[0, :, g * HEAD_DIM:(g + 1) * HEAD_DIM] = o[g * tq:(g + 1) * tq].astype(o_ref.dtype)


def moba_prompt_attention(proj, kmean):
    b, s, _ = proj.shape
    nb = s // MOBA_BLOCK
    rows = GROUP * MOBA_BLOCK
    gw = GROUP * HEAD_DIM
    return pl.pallas_call(
        _moba_kernel,
        out_shape=jax.ShapeDtypeStruct((b, s, D_ATTN), BF16),
        grid=(b, N_KV_HEADS, nb),
        in_specs=[pl.BlockSpec((1, MOBA_BLOCK, gw), lambda bb, hk, qi: (bb, qi, hk)),
                  pl.BlockSpec((1, s, HEAD_DIM), lambda bb, hk, qi: (bb, 0, COL_K // HEAD_DIM + hk)),
                  pl.BlockSpec((1, s, HEAD_DIM), lambda bb, hk, qi: (bb, 0, COL_V // HEAD_DIM + hk)),
                  pl.BlockSpec((1, nb, HEAD_DIM), lambda bb, hk, qi: (bb, 0, hk))],
        out_specs=pl.BlockSpec((1, MOBA_BLOCK, gw), lambda bb, hk, qi: (bb, qi, hk)),
        scratch_shapes=[pltpu.VMEM((s, HEAD_DIM), BF16),
                        pltpu.VMEM((nb, HEAD_DIM, MOBA_BLOCK), BF16),
                        pltpu.VMEM((nb, rows), F32),
                        pltpu.VMEM((1, rows), F32),
                        pltpu.VMEM((1, rows), F32),
                        pltpu.VMEM((HEAD_DIM, rows), F32)],
        compiler_params=_cparams(("arbitrary", "arbitrary", "arbitrary")),
        name="moba_prompt_attention",
    )(proj, proj, proj, kmean)


def _page_sum_kernel(pt_ref, *refs, pages_per_step):
    del pt_ref
    o_ref = refs[pages_per_step]
    sub = VREG_SUBLANES
    for r in range(0, pages_per_step, 2):
        grp = (jnp.sum(refs[r][0].reshape(-1, sub, HEAD_DIM), axis=0)
               + jnp.sum(refs[r + 1][0].reshape(-1, sub, HEAD_DIM), axis=0))
        o_ref[0, 0, r // 2] = (grp[0:N_KV_HEADS] + grp[N_KV_HEADS:sub]) * (1.0 / MOBA_BLOCK)


def sample_block_means(cache_k3, page_table):
    db, n_pages = page_table.shape
    pps = 8 if n_pages % 8 == 0 else 2
    assert n_pages % pps == 0
    bps = pps // 2
    nbp = n_pages // 2
    prow = PAGE_SIZE * N_KV_HEADS

    def page_spec(r):
        return pl.BlockSpec((1, prow, HEAD_DIM), lambda bb, p, pt: (pt[bb, p * pps + r], 0, 0))

    out = pl.pallas_call(
        functools.partial(_page_sum_kernel, pages_per_step=pps),
        out_shape=jax.ShapeDtypeStruct((db, n_pages // pps, bps, N_KV_HEADS, HEAD_DIM), F32),
        grid_spec=pltpu.PrefetchScalarGridSpec(
            num_scalar_prefetch=1,
            grid=(db, n_pages // pps),
            in_specs=[page_spec(r) for r in range(pps)],
            out_specs=pl.BlockSpec((1, 1, bps, N_KV_HEADS, HEAD_DIM), lambda bb, p, pt: (bb, p, 0, 0, 0)),
        ),
        compiler_params=_cparams(("arbitrary", "arbitrary")),
        name="sample_block_means",
    )(page_table, *([cache_k3] * pps))
    return out.reshape(db, nbp, D_KV)


def _sample_select_kernel(q_ref, km_ref, o_ref):
    nbp = km_ref.shape[1]
    q = q_ref[0]
    head_kv = lax.broadcasted_iota(jnp.int32, (N_HEADS, nbp), 0) // GROUP
    gate = jnp.zeros((N_HEADS, nbp), F32)
    for hk in range(N_KV_HEADS):
        km = km_ref[0, :, hk * HEAD_DIM:(hk + 1) * HEAD_DIM]
        gk = lax.dot_general(q, km, (((1,), (1,)), ((), ())), precision=lax.Precision.HIGHEST,
                             preferred_element_type=F32)
        gate = jnp.where(head_kv == hk, gk, gate)
    blk = lax.broadcasted_iota(jnp.int32, gate.shape, 1)
    lane = lax.broadcasted_iota(jnp.int32, (N_HEADS, LANES), 1)
    out = jnp.zeros((N_HEADS, LANES), jnp.int32)
    work = gate
    for r in range(MOBA_TOPK):
        m = jnp.max(work, axis=1, keepdims=True)
        idx = jnp.min(jnp.where(work == m, blk, nbp), axis=1, keepdims=True)
        out = jnp.where(lane == r, idx, out)
        work = jnp.where(blk == idx, -jnp.inf, work)
    o_ref[0] = out


def sample_select(q_s, kmean_s):
    db = q_s.shape[0]
    nbp = kmean_s.shape[1]
    return pl.pallas_call(
        _sample_select_kernel,
        out_shape=jax.ShapeDtypeStruct((db, N_HEADS, LANES), jnp.int32),
        grid=(db,),
        in_specs=[pl.BlockSpec((1, N_HEADS, HEAD_DIM), lambda bb: (bb, 0, 0)),
                  pl.BlockSpec((1, nbp, D_KV), lambda bb: (bb, 0, 0))],
        out_specs=pl.BlockSpec((1, N_HEADS, LANES), lambda bb: (bb, 0, 0)),
        compiler_params=_cparams(("arbitrary",)),
        name="sample_select",
    )(q_s, kmean_s)


def _sample_attn_kernel(pt_ref, sel_ref, q_ref, *refs, n_pages):
    del pt_ref, sel_ref
    kp_refs = refs[:n_pages]
    vp_refs = refs[n_pages:2 * n_pages]
    kn_ref, vn_ref, o_ref, s_ref = refs[2 * n_pages:]
    h = pl.program_id(1)
    hk = h // GROUP
    prow = PAGE_SIZE * N_KV_HEADS
    qh = q_ref[0, pl.ds(h, 1), :] * (HEAD_DIM ** -0.5)
    qb = jnp.broadcast_to(qh, (HEAD_DIM, HEAD_DIM)).astype(BF16)

    def pick_kv(row):
        out = jnp.zeros((1, HEAD_DIM), F32)
        for kk in range(N_KV_HEADS):
            out = out + jnp.where(hk == kk, row[:, kk * HEAD_DIM:(kk + 1) * HEAD_DIM], 0.0)
        return out

    s_self = jnp.sum(pick_kv(kn_ref[0]) * qh, axis=1, keepdims=True)
    row_kv = lax.broadcasted_iota(jnp.int32, (prow, HEAD_DIM), 0) % N_KV_HEADS
    m = jnp.broadcast_to(s_self, (1, HEAD_DIM))
    for j in range(n_pages):
        sj = lax.dot_general(kp_refs[j][0].astype(BF16), qb, (((1,), (1,)), ((), ())), preferred_element_type=F32)
        sj = jnp.where(row_kv == hk, sj, NEG)
        s_ref[j] = sj
        m = jnp.maximum(m, jnp.max(sj, axis=0, keepdims=True))
    p_self = jnp.exp(s_self - m)
    l = p_self
    acc = p_self * pick_kv(vn_ref[0])
    for j in range(n_pages):
        pj = jnp.exp(s_ref[j] - m)
        l = l + jnp.sum(pj, axis=0, keepdims=True)
        acc = acc + jnp.sum(pj * vp_refs[j][0], axis=0, keepdims=True)
    o_ref[0, 0] = acc / l


def sample_attention(q_s, k_new, v_new, cache_k3, cache_v3, page_table, sel):
    db = q_s.shape[0]
    pages_per_blk = MOBA_BLOCK // PAGE_SIZE
    n_pages = MOBA_TOPK * pages_per_blk
    prow = PAGE_SIZE * N_KV_HEADS

    def page_spec(j):
        def idx(bb, h, pt, sl):
            return (pt[bb, sl[bb, h * MOBA_TOPK + j // pages_per_blk] * pages_per_blk + j % pages_per_blk], 0, 0)
        return pl.BlockSpec((1, prow, HEAD_DIM), idx)

    out = pl.pallas_call(
        functools.partial(_sample_attn_kernel, n_pages=n_pages),
        out_shape=jax.ShapeDtypeStruct((db, N_HEADS, 1, HEAD_DIM), F32),
        grid_spec=pltpu.PrefetchScalarGridSpec(
            num_scalar_prefetch=2,
            grid=(db, N_HEADS),
            in_specs=([pl.BlockSpec((1, N_HEADS, HEAD_DIM), lambda bb, h, pt, sl: (bb, 0, 0))]
                      + [page_spec(j) for j in range(n_pages)] + [page_spec(j) for j in range(n_pages)]
                      + [pl.BlockSpec((1, 1, D_KV), lambda bb, h, pt, sl: (bb, 0, 0)),
                         pl.BlockSpec((1, 1, D_KV), lambda bb, h, pt, sl: (bb, 0, 0))]),
            out_specs=pl.BlockSpec((1, 1, 1, HEAD_DIM), lambda bb, h, pt, sl: (bb, h, 0, 0)),
            scratch_shapes=[pltpu.VMEM((n_pages, prow, HEAD_DIM), F32)],
        ),
        compiler_params=_cparams(("arbitrary", "arbitrary")),
        name="sample_attention",
    )(page_table, sel, q_s, *([cache_k3] * n_pages), *([cache_v3] * n_pages), k_new, v_new)
    return out.reshape(db, D_ATTN)


def _lru_gates(xc, wa_ref, wx_ref, ba, bx, lam):
    nh = xc.shape[1] // LRU_BLOCK
    xb = xc.astype(BF16)
    ra, rx = [], []
    for hh in range(nh):
        xs = xb[:, hh * LRU_BLOCK:(hh + 1) * LRU_BLOCK]
        ra.append(jnp.dot(xs, wa_ref[hh], preferred_element_type=F32))
        rx.append(jnp.dot(xs, wx_ref[hh], preferred_element_type=F32))
    r = jax.nn.sigmoid(jnp.concatenate(ra, axis=1) + ba)
    gi = jax.nn.sigmoid(jnp.concatenate(rx, axis=1) + bx)
    log_a = (-LRU_C * r) * jax.nn.softplus(-lam)
    a = jnp.exp(log_a)
    bxs = jnp.sqrt(1.0 - jnp.exp(2.0 * log_a)) * (gi * xc)
    return a, bxs


def _lru_kernel(xl_ref, gl_ref, cprev_ref, h0_ref, cw_ref, cb_ref, wa_ref, wx_ref, ba_ref, bx_ref, lam_ref,
                o_ref, hl_ref, xprev_ref, hc_ref, a_ref, b_ref):
    ti = pl.program_id(2)
    tt = xl_ref.shape[1]
    sub = VREG_SUBLANES

    @pl.when(ti == 0)
    def _():
        xprev_ref[...] = cprev_ref[0]
        hc_ref[...] = jnp.broadcast_to(h0_ref[0], hc_ref.shape)

    x = xl_ref[0]
    cw = cw_ref[...]
    xc = cb_ref[...] + x * cw[CONV_W - 1:CONV_W]
    first = jnp.concatenate([xprev_ref[...], x[0:sub]], axis=0)
    xc_first = cb_ref[...] + x[0:sub] * cw[CONV_W - 1:CONV_W]
    for d in range(1, CONV_W):
        wj = cw[CONV_W - 1 - d:CONV_W - d]
        xc = xc + pltpu.roll(x, d, 0) * wj
        xc_first = xc_first + pltpu.roll(first, d, 0)[sub:2 * sub] * wj
    xc = jnp.concatenate([xc_first, xc[sub:]], axis=0)
    xprev_ref[...] = x[tt - sub:tt]

    a, bxs = _lru_gates(xc, wa_ref, wx_ref, ba_ref[...], bx_ref[...], lam_ref[...])

    rowm = lax.broadcasted_iota(jnp.int32, a.shape, 0) % sub
    for d in (1, 2, 4):
        ok = rowm >= d
        a_sh = pltpu.roll(a, d, 0)
        b_sh = pltpu.roll(bxs, d, 0)
        bxs = jnp.where(ok, a * b_sh + bxs, bxs)
        a = jnp.where(ok, a * a_sh, a)
    a_ref[...] = a
    b_ref[...] = bxs

    def grp(gidx, hprev):
        st = pl.multiple_of(gidx * sub, sub)
        hg = a_ref[pl.ds(st, sub), :] * hprev + b_ref[pl.ds(st, sub), :]
        b_ref[pl.ds(st, sub), :] = hg
        return jnp.broadcast_to(hg[sub - 1:sub], hprev.shape)

    hlast = lax.fori_loop(0, tt // sub, grp, hc_ref[...])
    hc_ref[...] = hlast
    o_ref[0] = (b_ref[...] * _gelu(gl_ref[0])).astype(o_ref.dtype)

    @pl.when(ti == pl.num_programs(2) - 1)
    def _():
        hl_ref[0] = hlast[0:1]


def lru_prompt(proj, conv_prev8, h0, conv_w, conv_b, wa_bf, wx_bf, ba, bx, lam):
    b, s, _ = proj.shape
    c = D_LRU
    tc = 512
    tt = min(512, s)
    assert s % tt == 0
    hpt = tc // LRU_BLOCK
    vec = lambda: pl.BlockSpec((1, tc), lambda bb, ci, ti: (0, ci))
    return pl.pallas_call(
        _lru_kernel,
        out_shape=[jax.ShapeDtypeStruct((b, s, c), BF16), jax.ShapeDtypeStruct((b, 1, c), F32)],
        grid=(b, c // tc, s // tt),
        in_specs=[pl.BlockSpec((1, tt, tc), lambda bb, ci, ti: (bb, ti, COL_XL // tc + ci)),
                  pl.BlockSpec((1, tt, tc), lambda bb, ci, ti: (bb, ti, COL_GL // tc + ci)),
                  pl.BlockSpec((1, VREG_SUBLANES, tc), lambda bb, ci, ti: (bb, 0, ci)),
                  pl.BlockSpec((1, 1, tc), lambda bb, ci, ti: (bb, 0, ci)),
                  pl.BlockSpec((CONV_W, tc), lambda bb, ci, ti: (0, ci)),
                  vec(),
                  pl.BlockSpec((hpt, LRU_BLOCK, LRU_BLOCK), lambda bb, ci, ti: (ci, 0, 0)),
                  pl.BlockSpec((hpt, LRU_BLOCK, LRU_BLOCK), lambda bb, ci, ti: (ci, 0, 0)),
                  vec(), vec(), vec()],
        out_specs=[pl.BlockSpec((1, tt, tc), lambda bb, ci, ti: (bb, ti, ci)),
                   pl.BlockSpec((1, 1, tc), lambda bb, ci, ti: (bb, 0, ci))],
        scratch_shapes=[pltpu.VMEM((VREG_SUBLANES, tc), F32), pltpu.VMEM((VREG_SUBLANES, tc), F32),
                        pltpu.VMEM((tt, tc), F32), pltpu.VMEM((tt, tc), F32)],
        compiler_params=_cparams(("arbitrary", "arbitrary", "arbitrary")),
        name="lru_prompt",
    )(proj, proj, conv_prev8, h0, conv_w, conv_b.reshape(1, c), wa_bf, wx_bf,
      ba.reshape(1, c), bx.reshape(1, c), lam.reshape(1, c))


def _lru_step_kernel(xl_ref, gl_ref, cprev_ref, h0_ref, cw_ref, cb_ref, wa_ref, wx_ref, ba_ref, bx_ref, lam_ref,
                     o_ref, hl_ref):
    x = xl_ref[0]
    cw = cw_ref[...]
    xc = cb_ref[...] + x * cw[CONV_W - 1:CONV_W]
    for j in range(CONV_W - 1):
        xc = xc + cprev_ref[j] * cw[j:j + 1]
    a, bxs = _lru_gates(xc, wa_ref, wx_ref, ba_ref[...], bx_ref[...], lam_ref[...])
    h = a * h0_ref[...] + bxs
    hl_ref[...] = h
    o_ref[...] = (h * _gelu(gl_ref[0])).astype(o_ref.dtype)


def lru_step(proj_s, conv_prev_t, h0, conv_w, conv_b, wa_bf, wx_bf, ba, bx, lam):
    _, db, _ = proj_s.shape
    c = D_LRU
    tc = 1024
    hpt = tc // LRU_BLOCK
    vec = lambda: pl.BlockSpec((1, tc), lambda ci: (0, ci))
    return pl.pallas_call(
        _lru_step_kernel,
        out_shape=[jax.ShapeDtypeStruct((db, c), BF16), jax.ShapeDtypeStruct((db, c), F32)],
        grid=(c // tc,),
        in_specs=[pl.BlockSpec((1, db, tc), lambda ci: (0, 0, COL_XL // tc + ci)),
                  pl.BlockSpec((1, db, tc), lambda ci: (0, 0, COL_GL // tc + ci)),
                  pl.BlockSpec((CONV_W - 1, db, tc), lambda ci: (0, 0, ci)),
                  pl.BlockSpec((db, tc), lambda ci: (0, ci)),
                  pl.BlockSpec((CONV_W, tc), lambda ci: (0, ci)),
                  vec(),
                  pl.BlockSpec((hpt, LRU_BLOCK, LRU_BLOCK), lambda ci: (ci, 0, 0)),
                  pl.BlockSpec((hpt, LRU_BLOCK, LRU_BLOCK), lambda ci: (ci, 0, 0)),
                  vec(), vec(), vec()],
        out_specs=[pl.BlockSpec((db, tc), lambda ci: (0, ci)), pl.BlockSpec((db, tc), lambda ci: (0, ci))],
        compiler_params=_cparams(("arbitrary",)),
        name="lru_step",
    )(proj_s, proj_s, conv_prev_t, h0, conv_w, conv_b.reshape(1, c), wa_bf, wx_bf,
      ba.reshape(1, c), bx.reshape(1, c), lam.reshape(1, c))


def _merge_kernel(ol_ref, oa_ref, g0_ref, g1_ref, wl_ref, wa_ref, o_ref):
    yl = jnp.dot(ol_ref[0], wl_ref[...], preferred_element_type=F32)
    ya = jnp.dot(oa_ref[0], wa_ref[...], preferred_element_type=F32)
    o_ref[0] = (jax.nn.sigmoid(g0_ref[0]) * yl + jax.nn.sigmoid(g1_ref[0]) * ya).astype(o_ref.dtype)


def branch_merge(o_lru, o_attn, proj, wl_bf, wa_bf):
    b, s, d = o_lru.shape
    tm = min(1024, s)
    tn = 512
    return pl.pallas_call(
        _merge_kernel,
        out_shape=jax.ShapeDtypeStruct((b, s, d), BF16),
        grid=(b, s // tm, d // tn),
        in_specs=[pl.BlockSpec((1, tm, d), lambda bb, i, j: (bb, i, 0)),
                  pl.BlockSpec((1, tm, d), lambda bb, i, j: (bb, i, 0)),
                  pl.BlockSpec((1, tm, tn), lambda bb, i, j: (bb, i, COL_GB // tn + j)),
                  pl.BlockSpec((1, tm, tn), lambda bb, i, j: (bb, i, (COL_GB + D_MODEL) // tn + j)),
                  pl.BlockSpec((d, tn), lambda bb, i, j: (0, j)),
                  pl.BlockSpec((d, tn), lambda bb, i, j: (0, j))],
        out_specs=pl.BlockSpec((1, tm, tn), lambda bb, i, j: (bb, i, j)),
        compiler_params=_cparams(("arbitrary", "arbitrary", "arbitrary")),
        name="branch_merge",
    )(o_lru, o_attn, proj, proj, wl_bf, wa_bf)


def _out_proj_kernel(m_ref, x_ref, gt_ref, w_ref, o_ref):
    o_ref[0] = x_ref[0] + gt_ref[0] * jnp.dot(m_ref[0], w_ref[...], preferred_element_type=F32)


def out_project(merged, x, gt, w_bf):
    b, s, d = x.shape
    tm = min(1024, s)
    tn = 512
    return pl.pallas_call(
        _out_proj_kernel,
        out_shape=jax.ShapeDtypeStruct((b, s, d), F32),
        grid=(b, s // tm, d // tn),
        in_specs=[pl.BlockSpec((1, tm, d), lambda bb, i, j: (bb, i, 0)),
                  pl.BlockSpec((1, tm, tn), lambda bb, i, j: (bb, i, j)),
                  _mod_spec(gt, tm, tn),
                  pl.BlockSpec((d, tn), lambda bb, i, j: (0, j))],
        out_specs=pl.BlockSpec((1, tm, tn), lambda bb, i, j: (bb, i, j)),
        compiler_params=_cparams(("arbitrary", "arbitrary", "arbitrary")),
        name="out_project",
    )(merged, x, gt, w_bf)


def _topk_rows(s, k, n):
    row = lax.broadcasted_iota(jnp.int32, s.shape, 0)
    slot = lax.broadcasted_iota(jnp.int32, (k, s.shape[1]), 0)
    vals = jnp.zeros((k, s.shape[1]), F32)
    idxs = jnp.zeros((k, s.shape[1]), jnp.int32)
    for r in range(k):
        m = jnp.max(s, axis=0, keepdims=True)
        am = jnp.min(jnp.where(s == m, row, n), axis=0, keepdims=True)
        vals = jnp.where(slot == r, m, vals)
        idxs = jnp.where(slot == r, am, idxs)
        s = jnp.where(row == am, -jnp.inf, s)
    return vals, idxs


def _peer_topk_kernel(q_ref, sk_ref, e_ref, g_ref):
    tt = q_ref.shape[0]
    kk = PEER_TOPK
    for hh in range(PEER_HEADS):
        sv, si = [], []
        for p in range(2):
            c0 = (hh * 2 + p) * PEER_HALF
            qc = q_ref[:, c0:c0 + PEER_HALF]
            s = lax.dot_general(sk_ref[hh, p], qc, (((1,), (1,)), ((), ())), precision=lax.Precision.HIGHEST,
                                preferred_element_type=F32)
            v, i = _topk_rows(s, kk, PEER_N_KEYS)
            sv.append(v)
            si.append(i)
        half = kk // 2
        cand = jnp.concatenate([sv[0][0:1] + sv[1]]
                               + [sv[0][a:a + 1] + sv[1][0:half] for a in range(1, half)]
                               + [sv[0][half:kk] + sv[1][0:1]], axis=0)
        cidx = jnp.concatenate([si[0][0:1] * PEER_N_KEYS + si[1]]
                               + [si[0][a:a + 1] * PEER_N_KEYS + si[1][0:half] for a in range(1, half)]
                               + [si[0][half:kk] * PEER_N_KEYS + si[1][0:1]], axis=0)
        ncand = cand.shape[0]
        row = lax.broadcasted_iota(jnp.int32, cand.shape, 0)
        slot = lax.broadcasted_iota(jnp.int32, (kk, tt), 0)
        fv = jnp.zeros((kk, tt), F32)
        eid = jnp.zeros((kk, tt), jnp.int32)
        for r in range(kk):
            m = jnp.max(cand, axis=0, keepdims=True)
            am = jnp.min(jnp.where(cand == m, row, ncand), axis=0, keepdims=True)
            pick = row == am
            fv = jnp.where(slot == r, m, fv)
            eid = jnp.where(slot == r, jnp.max(jnp.where(pick, cidx, -1), axis=0, keepdims=True), eid)
            cand = jnp.where(pick, -jnp.inf, cand)
        ex = jnp.exp(fv - fv[0:1])
        g_ref[hh * kk:(hh + 1) * kk, :] = ex / jnp.sum(ex, axis=0, keepdims=True)
        e_ref[hh * kk:(hh + 1) * kk, :] = eid


def peer_topk(qp, sub_keys):
    n = qp.shape[0]
    tt = 256 if n % 256 == 0 else n
    return pl.pallas_call(
        _peer_topk_kernel,
        out_shape=[jax.ShapeDtypeStruct((PEER_SEL, n), jnp.int32), jax.ShapeDtypeStruct((PEER_SEL, n), F32)],
        grid=(n // tt,),
        in_specs=[pl.BlockSpec((tt, qp.shape[1]), lambda i: (i, 0)),
                  pl.BlockSpec(sub_keys.shape, lambda i: (0, 0, 0, 0))],
        out_specs=[pl.BlockSpec((PEER_SEL, tt), lambda i: (0, i)), pl.BlockSpec((PEER_SEL, tt), lambda i: (0, i))],
        compiler_params=_cparams(("arbitrary",)),
        name="peer_topk",
    )(qp, sub_keys)


PEER_TOK_TILE = 8


def _peer_gather_kernel(e_ref, en_ref, h_ref, g_ref, uv_hbm, o_ref, buf0, buf1, sem, *, n):
    i = pl.program_id(0)
    nrow = PEER_TOK_TILE * PEER_SEL
    bufs = (buf0, buf1)

    def row_copy(idx_ref, r, dst):
        return pltpu.make_async_copy(uv_hbm.at[idx_ref[r]], bufs[dst].at[r], sem.at[dst])

    def tile_wait(which):
        pltpu.make_async_copy(uv_hbm.at[pl.ds(0, nrow)], bufs[which], sem.at[which]).wait()

    @pl.when(i == 0)
    def _():
        def first(r, c):
            row_copy(e_ref, r, 0).start()
            return c
        lax.fori_loop(0, nrow, first, 0, unroll=8)

    sub_id = lax.broadcasted_iota(jnp.int32, (ROW_SUB, LANES), 0)
    lane_id = lax.broadcasted_iota(jnp.int32, (ROW_SUB, LANES), 1)
    diag = (lane_id % ROW_SUB) == sub_id
    group_sum = ((lax.broadcasted_iota(jnp.int32, (LANES, LANES), 0) // ROW_SUB)
                 == (lax.broadcasted_iota(jnp.int32, (LANES, LANES), 1) // ROW_SUB)).astype(F32)

    def hi_lo(x):
        hi = x.astype(BF16)
        return jnp.concatenate([hi, (x - hi.astype(F32)).astype(BF16)], axis=0)

    def fold(x):
        return x[0:ROW_SUB] + x[ROW_SUB:2 * ROW_SUB]

    def tile(cur):
        nxt = 1 - cur
        buf = bufs[cur]
        tile_wait(cur)
        toks = range(PEER_TOK_TILE)

        def issue(t):
            for k in range(PEER_SEL):
                row_copy(en_ref, t * PEER_SEL + k, nxt).start(priority=k % 2)

        def rows_as_tile(ref, t):
            return jnp.concatenate([ref[t:t + 1, r * LANES:(r + 1) * LANES] for r in range(ROW_SUB)], axis=0)

        zs = []
        for t in toks:
            issue(t)
            ub = buf[pl.ds(t * PEER_SEL, PEER_SEL), pl.ds(0, ROW_SUB), :].reshape(PEER_SEL * ROW_SUB, LANES)
            y = fold(lax.dot_general(hi_lo(rows_as_tile(h_ref, t)), ub, (((1,), (1,)), ((), ())),
                                     preferred_element_type=F32))
            zs.append(jnp.concatenate(
                [jnp.sum(jnp.where(diag, y[:, r * LANES:(r + 1) * LANES], 0.0), axis=0, keepdims=True)
                 for r in range(ROW_SUB)], axis=0))
        act = jnp.dot(jnp.concatenate(zs, axis=0), group_sum, precision=lax.Precision.HIGHEST,
                      preferred_element_type=F32)
        for t in toks:
            whl = hi_lo(rows_as_tile(g_ref, t) * _gelu(act[t * ROW_SUB:(t + 1) * ROW_SUB])).astype(F32)
            wexp = jnp.concatenate(
                [jnp.concatenate([jnp.where(diag, whl[r:r + 1], 0.0),
                                  jnp.where(diag, whl[ROW_SUB + r:ROW_SUB + r + 1], 0.0)], axis=0)
                 for r in range(ROW_SUB)], axis=1).astype(BF16)
            vb = buf[pl.ds(t * PEER_SEL, PEER_SEL), pl.ds(ROW_SUB, ROW_SUB), :].reshape(PEER_SEL * ROW_SUB, LANES)
            acc = fold(jnp.dot(wexp, vb, preferred_element_type=F32))
            for s in range(ROW_SUB):
                o_ref[t:t + 1, s * LANES:(s + 1) * LANES] = acc[s:s + 1]

        @pl.when(i == n - 1)
        def _():
            tile_wait(nxt)

    @pl.when(i % 2 == 0)
    def _():
        tile(0)

    @pl.when(i % 2 == 1)
    def _():
        tile(1)


def peer_gather(eidx, h2, gates, uv):
    n, d = h2.shape
    tk = PEER_TOK_TILE
    nrow = tk * PEER_SEL
    nt = n // tk
    return pl.pallas_call(
        functools.partial(_peer_gather_kernel, n=nt),
        out_shape=jax.ShapeDtypeStruct((n, d), F32),
        grid=(nt,),
        in_specs=[pl.BlockSpec((nrow,), lambda i: (i,), memory_space=pltpu.SMEM),
                  pl.BlockSpec((nrow,), lambda i: (jnp.minimum(i + 1, nt - 1),), memory_space=pltpu.SMEM),
                  pl.BlockSpec((tk, d), lambda i: (i, 0)),
                  pl.BlockSpec((tk, d), lambda i: (i, 0)),
                  pl.BlockSpec(memory_space=pl.ANY)],
        out_specs=pl.BlockSpec((tk, d), lambda i: (i, 0)),
        scratch_shapes=[pltpu.VMEM((nrow, 2 * ROW_SUB, LANES), BF16), pltpu.VMEM((nrow, 2 * ROW_SUB, LANES), BF16),
                        pltpu.SemaphoreType.DMA((2,))],
        compiler_params=_cparams(("arbitrary",)),
        name="peer_gather",
    )(eidx.reshape(-1), eidx.reshape(-1), h2, jnp.repeat(gates, ROW_SUB, axis=1), uv)


def _final_kernel(x_ref, ff_ref, gt_ref, g_ref, o_ref):
    x = x_ref[0] + gt_ref[0] * ff_ref[0]
    o_ref[0] = x * lax.rsqrt(jnp.mean(x * x, axis=-1, keepdims=True) + EPS) * g_ref[...]


def final_norm(x1, ff, gt, g):
    b, s, d = x1.shape
    tm = min(512, s)
    gspec = (pl.BlockSpec((1, 1, d), lambda bb, i: (bb, 0, 0)) if gt.shape[1] == 1
             else pl.BlockSpec((1, tm, d), lambda bb, i: (bb, i, 0)))
    return pl.pallas_call(
        _final_kernel,
        out_shape=jax.ShapeDtypeStruct((b, s, d), F32),
        grid=(b, s // tm),
        in_specs=[pl.BlockSpec((1, tm, d), lambda bb, i: (bb, i, 0)),
                  pl.BlockSpec((1, tm, d), lambda bb, i: (bb, i, 0)),
                  gspec,
                  pl.BlockSpec((1, d), lambda bb, i: (0, 0))],
        out_specs=pl.BlockSpec((1, tm, d), lambda bb, i: (bb, i, 0)),
        compiler_params=_cparams(("arbitrary", "arbitrary")),
        name="final_norm",
    )(x1, ff, gt, g.reshape(1, d))


def _peer_block(x1, sh2, sc2, gt2, norm2_g, wq_bf, sub_keys, uv, final_g, cos, sin):
    b, s, d = x1.shape
    n = b * s
    qp, h2 = norm_proj(x1, sh2, sc2, norm2_g, wq_bf, cos, sin, rope_cols=0, emit_h=True)
    qp2 = qp.reshape(n, -1)
    n_pad = -(-n // LANES) * LANES
    if n_pad != n:
        qp2 = jnp.pad(qp2, ((0, n_pad - n), (0, 0)))
    e_t, g_t = peer_topk(qp2, sub_keys)
    eidx = e_t.T[:n]
    gates = g_t.T[:n]
    ff = peer_gather(eidx, h2.reshape(n, d), gates, uv)
    return final_norm(x1, ff.reshape(b, s, d), gt2, final_g)


def kernel(x_prompt, x_sample, c_prompt, c_sample, cache_k, cache_v, state_lru, state_conv, page_table, norm1_g, w_ada, b_ada, w_in, conv_w, conv_b, lru_wa, lru_ba, lru_wx, lru_bx, lru_lam, w_br_lru, w_br_attn, w_out, norm2_g, peer_wq, peer_subkeys, peer_u, peer_v, final_g):
    assert w_ada.shape[0] == 1, "single layer"
    b, s, d = x_prompt.shape
    db = x_sample.shape[0]
    n_pages = page_table.shape[1]
    past_len = n_pages * PAGE_SIZE
    assert s % MOBA_BLOCK == 0 and past_len % MOBA_BLOCK == 0 and x_sample.shape[1] == 1

    w_in_bf = w_in[0].astype(BF16)
    wl_bf, wa_bf, wo_bf = w_br_lru[0].astype(BF16), w_br_attn[0].astype(BF16), w_out[0].astype(BF16)
    wq_bf = peer_wq[0].astype(BF16)
    lwa_bf, lwx_bf = lru_wa[0].astype(BF16), lru_wx[0].astype(BF16)
    uv = jnp.concatenate([peer_u[0].reshape(-1, ROW_SUB, LANES), peer_v[0].reshape(-1, ROW_SUB, LANES)],
                         axis=1).astype(BF16)

    mod = ada_project(jnp.concatenate([c_prompt, c_sample], axis=0), w_ada[0], b_ada[0])
    mod_p = [m.reshape(b, 1, d) for m in jnp.split(mod[:b], N_MOD, axis=-1)]
    mod_s = [m.reshape(1, db, d) for m in jnp.split(mod[b:], N_MOD, axis=-1)]

    cos_p, sin_p = rope_tables(jnp.arange(s, dtype=jnp.int32))
    (proj,) = norm_proj(x_prompt, mod_p[0], mod_p[1], norm1_g[0], w_in_bf, cos_p, sin_p,
                        rope_cols=D_ATTN + D_KV, emit_h=False)
    kmean = moba_block_means(proj)
    o_attn = moba_prompt_attention(proj, kmean)
    o_lru, h_last_p = lru_prompt(proj, jnp.zeros((b, VREG_SUBLANES, D_LRU), F32), jnp.zeros((b, 1, D_LRU), F32),
                                 conv_w[0], conv_b[0], lwa_bf, lwx_bf, lru_ba[0], lru_bx[0], lru_lam[0])
    merged = branch_merge(o_lru, o_attn, proj, wl_bf, wa_bf)
    x1 = out_project(merged, x_prompt, mod_p[2], wo_bf)
    y_prompt = _peer_block(x1, mod_p[3], mod_p[4], mod_p[5], norm2_g[0], wq_bf, peer_subkeys[0], uv,
                           final_g, cos_p, sin_p)
    k_prompt = proj[:, :, COL_K:COL_K + D_KV].reshape(1, b, s, N_KV_HEADS, HEAD_DIM)
    v_prompt = proj[:, :, COL_V:COL_V + D_KV].reshape(1, b, s, N_KV_HEADS, HEAD_DIM)
    conv_prompt = proj[:, s - (CONV_W - 1):, COL_XL:COL_XL + D_LRU].reshape(1, b, CONV_W - 1, D_LRU)

    xs = x_sample.reshape(1, db, d)
    cos_s, sin_s = rope_tables(jnp.full((db,), past_len, jnp.int32))
    (proj_s,) = norm_proj(xs, mod_s[0], mod_s[1], norm1_g[0], w_in_bf, cos_s, sin_s,
                          rope_cols=D_ATTN + D_KV, emit_h=False)
    q_s = proj_s[0, :, COL_Q:COL_Q + D_ATTN].reshape(db, N_HEADS, HEAD_DIM)
    k_new = proj_s[0, :, COL_K:COL_K + D_KV].reshape(db, 1, D_KV)
    v_new = proj_s[0, :, COL_V:COL_V + D_KV].reshape(db, 1, D_KV)
    n_pool = cache_k.shape[1]
    cache_k3 = cache_k[0].reshape(n_pool, PAGE_SIZE * N_KV_HEADS, HEAD_DIM)
    cache_v3 = cache_v[0].reshape(n_pool, PAGE_SIZE * N_KV_HEADS, HEAD_DIM)
    kmean_s = sample_block_means(cache_k3, page_table)
    sel = sample_select(q_s, kmean_s)[:, :, :MOBA_TOPK].reshape(db, N_HEADS * MOBA_TOPK)
    o_attn_s = sample_attention(q_s, k_new, v_new, cache_k3, cache_v3, page_table, sel)
    o_lru_s, h_last_s = lru_step(proj_s, jnp.transpose(state_conv[0], (1, 0, 2)), state_lru[0],
                                 conv_w[0], conv_b[0], lwa_bf, lwx_bf, lru_ba[0], lru_bx[0], lru_lam[0])
    merged_s = branch_merge(o_lru_s.reshape(1, db, d), o_attn_s.astype(BF16).reshape(1, db, d), proj_s, wl_bf, wa_bf)
    x1_s = out_project(merged_s, xs, mod_s[2], wo_bf)
    y_sample = _peer_block(x1_s, mod_s[3], mod_s[4], mod_s[5], norm2_g[0], wq_bf, peer_subkeys[0], uv,
                           final_g, cos_s, sin_s)
    xl_s = proj_s[0, :, COL_XL:COL_XL + D_LRU]
    conv_sample = jnp.concatenate([state_conv[0][:, 1:], xl_s[:, None, :]], axis=1)[None]

    return (y_prompt, y_sample.reshape(db, 1, d), k_prompt, v_prompt,
            h_last_p.reshape(1, b, D_LRU), conv_prompt,
            k_new.reshape(1, db, 1, N_KV_HEADS, HEAD_DIM), v_new.reshape(1, db, 1, N_KV_HEADS, HEAD_DIM),
            h_last_s.reshape(1, db, D_LRU), conv_sample)
```

```python
import functools
import math

import jax
import jax.numpy as jnp
from jax import lax
from jax.experimental import pallas as pl
from jax.experimental.pallas import tpu as pltpu

D_MODEL = 2048
PAGE_SIZE = 128
N_HEADS = 16
N_KV_HEADS = 4
HEAD_DIM = 128
GROUP = N_HEADS // N_KV_HEADS
D_ATTN = N_HEADS * HEAD_DIM
D_KV = N_KV_HEADS * HEAD_DIM
ROPE_THETA = 10000.0
MOBA_BLOCK = 256
MOBA_TOPK = 3
D_LRU = D_MODEL
LRU_HEADS = 16
LRU_BLOCK = D_LRU // LRU_HEADS
CONV_W = 4
LRU_C = 8.0
PEER_HEADS = 8
PEER_N_KEYS = 128
PEER_HALF = 128
PEER_TOPK = 16
PEER_SEL = PEER_HEADS * PEER_TOPK
N_MOD = 6
EPS = 1e-6

COL_Q = 0
COL_K = D_ATTN
COL_V = D_ATTN + D_KV
COL_XL = D_ATTN + 2 * D_KV
COL_GL = COL_XL + D_LRU
COL_GB = COL_GL + D_LRU
IN_COLS = COL_GB + 2 * D_MODEL

VREG_SUBLANES = 8
LANES = 128
ROW_SUB = D_MODEL // LANES
NEG = -1e30
BF16 = jnp.bfloat16
F32 = jnp.float32
VMEM_LIMIT = 52 * 1024 * 1024


def _cparams(sem):
    return pltpu.CompilerParams(dimension_semantics=sem, vmem_limit_bytes=VMEM_LIMIT)


def _gelu(x):
    return 0.5 * x * (1.0 + lax.erf(x * (1.0 / math.sqrt(2.0))))


def _mod_spec(mod, tm, tn, tiled_cols=True):
    col = (lambda j: j) if tiled_cols else (lambda j: 0)
    if mod.shape[1] == 1:
        return pl.BlockSpec((1, 1, tn), lambda b, i, j: (b, 0, col(j)))
    return pl.BlockSpec((1, tm, tn), lambda b, i, j: (b, i, col(j)))


def _ada_kernel(c_ref, w_ref, b_ref, o_ref):
    o_ref[...] = jnp.dot(c_ref[...].astype(BF16), w_ref[...].astype(BF16),
                         preferred_element_type=F32) + b_ref[...]


def ada_project(c_all, w_ada, b_ada):
    m, d = c_all.shape
    n = w_ada.shape[1]
    tn = 1024
    return pl.pallas_call(
        _ada_kernel,
        out_shape=jax.ShapeDtypeStruct((m, n), F32),
        grid=(n // tn,),
        in_specs=[pl.BlockSpec((m, d), lambda j: (0, 0)),
                  pl.BlockSpec((d, tn), lambda j: (0, j)),
                  pl.BlockSpec((1, tn), lambda j: (0, j))],
        out_specs=pl.BlockSpec((m, tn), lambda j: (0, j)),
        compiler_params=_cparams(("arbitrary",)),
        name="ada_project",
    )(c_all, w_ada, b_ada.reshape(1, n))


def _norm_proj_kernel(x_ref, sh_ref, sc_ref, g_ref, w_ref, cos_ref, sin_ref, o_ref, hs_ref, *, rope_tiles):
    j = pl.program_id(2)

    @pl.when(j == 0)
    def _():
        x = x_ref[0]
        y = x * lax.rsqrt(jnp.mean(x * x, axis=-1, keepdims=True) + EPS) * g_ref[...]
        hs_ref[...] = (y * (1.0 + sc_ref[0]) + sh_ref[0]).astype(BF16)

    acc = jnp.dot(hs_ref[...], w_ref[...].astype(BF16), preferred_element_type=F32)

    if rope_tiles:
        @pl.when(j < rope_tiles)
        def _():
            cos = cos_ref[...]
            sin = sin_ref[...]
            parts = []
            for hh in range(acc.shape[1] // HEAD_DIM):
                a = acc[:, hh * HEAD_DIM:(hh + 1) * HEAD_DIM]
                parts.append(a * cos + pltpu.roll(a, HEAD_DIM // 2, 1) * sin)
            o_ref[0] = jnp.concatenate(parts, axis=1)

        @pl.when(j >= rope_tiles)
        def _():
            o_ref[0] = acc
    else:
        o_ref[0] = acc


def norm_proj(x, sh, sc, g, w, cos, sin, *, rope_cols):
    b, s, d = x.shape
    n = w.shape[1]
    tm = min(1024, s)
    tn = 512
    assert s % tm == 0 and n % tn == 0 and rope_cols % tn == 0
    return pl.pallas_call(
        functools.partial(_norm_proj_kernel, rope_tiles=rope_cols // tn),
        out_shape=jax.ShapeDtypeStruct((b, s, n), F32),
        grid=(b, s // tm, n // tn),
        in_specs=[pl.BlockSpec((1, tm, d), lambda bb, i, j: (bb, i, 0)),
                  _mod_spec(sh, tm, d, False), _mod_spec(sc, tm, d, False),
                  pl.BlockSpec((1, d), lambda bb, i, j: (0, 0)),
                  pl.BlockSpec((d, tn), lambda bb, i, j: (0, j)),
                  pl.BlockSpec((tm, HEAD_DIM), lambda bb, i, j: (i, 0)),
                  pl.BlockSpec((tm, HEAD_DIM), lambda bb, i, j: (i, 0))],
        out_specs=pl.BlockSpec((1, tm, tn), lambda bb, i, j: (bb, i, j)),
        scratch_shapes=[pltpu.VMEM((tm, d), BF16)],
        compiler_params=_cparams(("arbitrary", "arbitrary", "arbitrary")),
        name="norm_proj",
    )(x, sh, sc, g.reshape(1, d), w, cos, sin)


def rope_tables(pos):
    half = HEAD_DIM // 2
    inv = jnp.exp(-math.log(ROPE_THETA) * jnp.arange(half, dtype=F32) / half)
    ang = pos.astype(F32)[:, None] * inv[None, :]
    cos, sin = jnp.cos(ang), jnp.sin(ang)
    return jnp.concatenate([cos, cos], axis=1), jnp.concatenate([-sin, sin], axis=1)


def _kmean_kernel(k_ref, o_ref):
    o_ref[0, 0] = jnp.mean(k_ref[0], axis=0, keepdims=True)


def moba_block_means(proj):
    b, s, _ = proj.shape
    nb = s // MOBA_BLOCK
    out = pl.pallas_call(
        _kmean_kernel,
        out_shape=jax.ShapeDtypeStruct((b, nb, 1, D_KV), F32),
        grid=(b, nb),
        in_specs=[pl.BlockSpec((1, MOBA_BLOCK, D_KV), lambda bb, i: (bb, i, COL_K // D_KV))],
        out_specs=pl.BlockSpec((1, 1, 1, D_KV), lambda bb, i: (bb, i, 0, 0)),
        compiler_params=_cparams(("arbitrary", "arbitrary")),
        name="moba_block_means",
    )(proj)
    return out.reshape(b, nb, D_KV)


def _top3_bias(gate, own_blk, nb):
    blk = lax.broadcasted_iota(jnp.int32, gate.shape, 0)
    past = blk < own_blk
    work = jnp.where(past, gate, -jnp.inf)
    sel = jnp.zeros(gate.shape, jnp.bool_)
    for _ in range(min(MOBA_TOPK, nb)):
        m = jnp.max(work, axis=0, keepdims=True)
        idx = jnp.min(jnp.where(work == m, blk, nb), axis=0, keepdims=True)
        pick = blk == idx
        sel = jnp.logical_or(sel, jnp.logical_and(pick, past))
        work = jnp.where(pick, -jnp.inf, work)
    return jnp.where(sel, 0.0, NEG).astype(F32)


def _moba_kernel(q_ref, k_ref, v_ref, km_ref, o_ref, kb_ref, vt_ref, qa_ref, m_ref, l_ref, acc_ref, sa_ref, sb_ref):
    qi = pl.program_id(2)
    nb = km_ref.shape[1]
    tq = MOBA_BLOCK
    rows = GROUP * tq
    hd = HEAD_DIM
    assert nb <= hd

    @pl.when(qi == 0)
    def _():
        lane = lax.broadcasted_iota(jnp.int32, (MOBA_BLOCK, hd), 1)

        def cp(jb, c):
            st = pl.multiple_of(jb * MOBA_BLOCK, MOBA_BLOCK)
            kb_ref[pl.ds(st, MOBA_BLOCK), 0:hd] = k_ref[0, pl.ds(st, MOBA_BLOCK), :].astype(BF16)
            kb_ref[pl.ds(st, MOBA_BLOCK), hd:2 * hd] = jnp.where(lane == jb, 1.0, 0.0).astype(BF16)
            vt_ref[jb] = v_ref[0, pl.ds(st, MOBA_BLOCK), :].T.astype(BF16)
            return c
        lax.fori_loop(0, nb, cp, 0)

    q = q_ref[0]
    qcat = jnp.concatenate([q[:, g * hd:(g + 1) * hd] for g in range(GROUP)], axis=0)
    qt = qcat.T
    gate = jnp.dot(km_ref[0], qt, precision=lax.Precision.HIGHEST, preferred_element_type=F32)
    bias = _top3_bias(gate, qi, nb)
    qa_ref[0:hd] = (qt * (hd ** -0.5)).astype(BF16)
    qa_ref[hd:2 * hd] = jnp.concatenate([bias, jnp.zeros((hd - nb, rows), F32)], axis=0).astype(BF16)

    st = pl.multiple_of(qi * MOBA_BLOCK, MOBA_BLOCK)
    s = jnp.dot(kb_ref[pl.ds(st, MOBA_BLOCK), 0:hd], qa_ref[0:hd], preferred_element_type=F32)
    key_t = lax.broadcasted_iota(jnp.int32, (MOBA_BLOCK, rows), 0)
    q_t = lax.broadcasted_iota(jnp.int32, (MOBA_BLOCK, rows), 1) % tq
    s = jnp.where(key_t <= q_t, s, NEG)
    m0 = jnp.max(s, axis=0, keepdims=True)
    p = jnp.exp(s - m0)
    m_ref[...] = m0
    l_ref[...] = jnp.sum(p, axis=0, keepdims=True)
    acc_ref[...] = jnp.dot(vt_ref[qi], p.astype(BF16), preferred_element_type=F32)

    def scores(j):
        stj = pl.multiple_of(j * MOBA_BLOCK, MOBA_BLOCK)
        return jnp.dot(kb_ref[pl.ds(stj, MOBA_BLOCK), :], qa_ref[...], preferred_element_type=F32)

    def update(sj, j):
        m_old = m_ref[...]
        m_new = jnp.maximum(m_old, jnp.max(sj, axis=0, keepdims=True))
        alpha = jnp.exp(m_old - m_new)
        pj = jnp.exp(sj - m_new)
        l_ref[...] = alpha * l_ref[...] + jnp.sum(pj, axis=0, keepdims=True)
        acc_ref[...] = alpha * acc_ref[...] + jnp.dot(vt_ref[j], pj.astype(BF16), preferred_element_type=F32)
        m_ref[...] = m_new

    last = jnp.maximum(qi - 1, 0)
    sa_ref[...] = scores(0)

    def two(jj, c):
        j0 = 2 * jj
        sb_ref[...] = scores(j0 + 1)
        update(sa_ref[...], j0)
        sa_ref[...] = scores(jnp.minimum(j0 + 2, last))
        update(sb_ref[...], j0 + 1)
        return c

    lax.fori_loop(0, qi // 2, two, 0)

    @pl.when(qi % 2 == 1)
    def _():
        update(sa_ref[...], qi - 1)

    o = (acc_ref[...] / l_ref[...]).T
    for g in range(GROUP):
        o_ref[0, :, g * HEAD_DIM:(g + 1) * HEAD_DIM] = o[g * tq:(g + 1) * tq].astype(o_ref.dtype)


def moba_prompt_attention(proj, kmean):
    b, s, _ = proj.shape
    nb = s // MOBA_BLOCK
    rows = GROUP * MOBA_BLOCK
    gw = GROUP * HEAD_DIM
    return pl.pallas_call(
        _moba_kernel,
        out_shape=jax.ShapeDtypeStruct((b, s, D_ATTN), BF16),
        grid=(b, N_KV_HEADS, nb),
        in_specs=[pl.BlockSpec((1, MOBA_BLOCK, gw), lambda bb, hk, qi: (bb, qi, hk)),
                  pl.BlockSpec((1, s, HEAD_DIM), lambda bb, hk, qi: (bb, 0, COL_K // HEAD_DIM + hk)),
                  pl.BlockSpec((1, s, HEAD_DIM), lambda bb, hk, qi: (bb, 0, COL_V // HEAD_DIM + hk)),
                  pl.BlockSpec((1, nb, HEAD_DIM), lambda bb, hk, qi: (bb, 0, hk))],
        out_specs=pl.BlockSpec((1, MOBA_BLOCK, gw), lambda bb, hk, qi: (bb, qi, hk)),
        scratch_shapes=[pltpu.VMEM((s, 2 * HEAD_DIM), BF16),
                        pltpu.VMEM((nb, HEAD_DIM, MOBA_BLOCK), BF16),
                        pltpu.VMEM((2 * HEAD_DIM, rows), BF16),
                        pltpu.VMEM((1, rows), F32),
                        pltpu.VMEM((1, rows), F32),
                        pltpu.VMEM((HEAD_DIM, rows), F32),
                        pltpu.VMEM((MOBA_BLOCK, rows), F32),
                        pltpu.VMEM((MOBA_BLOCK, rows), F32)],
        compiler_params=_cparams(("arbitrary", "arbitrary", "arbitrary")),
        name="moba_prompt_attention",
    )(proj, proj, proj, kmean)


def _page_sum_kernel(pt_ref, *refs, pages_per_step):
    del pt_ref
    o_ref = refs[pages_per_step]
    sub = VREG_SUBLANES
    for r in range(0, pages_per_step, 2):
        grp = (jnp.sum(refs[r][0].reshape(-1, sub, HEAD_DIM), axis=0)
               + jnp.sum(refs[r + 1][0].reshape(-1, sub, HEAD_DIM), axis=0))
        o_ref[0, 0, r // 2] = (grp[0:N_KV_HEADS] + grp[N_KV_HEADS:sub]) * (1.0 / MOBA_BLOCK)


def sample_block_means(cache_k3, page_table):
    db, n_pages = page_table.shape
    pps = 8 if n_pages % 8 == 0 else 2
    assert n_pages % pps == 0
    bps = pps // 2
    nbp = n_pages // 2
    prow = PAGE_SIZE * N_KV_HEADS

    def page_spec(r):
        return pl.BlockSpec((1, prow, HEAD_DIM), lambda bb, p, pt: (pt[bb, p * pps + r], 0, 0))

    out = pl.pallas_call(
        functools.partial(_page_sum_kernel, pages_per_step=pps),
        out_shape=jax.ShapeDtypeStruct((db, n_pages // pps, bps, N_KV_HEADS, HEAD_DIM), F32),
        grid_spec=pltpu.PrefetchScalarGridSpec(
            num_scalar_prefetch=1,
            grid=(db, n_pages // pps),
            in_specs=[page_spec(r) for r in range(pps)],
            out_specs=pl.BlockSpec((1, 1, bps, N_KV_HEADS, HEAD_DIM), lambda bb, p, pt: (bb, p, 0, 0, 0)),
        ),
        compiler_params=_cparams(("arbitrary", "arbitrary")),
        name="sample_block_means",
    )(page_table, *([cache_k3] * pps))
    return out.reshape(db, nbp, D_KV)


def _sample_select_kernel(q_ref, km_ref, o_ref):
    nbp = km_ref.shape[1]
    q = q_ref[0]
    head_kv = lax.broadcasted_iota(jnp.int32, (N_HEADS, nbp), 0) // GROUP
    gate = jnp.zeros((N_HEADS, nbp), F32)
    for hk in range(N_KV_HEADS):
        km = km_ref[0, :, hk * HEAD_DIM:(hk + 1) * HEAD_DIM]
        gk = lax.dot_general(q, km, (((1,), (1,)), ((), ())), precision=lax.Precision.HIGHEST,
                             preferred_element_type=F32)
        gate = jnp.where(head_kv == hk, gk, gate)
    blk = lax.broadcasted_iota(jnp.int32, gate.shape, 1)
    lane = lax.broadcasted_iota(jnp.int32, (N_HEADS, LANES), 1)
    out = jnp.zeros((N_HEADS, LANES), jnp.int32)
    work = gate
    for r in range(MOBA_TOPK):
        m = jnp.max(work, axis=1, keepdims=True)
        idx = jnp.min(jnp.where(work == m, blk, nbp), axis=1, keepdims=True)
        out = jnp.where(lane == r, idx, out)
        work = jnp.where(blk == idx, -jnp.inf, work)
    o_ref[0] = out


def sample_select(q_s, kmean_s):
    db = q_s.shape[0]
    nbp = kmean_s.shape[1]
    return pl.pallas_call(
        _sample_select_kernel,
        out_shape=jax.ShapeDtypeStruct((db, N_HEADS, LANES), jnp.int32),
        grid=(db,),
        in_specs=[pl.BlockSpec((1, N_HEADS, HEAD_DIM), lambda bb: (bb, 0, 0)),
                  pl.BlockSpec((1, nbp, D_KV), lambda bb: (bb, 0, 0))],
        out_specs=pl.BlockSpec((1, N_HEADS, LANES), lambda bb: (bb, 0, 0)),
        compiler_params=_cparams(("arbitrary",)),
        name="sample_select",
    )(q_s, kmean_s)


def _sample_attn_kernel(pt_ref, sel_ref, q_ref, *refs, n_pages):
    del pt_ref, sel_ref
    kp_refs = refs[:n_pages]
    vp_refs = refs[n_pages:2 * n_pages]
    kn_ref, vn_ref, o_ref, s_ref = refs[2 * n_pages:]
    h = pl.program_id(1)
    hk = h // GROUP
    prow = PAGE_SIZE * N_KV_HEADS
    qh = q_ref[0, pl.ds(h, 1), :] * (HEAD_DIM ** -0.5)
    qb = jnp.broadcast_to(qh, (HEAD_DIM, HEAD_DIM)).astype(BF16)

    def pick_kv(row):
        out = jnp.zeros((1, HEAD_DIM), F32)
        for kk in range(N_KV_HEADS):
            out = out + jnp.where(hk == kk, row[:, kk * HEAD_DIM:(kk + 1) * HEAD_DIM], 0.0)
        return out

    s_self = jnp.sum(pick_kv(kn_ref[0]) * qh, axis=1, keepdims=True)
    row_kv = lax.broadcasted_iota(jnp.int32, (prow, HEAD_DIM), 0) % N_KV_HEADS
    m = jnp.broadcast_to(s_self, (1, HEAD_DIM))
    for j in range(n_pages):
        sj = lax.dot_general(kp_refs[j][0].astype(BF16), qb, (((1,), (1,)), ((), ())), preferred_element_type=F32)
        sj = jnp.where(row_kv == hk, sj, NEG)
        s_ref[j] = sj
        m = jnp.maximum(m, jnp.max(sj, axis=0, keepdims=True))
    p_self = jnp.exp(s_self - m)
    l = p_self
    acc = p_self * pick_kv(vn_ref[0])
    for j in range(n_pages):
        pj = jnp.exp(s_ref[j] - m)
        l = l + jnp.sum(pj, axis=0, keepdims=True)
        acc = acc + jnp.sum(pj * vp_refs[j][0], axis=0, keepdims=True)
    o_ref[0, 0] = acc / l


def sample_attention(q_s, k_new, v_new, cache_k3, cache_v3, page_table, sel):
    db = q_s.shape[0]
    pages_per_blk = MOBA_BLOCK // PAGE_SIZE
    n_pages = MOBA_TOPK * pages_per_blk
    prow = PAGE_SIZE * N_KV_HEADS

    def page_spec(j):
        def idx(bb, h, pt, sl):
            return (pt[bb, sl[bb, h * MOBA_TOPK + j // pages_per_blk] * pages_per_blk + j % pages_per_blk], 0, 0)
        return pl.BlockSpec((1, prow, HEAD_DIM), idx)

    out = pl.pallas_call(
        functools.partial(_sample_attn_kernel, n_pages=n_pages),
        out_shape=jax.ShapeDtypeStruct((db, N_HEADS, 1, HEAD_DIM), F32),
        grid_spec=pltpu.PrefetchScalarGridSpec(
            num_scalar_prefetch=2,
            grid=(db, N_HEADS),
            in_specs=([pl.BlockSpec((1, N_HEADS, HEAD_DIM), lambda bb, h, pt, sl: (bb, 0, 0))]
                      + [page_spec(j) for j in range(n_pages)] + [page_spec(j) for j in range(n_pages)]
                      + [pl.BlockSpec((1, 1, D_KV), lambda bb, h, pt, sl: (bb, 0, 0)),
                         pl.BlockSpec((1, 1, D_KV), lambda bb, h, pt, sl: (bb, 0, 0))]),
            out_specs=pl.BlockSpec((1, 1, 1, HEAD_DIM), lambda bb, h, pt, sl: (bb, h, 0, 0)),
            scratch_shapes=[pltpu.VMEM((n_pages, prow, HEAD_DIM), F32)],
        ),
        compiler_params=_cparams(("arbitrary", "arbitrary")),
        name="sample_attention",
    )(page_table, sel, q_s, *([cache_k3] * n_pages), *([cache_v3] * n_pages), k_new, v_new)
    return out.reshape(db, D_ATTN)


def _lru_gates(xc, wa_ref, wx_ref, ba, bx, lam):
    nh = xc.shape[1] // LRU_BLOCK
    xb = xc.astype(BF16)
    ra, rx = [], []
    for hh in range(nh):
        xs = xb[:, hh * LRU_BLOCK:(hh + 1) * LRU_BLOCK]
        ra.append(jnp.dot(xs, wa_ref[hh], preferred_element_type=F32))
        rx.append(jnp.dot(xs, wx_ref[hh], preferred_element_type=F32))
    r = jax.nn.sigmoid(jnp.concatenate(ra, axis=1) + ba)
    gi = jax.nn.sigmoid(jnp.concatenate(rx, axis=1) + bx)
    log_a = (-LRU_C * r) * jax.nn.softplus(-lam)
    a = jnp.exp(log_a)
    bxs = jnp.sqrt(1.0 - jnp.exp(2.0 * log_a)) * (gi * xc)
    return a, bxs


def _lru_kernel(xl_ref, gl_ref, cprev_ref, h0_ref, cw_ref, cb_ref, wa_ref, wx_ref, ba_ref, bx_ref, lam_ref,
                o_ref, hl_ref, xprev_ref, hc_ref, a_ref, b_ref):
    ti = pl.program_id(2)
    tt = xl_ref.shape[1]
    sub = VREG_SUBLANES

    @pl.when(ti == 0)
    def _():
        xprev_ref[...] = cprev_ref[0]
        hc_ref[...] = jnp.broadcast_to(h0_ref[0], hc_ref.shape)

    x = xl_ref[0]
    cw = cw_ref[...]
    xc = cb_ref[...] + x * cw[CONV_W - 1:CONV_W]
    first = jnp.concatenate([xprev_ref[...], x[0:sub]], axis=0)
    xc_first = cb_ref[...] + x[0:sub] * cw[CONV_W - 1:CONV_W]
    for d in range(1, CONV_W):
        wj = cw[CONV_W - 1 - d:CONV_W - d]
        xc = xc + pltpu.roll(x, d, 0) * wj
        xc_first = xc_first + pltpu.roll(first, d, 0)[sub:2 * sub] * wj
    xc = jnp.concatenate([xc_first, xc[sub:]], axis=0)
    xprev_ref[...] = x[tt - sub:tt]

    a, bxs = _lru_gates(xc, wa_ref, wx_ref, ba_ref[...], bx_ref[...], lam_ref[...])

    rowm = lax.broadcasted_iota(jnp.int32, a.shape, 0) % sub
    for d in (1, 2, 4):
        ok = rowm >= d
        a_sh = pltpu.roll(a, d, 0)
        b_sh = pltpu.roll(bxs, d, 0)
        bxs = jnp.where(ok, a * b_sh + bxs, bxs)
        a = jnp.where(ok, a * a_sh, a)
    a_ref[...] = a
    b_ref[...] = bxs

    def grp(gidx, hprev):
        st = pl.multiple_of(gidx * sub, sub)
        hg = a_ref[pl.ds(st, sub), :] * hprev + b_ref[pl.ds(st, sub), :]
        b_ref[pl.ds(st, sub), :] = hg
        return jnp.broadcast_to(hg[sub - 1:sub], hprev.shape)

    hlast = lax.fori_loop(0, tt // sub, grp, hc_ref[...])
    hc_ref[...] = hlast
    o_ref[0] = (b_ref[...] * _gelu(gl_ref[0])).astype(o_ref.dtype)

    @pl.when(ti == pl.num_programs(2) - 1)
    def _():
        hl_ref[0] = hlast[0:1]


def lru_prompt(proj, conv_prev8, h0, conv_w, conv_b, wa_bf, wx_bf, ba, bx, lam):
    b, s, _ = proj.shape
    c = D_LRU
    tc = 512
    tt = min(512, s)
    assert s % tt == 0
    hpt = tc // LRU_BLOCK
    vec = lambda: pl.BlockSpec((1, tc), lambda bb, ci, ti: (0, ci))
    return pl.pallas_call(
        _lru_kernel,
        out_shape=[jax.ShapeDtypeStruct((b, s, c), BF16), jax.ShapeDtypeStruct((b, 1, c), F32)],
        grid=(b, c // tc, s // tt),
        in_specs=[pl.BlockSpec((1, tt, tc), lambda bb, ci, ti: (bb, ti, COL_XL // tc + ci)),
                  pl.BlockSpec((1, tt, tc), lambda bb, ci, ti: (bb, ti, COL_GL // tc + ci)),
                  pl.BlockSpec((1, VREG_SUBLANES, tc), lambda bb, ci, ti: (bb, 0, ci)),
                  pl.BlockSpec((1, 1, tc), lambda bb, ci, ti: (bb, 0, ci)),
                  pl.BlockSpec((CONV_W, tc), lambda bb, ci, ti: (0, ci)),
                  vec(),
                  pl.BlockSpec((hpt, LRU_BLOCK, LRU_BLOCK), lambda bb, ci, ti: (ci, 0, 0)),
                  pl.BlockSpec((hpt, LRU_BLOCK, LRU_BLOCK), lambda bb, ci, ti: (ci, 0, 0)),
                  vec(), vec(), vec()],
        out_specs=[pl.BlockSpec((1, tt, tc), lambda bb, ci, ti: (bb, ti, ci)),
                   pl.BlockSpec((1, 1, tc), lambda bb, ci, ti: (bb, 0, ci))],
        scratch_shapes=[pltpu.VMEM((VREG_SUBLANES, tc), F32), pltpu.VMEM((VREG_SUBLANES, tc), F32),
                        pltpu.VMEM((tt, tc), F32), pltpu.VMEM((tt, tc), F32)],
        compiler_params=_cparams(("arbitrary", "arbitrary", "arbitrary")),
        name="lru_prompt",
    )(proj, proj, conv_prev8, h0, conv_w, conv_b.reshape(1, c), wa_bf, wx_bf,
      ba.reshape(1, c), bx.reshape(1, c), lam.reshape(1, c))


def _lru_step_kernel(xl_ref, gl_ref, cprev_ref, h0_ref, cw_ref, cb_ref, wa_ref, wx_ref, ba_ref, bx_ref, lam_ref,
                     o_ref, hl_ref):
    x = xl_ref[0]
    cw = cw_ref[...]
    xc = cb_ref[...] + x * cw[CONV_W - 1:CONV_W]
    for j in range(CONV_W - 1):
        xc = xc + cprev_ref[j] * cw[j:j + 1]
    a, bxs = _lru_gates(xc, wa_ref, wx_ref, ba_ref[...], bx_ref[...], lam_ref[...])
    h = a * h0_ref[...] + bxs
    hl_ref[...] = h
    o_ref[...] = (h * _gelu(gl_ref[0])).astype(o_ref.dtype)


def lru_step(proj_s, conv_prev_t, h0, conv_w, conv_b, wa_bf, wx_bf, ba, bx, lam):
    _, db, _ = proj_s.shape
    c = D_LRU
    tc = 1024
    hpt = tc // LRU_BLOCK
    vec = lambda: pl.BlockSpec((1, tc), lambda ci: (0, ci))
    return pl.pallas_call(
        _lru_step_kernel,
        out_shape=[jax.ShapeDtypeStruct((db, c), BF16), jax.ShapeDtypeStruct((db, c), F32)],
        grid=(c // tc,),
        in_specs=[pl.BlockSpec((1, db, tc), lambda ci: (0, 0, COL_XL // tc + ci)),
                  pl.BlockSpec((1, db, tc), lambda ci: (0, 0, COL_GL // tc + ci)),
                  pl.BlockSpec((CONV_W - 1, db, tc), lambda ci: (0, 0, ci)),
                  pl.BlockSpec((db, tc), lambda ci: (0, ci)),
                  pl.BlockSpec((CONV_W, tc), lambda ci: (0, ci)),
                  vec(),
                  pl.BlockSpec((hpt, LRU_BLOCK, LRU_BLOCK), lambda ci: (ci, 0, 0)),
                  pl.BlockSpec((hpt, LRU_BLOCK, LRU_BLOCK), lambda ci: (ci, 0, 0)),
                  vec(), vec(), vec()],
        out_specs=[pl.BlockSpec((db, tc), lambda ci: (0, ci)), pl.BlockSpec((db, tc), lambda ci: (0, ci))],
        compiler_params=_cparams(("arbitrary",)),
        name="lru_step",
    )(proj_s, proj_s, conv_prev_t, h0, conv_w, conv_b.reshape(1, c), wa_bf, wx_bf,
      ba.reshape(1, c), bx.reshape(1, c), lam.reshape(1, c))


def _merge_kernel(ol_ref, oa_ref, g0_ref, g1_ref, wl_ref, wa_ref, o_ref):
    yl = jnp.dot(ol_ref[0], wl_ref[...], preferred_element_type=F32)
    ya = jnp.dot(oa_ref[0], wa_ref[...], preferred_element_type=F32)
    o_ref[0] = (jax.nn.sigmoid(g0_ref[0]) * yl + jax.nn.sigmoid(g1_ref[0]) * ya).astype(o_ref.dtype)


def branch_merge(o_lru, o_attn, proj, wl_bf, wa_bf):
    b, s, d = o_lru.shape
    tm = min(1024, s)
    tn = 512
    return pl.pallas_call(
        _merge_kernel,
        out_shape=jax.ShapeDtypeStruct((b, s, d), BF16),
        grid=(b, s // tm, d // tn),
        in_specs=[pl.BlockSpec((1, tm, d), lambda bb, i, j: (bb, i, 0)),
                  pl.BlockSpec((1, tm, d), lambda bb, i, j: (bb, i, 0)),
                  pl.BlockSpec((1, tm, tn), lambda bb, i, j: (bb, i, COL_GB // tn + j)),
                  pl.BlockSpec((1, tm, tn), lambda bb, i, j: (bb, i, (COL_GB + D_MODEL) // tn + j)),
                  pl.BlockSpec((d, tn), lambda bb, i, j: (0, j)),
                  pl.BlockSpec((d, tn), lambda bb, i, j: (0, j))],
        out_specs=pl.BlockSpec((1, tm, tn), lambda bb, i, j: (bb, i, j)),
        compiler_params=_cparams(("arbitrary", "arbitrary", "arbitrary")),
        name="branch_merge",
    )(o_lru, o_attn, proj, proj, wl_bf, wa_bf)


def _out_proj_kernel(m_ref, x_ref, gt_ref, w_ref, o_ref):
    o_ref[0] = x_ref[0] + gt_ref[0] * jnp.dot(m_ref[0], w_ref[...], preferred_element_type=F32)


def out_project(merged, x, gt, w_bf):
    b, s, d = x.shape
    tm = min(1024, s)
    tn = 512
    return pl.pallas_call(
        _out_proj_kernel,
        out_shape=jax.ShapeDtypeStruct((b, s, d), F32),
        grid=(b, s // tm, d // tn),
        in_specs=[pl.BlockSpec((1, tm, d), lambda bb, i, j: (bb, i, 0)),
                  pl.BlockSpec((1, tm, tn), lambda bb, i, j: (bb, i, j)),
                  _mod_spec(gt, tm, tn),
                  pl.BlockSpec((d, tn), lambda bb, i, j: (0, j))],
        out_specs=pl.BlockSpec((1, tm, tn), lambda bb, i, j: (bb, i, j)),
        compiler_params=_cparams(("arbitrary", "arbitrary", "arbitrary")),
        name="out_project",
    )(merged, x, gt, w_bf)


def _topk_rows(s, k, n):
    row = lax.broadcasted_iota(jnp.int32, s.shape, 0)
    slot = lax.broadcasted_iota(jnp.int32, (k, s.shape[1]), 0)
    vals = jnp.zeros((k, s.shape[1]), F32)
    idxs = jnp.zeros((k, s.shape[1]), jnp.int32)
    for r in range(k):
        m = jnp.max(s, axis=0, keepdims=True)
        am = jnp.min(jnp.where(s == m, row, n), axis=0, keepdims=True)
        vals = jnp.where(slot == r, m, vals)
        idxs = jnp.where(slot == r, am, idxs)
        s = jnp.where(row == am, -jnp.inf, s)
    return vals, idxs


def _peer_topk_kernel(q_ref, sk_ref, e_ref, g_ref):
    tt = q_ref.shape[0]
    kk = PEER_TOPK
    for hh in range(PEER_HEADS):
        sv, si = [], []
        for p in range(2):
            c0 = (hh * 2 + p) * PEER_HALF
            qc = q_ref[:, c0:c0 + PEER_HALF]
            s = lax.dot_general(sk_ref[hh, p], qc, (((1,), (1,)), ((), ())), precision=lax.Precision.HIGHEST,
                                preferred_element_type=F32)
            v, i = _topk_rows(s, kk, PEER_N_KEYS)
            sv.append(v)
            si.append(i)
        half = kk // 2
        cand = jnp.concatenate([sv[0][0:1] + sv[1]]
                               + [sv[0][a:a + 1] + sv[1][0:half] for a in range(1, half)]
                               + [sv[0][half:kk] + sv[1][0:1]], axis=0)
        cidx = jnp.concatenate([si[0][0:1] * PEER_N_KEYS + si[1]]
                               + [si[0][a:a + 1] * PEER_N_KEYS + si[1][0:half] for a in range(1, half)]
                               + [si[0][half:kk] * PEER_N_KEYS + si[1][0:1]], axis=0)
        ncand = cand.shape[0]
        row = lax.broadcasted_iota(jnp.int32, cand.shape, 0)
        slot = lax.broadcasted_iota(jnp.int32, (kk, tt), 0)
        fv = jnp.zeros((kk, tt), F32)
        eid = jnp.zeros((kk, tt), jnp.int32)
        for r in range(kk):
            m = jnp.max(cand, axis=0, keepdims=True)
            am = jnp.min(jnp.where(cand == m, row, ncand), axis=0, keepdims=True)
            pick = row == am
            fv = jnp.where(slot == r, m, fv)
            eid = jnp.where(slot == r, jnp.max(jnp.where(pick, cidx, -1), axis=0, keepdims=True), eid)
            cand = jnp.where(pick, -jnp.inf, cand)
        ex = jnp.exp(fv - fv[0:1])
        g_ref[hh * kk:(hh + 1) * kk, :] = ex / jnp.sum(ex, axis=0, keepdims=True)
        e_ref[hh * kk:(hh + 1) * kk, :] = eid


def peer_topk(qp, sub_keys):
    n = qp.shape[0]
    tt = 256 if n % 256 == 0 else n
    return pl.pallas_call(
        _peer_topk_kernel,
        out_shape=[jax.ShapeDtypeStruct((PEER_SEL, n), jnp.int32), jax.ShapeDtypeStruct((PEER_SEL, n), F32)],
        grid=(n // tt,),
        in_specs=[pl.BlockSpec((tt, qp.shape[1]), lambda i: (i, 0)),
                  pl.BlockSpec(sub_keys.shape, lambda i: (0, 0, 0, 0))],
        out_specs=[pl.BlockSpec((PEER_SEL, tt), lambda i: (0, i)), pl.BlockSpec((PEER_SEL, tt), lambda i: (0, i))],
        compiler_params=_cparams(("arbitrary",)),
        name="peer_topk",
    )(qp, sub_keys)


PEER_TOK_TILE = 8


def _peer_gather_kernel(e_ref, en_ref, x_ref, sh_ref, sc_ref, gt_ref, g2_ref, gf_ref, gate_ref, uv_hbm, o_ref,
                        buf0, buf1, sem, *, n):
    i = pl.program_id(0)
    nrow = PEER_TOK_TILE * PEER_SEL
    bufs = (buf0, buf1)

    def row_copy(idx_ref, r, dst):
        return pltpu.make_async_copy(uv_hbm.at[idx_ref[r]], bufs[dst].at[r], sem.at[dst])

    def tile_wait(which):
        pltpu.make_async_copy(uv_hbm.at[pl.ds(0, nrow)], bufs[which], sem.at[which]).wait()

    @pl.when(i == 0)
    def _():
        def first(r, c):
            row_copy(e_ref, r, 0).start()
            return c
        lax.fori_loop(0, nrow, first, 0, unroll=8)

    sub_id = lax.broadcasted_iota(jnp.int32, (ROW_SUB, LANES), 0)
    lane_id = lax.broadcasted_iota(jnp.int32, (ROW_SUB, LANES), 1)
    diag = (lane_id % ROW_SUB) == sub_id
    mat_r = lax.broadcasted_iota(jnp.int32, (LANES, LANES), 0)
    mat_c = lax.broadcasted_iota(jnp.int32, (LANES, LANES), 1)
    group_sum = (mat_r // ROW_SUB == mat_c // ROW_SUB).astype(F32)
    spread = (mat_r % VREG_SUBLANES == mat_c // ROW_SUB).astype(F32)

    def hi_lo(x):
        hi = x.astype(BF16)
        return jnp.concatenate([hi, (x - hi.astype(F32)).astype(BF16)], axis=0)

    def fold(x):
        return x[0:ROW_SUB] + x[ROW_SUB:2 * ROW_SUB]

    def tile(cur):
        nxt = 1 - cur
        buf = bufs[cur]
        tile_wait(cur)
        toks = range(PEER_TOK_TILE)

        def issue(t):
            for k in range(PEER_SEL):
                row_copy(en_ref, t * PEER_SEL + k, nxt).start(priority=k % 2)

        def rms(v, g):
            return v * lax.rsqrt(jnp.mean(v * v, axis=-1, keepdims=True) + EPS) * g

        h_all = rms(x_ref[...], g2_ref[...]) * (1.0 + sc_ref[0]) + sh_ref[0]

        zs, gx = [], []
        for t in toks:
            issue(t)
            h = jnp.concatenate([h_all[t:t + 1, r * LANES:(r + 1) * LANES] for r in range(ROW_SUB)], axis=0)
            ub = buf[pl.ds(t * PEER_SEL, PEER_SEL), pl.ds(0, ROW_SUB), :].reshape(PEER_SEL * ROW_SUB, LANES)
            y = fold(lax.dot_general(hi_lo(h), ub, (((1,), (1,)), ((), ())), preferred_element_type=F32))
            zs.append(jnp.concatenate(
                [jnp.sum(jnp.where(diag, y[:, r * LANES:(r + 1) * LANES], 0.0), axis=0, keepdims=True)
                 for r in range(ROW_SUB)], axis=0))
            gx.append(jnp.where(lane_id // VREG_SUBLANES == sub_id, gate_ref[t:t + 1, :], 0.0))
        act = jnp.dot(jnp.concatenate(zs, axis=0), group_sum, precision=lax.Precision.HIGHEST,
                      preferred_element_type=F32)
        gex = jnp.dot(jnp.concatenate(gx, axis=0), spread, precision=lax.Precision.HIGHEST,
                      preferred_element_type=F32)
        w_all = gex * _gelu(act)
        for t in toks:
            whl = hi_lo(w_all[t * ROW_SUB:(t + 1) * ROW_SUB]).astype(F32)
            wexp = jnp.concatenate(
                [jnp.concatenate([jnp.where(diag, whl[r:r + 1], 0.0),
                                  jnp.where(diag, whl[ROW_SUB + r:ROW_SUB + r + 1], 0.0)], axis=0)
                 for r in range(ROW_SUB)], axis=1).astype(BF16)
            vb = buf[pl.ds(t * PEER_SEL, PEER_SEL), pl.ds(ROW_SUB, ROW_SUB), :].reshape(PEER_SEL * ROW_SUB, LANES)
            ff = fold(jnp.dot(wexp, vb, preferred_element_type=F32))
            for s in range(ROW_SUB):
                o_ref[t:t + 1, s * LANES:(s + 1) * LANES] = ff[s:s + 1]
        o_ref[...] = rms(x_ref[...] + gt_ref[0] * o_ref[...], gf_ref[...])

        @pl.when(i == n - 1)
        def _():
            tile_wait(nxt)

    @pl.when(i % 2 == 0)
    def _():
        tile(0)

    @pl.when(i % 2 == 1)
    def _():
        tile(1)


def peer_gather(eidx, gates, x1, sh, sc, gt, norm_g, final_g, uv):
    b, s, d = x1.shape
    n = b * s
    tk = PEER_TOK_TILE
    nrow = tk * PEER_SEL
    nt = n // tk
    per_row = sh.shape[1] != 1
    assert s % tk == 0
    tiles_per_batch = s // tk
    if per_row:
        mod_spec = pl.BlockSpec((1, tk, d), lambda i: (0, i, 0))
    else:
        mod_spec = pl.BlockSpec((1, 1, d), lambda i: (i // tiles_per_batch, 0, 0))
    vec_spec = pl.BlockSpec((1, d), lambda i: (0, 0))
    out = pl.pallas_call(
        functools.partial(_peer_gather_kernel, n=nt),
        out_shape=jax.ShapeDtypeStruct((n, d), F32),
        grid=(nt,),
        in_specs=[pl.BlockSpec((nrow,), lambda i: (i,), memory_space=pltpu.SMEM),
                  pl.BlockSpec((nrow,), lambda i: (jnp.minimum(i + 1, nt - 1),), memory_space=pltpu.SMEM),
                  pl.BlockSpec((tk, d), lambda i: (i, 0)),
                  mod_spec, mod_spec, mod_spec, vec_spec, vec_spec,
                  pl.BlockSpec((tk, PEER_SEL), lambda i: (i, 0)),
                  pl.BlockSpec(memory_space=pl.ANY)],
        out_specs=pl.BlockSpec((tk, d), lambda i: (i, 0)),
        scratch_shapes=[pltpu.VMEM((nrow, 2 * ROW_SUB, LANES), BF16), pltpu.VMEM((nrow, 2 * ROW_SUB, LANES), BF16),
                        pltpu.SemaphoreType.DMA((2,))],
        compiler_params=_cparams(("arbitrary",)),
        name="peer_gather",
    )(eidx.reshape(-1), eidx.reshape(-1), x1.reshape(n, d), sh, sc, gt, norm_g.reshape(1, d), final_g.reshape(1, d),
      gates, uv)
    return out.reshape(b, s, d)


def _peer_block(x1, sh2, sc2, gt2, norm2_g, wq, sub_keys, uv, final_g, cos, sin):
    b, s, d = x1.shape
    n = b * s
    qp = norm_proj(x1, sh2, sc2, norm2_g, wq, cos, sin, rope_cols=0)
    qp2 = qp.reshape(n, -1)
    n_pad = -(-n // LANES) * LANES
    if n_pad != n:
        qp2 = jnp.pad(qp2, ((0, n_pad - n), (0, 0)))
    e_t, g_t = peer_topk(qp2, sub_keys)
    return peer_gather(e_t.T[:n], g_t.T[:n], x1, sh2, sc2, gt2, norm2_g, final_g, uv)


def kernel(x_prompt, x_sample, c_prompt, c_sample, cache_k, cache_v, state_lru, state_conv, page_table, norm1_g, w_ada, b_ada, w_in, conv_w, conv_b, lru_wa, lru_ba, lru_wx, lru_bx, lru_lam, w_br_lru, w_br_attn, w_out, norm2_g, peer_wq, peer_subkeys, peer_u, peer_v, final_g):
    assert w_ada.shape[0] == 1, "single layer"
    b, s, d = x_prompt.shape
    db = x_sample.shape[0]
    n_pages = page_table.shape[1]
    past_len = n_pages * PAGE_SIZE
    assert s % MOBA_BLOCK == 0 and past_len % MOBA_BLOCK == 0 and x_sample.shape[1] == 1

    wl_bf, wa_bf, wo_bf = w_br_lru[0].astype(BF16), w_br_attn[0].astype(BF16), w_out[0].astype(BF16)
    lwa_bf, lwx_bf = lru_wa[0].astype(BF16), lru_wx[0].astype(BF16)
    uv = jnp.concatenate([peer_u[0].reshape(-1, ROW_SUB, LANES), peer_v[0].reshape(-1, ROW_SUB, LANES)],
                         axis=1).astype(BF16)

    mod = ada_project(jnp.concatenate([c_prompt, c_sample], axis=0), w_ada[0], b_ada[0])
    mod_p = [m.reshape(b, 1, d) for m in jnp.split(mod[:b], N_MOD, axis=-1)]
    mod_s = [m.reshape(1, db, d) for m in jnp.split(mod[b:], N_MOD, axis=-1)]

    cos_p, sin_p = rope_tables(jnp.arange(s, dtype=jnp.int32))
    proj = norm_proj(x_prompt, mod_p[0], mod_p[1], norm1_g[0], w_in[0], cos_p, sin_p, rope_cols=D_ATTN + D_KV)
    kmean = moba_block_means(proj)
    o_attn = moba_prompt_attention(proj, kmean)
    o_lru, h_last_p = lru_prompt(proj, jnp.zeros((b, VREG_SUBLANES, D_LRU), F32), jnp.zeros((b, 1, D_LRU), F32),
                                 conv_w[0], conv_b[0], lwa_bf, lwx_bf, lru_ba[0], lru_bx[0], lru_lam[0])
    merged = branch_merge(o_lru, o_attn, proj, wl_bf, wa_bf)
    x1 = out_project(merged, x_prompt, mod_p[2], wo_bf)
    y_prompt = _peer_block(x1, mod_p[3], mod_p[4], mod_p[5], norm2_g[0], peer_wq[0], peer_subkeys[0], uv,
                           final_g, cos_p, sin_p)
    k_prompt = proj[:, :, COL_K:COL_K + D_KV].reshape(1, b, s, N_KV_HEADS, HEAD_DIM)
    v_prompt = proj[:, :, COL_V:COL_V + D_KV].reshape(1, b, s, N_KV_HEADS, HEAD_DIM)
    conv_prompt = proj[:, s - (CONV_W - 1):, COL_XL:COL_XL + D_LRU].reshape(1, b, CONV_W - 1, D_LRU)

    xs = x_sample.reshape(1, db, d)
    cos_s, sin_s = rope_tables(jnp.full((db,), past_len, jnp.int32))
    proj_s = norm_proj(xs, mod_s[0], mod_s[1], norm1_g[0], w_in[0], cos_s, sin_s, rope_cols=D_ATTN + D_KV)
    q_s = proj_s[0, :, COL_Q:COL_Q + D_ATTN].reshape(db, N_HEADS, HEAD_DIM)
    k_new = proj_s[0, :, COL_K:COL_K + D_KV].reshape(db, 1, D_KV)
    v_new = proj_s[0, :, COL_V:COL_V + D_KV].reshape(db, 1, D_KV)
    n_pool = cache_k.shape[1]
    cache_k3 = cache_k[0].reshape(n_pool, PAGE_SIZE * N_KV_HEADS, HEAD_DIM)
    cache_v3 = cache_v[0].reshape(n_pool, PAGE_SIZE * N_KV_HEADS, HEAD_DIM)
    kmean_s = sample_block_means(cache_k3, page_table)
    sel = sample_select(q_s, kmean_s)[:, :, :MOBA_TOPK].reshape(db, N_HEADS * MOBA_TOPK)
    o_attn_s = sample_attention(q_s, k_new, v_new, cache_k3, cache_v3, page_table, sel)
    o_lru_s, h_last_s = lru_step(proj_s, jnp.transpose(state_conv[0], (1, 0, 2)), state_lru[0],
                                 conv_w[0], conv_b[0], lwa_bf, lwx_bf, lru_ba[0], lru_bx[0], lru_lam[0])
    merged_s = branch_merge(o_lru_s.reshape(1, db, d), o_attn_s.astype(BF16).reshape(1, db, d), proj_s, wl_bf, wa_bf)
    x1_s = out_project(merged_s, xs, mod_s[2], wo_bf)
    y_sample = _peer_block(x1_s, mod_s[3], mod_s[4], mod_s[5], norm2_g[0], peer_wq[0], peer_subkeys[0], uv,
                           final_g, cos_s, sin_s)
    xl_s = proj_s[0, :, COL_XL:COL_XL + D_LRU]
    conv_sample = jnp.concatenate([state_conv[0][:, 1:], xl_s[:, None, :]], axis=1)[None]

    return (y_prompt, y_sample.reshape(db, 1, d), k_prompt, v_prompt,
            h_last_p.reshape(1, b, D_LRU), conv_prompt,
            k_new.reshape(1, db, 1, N_KV_HEADS, HEAD_DIM), v_new.reshape(1, db, 1, N_KV_HEADS, HEAD_DIM),
            h_last_s.reshape(1, db, D_LRU), conv_sample)
```

```python
import functools
import math

import jax
import jax.numpy as jnp
from jax import lax
from jax.experimental import pallas as pl
from jax.experimental.pallas import tpu as pltpu

D_MODEL = 2048
PAGE_SIZE = 128
N_HEADS = 16
N_KV_HEADS = 4
HEAD_DIM = 128
GROUP = N_HEADS // N_KV_HEADS
D_ATTN = N_HEADS * HEAD_DIM
D_KV = N_KV_HEADS * HEAD_DIM
ROPE_THETA = 10000.0
MOBA_BLOCK = 256
MOBA_TOPK = 3
D_LRU = D_MODEL
LRU_HEADS = 16
LRU_BLOCK = D_LRU // LRU_HEADS
CONV_W = 4
LRU_C = 8.0
PEER_HEADS = 8
PEER_N_KEYS = 128
PEER_HALF = 128
PEER_TOPK = 16
PEER_SEL = PEER_HEADS * PEER_TOPK
N_MOD = 6
EPS = 1e-6

COL_Q = 0
COL_K = D_ATTN
COL_V = D_ATTN + D_KV
COL_XL = D_ATTN + 2 * D_KV
COL_GL = COL_XL + D_LRU
COL_GB = COL_GL + D_LRU
IN_COLS = COL_GB + 2 * D_MODEL

VREG_SUBLANES = 8
LANES = 128
ROW_SUB = D_MODEL // LANES
NEG = -1e30
BF16 = jnp.bfloat16
F32 = jnp.float32
VMEM_LIMIT = 52 * 1024 * 1024


def _cparams(sem):
    return pltpu.CompilerParams(dimension_semantics=sem, vmem_limit_bytes=VMEM_LIMIT)


def _gelu(x):
    return 0.5 * x * (1.0 + lax.erf(x * (1.0 / math.sqrt(2.0))))


def _mod_spec(mod, tm, tn, tiled_cols=True):
    col = (lambda j: j) if tiled_cols else (lambda j: 0)
    if mod.shape[1] == 1:
        return pl.BlockSpec((1, 1, tn), lambda b, i, j: (b, 0, col(j)))
    return pl.BlockSpec((1, tm, tn), lambda b, i, j: (b, i, col(j)))


def _ada_kernel(c_ref, w_ref, b_ref, o_ref):
    o_ref[...] = jnp.dot(c_ref[...].astype(BF16), w_ref[...].astype(BF16),
                         preferred_element_type=F32) + b_ref[...]


def ada_project(c_all, w_ada, b_ada):
    m, d = c_all.shape
    n = w_ada.shape[1]
    tn = 1024
    return pl.pallas_call(
        _ada_kernel,
        out_shape=jax.ShapeDtypeStruct((m, n), F32),
        grid=(n // tn,),
        in_specs=[pl.BlockSpec((m, d), lambda j: (0, 0)),
                  pl.BlockSpec((d, tn), lambda j: (0, j)),
                  pl.BlockSpec((1, tn), lambda j: (0, j))],
        out_specs=pl.BlockSpec((m, tn), lambda j: (0, j)),
        compiler_params=_cparams(("arbitrary",)),
        name="ada_project",
    )(c_all, w_ada, b_ada.reshape(1, n))


def _norm_proj_kernel(x_ref, sh_ref, sc_ref, g_ref, w_ref, cos_ref, sin_ref, o_ref, hs_ref, *, rope_tiles):
    j = pl.program_id(2)

    @pl.when(j == 0)
    def _():
        x = x_ref[0]
        y = x * lax.rsqrt(jnp.mean(x * x, axis=-1, keepdims=True) + EPS) * g_ref[...]
        hs_ref[...] = (y * (1.0 + sc_ref[0]) + sh_ref[0]).astype(BF16)

    acc = jnp.dot(hs_ref[...], w_ref[...].astype(BF16), preferred_element_type=F32)

    if rope_tiles:
        @pl.when(j < rope_tiles)
        def _():
            cos = cos_ref[...]
            sin = sin_ref[...]
            parts = []
            for hh in range(acc.shape[1] // HEAD_DIM):
                a = acc[:, hh * HEAD_DIM:(hh + 1) * HEAD_DIM]
                parts.append(a * cos + pltpu.roll(a, HEAD_DIM // 2, 1) * sin)
            o_ref[0] = jnp.concatenate(parts, axis=1)

        @pl.when(j >= rope_tiles)
        def _():
            o_ref[0] = acc
    else:
        o_ref[0] = acc


def norm_proj(x, sh, sc, g, w, cos, sin, *, rope_cols):
    b, s, d = x.shape
    n = w.shape[1]
    tm = min(1024, s)
    tn = 512
    assert s % tm == 0 and n % tn == 0 and rope_cols % tn == 0
    return pl.pallas_call(
        functools.partial(_norm_proj_kernel, rope_tiles=rope_cols // tn),
        out_shape=jax.ShapeDtypeStruct((b, s, n), F32),
        grid=(b, s // tm, n // tn),
        in_specs=[pl.BlockSpec((1, tm, d), lambda bb, i, j: (bb, i, 0)),
                  _mod_spec(sh, tm, d, False), _mod_spec(sc, tm, d, False),
                  pl.BlockSpec((1, d), lambda bb, i, j: (0, 0)),
                  pl.BlockSpec((d, tn), lambda bb, i, j: (0, j)),
                  pl.BlockSpec((tm, HEAD_DIM), lambda bb, i, j: (i, 0)),
                  pl.BlockSpec((tm, HEAD_DIM), lambda bb, i, j: (i, 0))],
        out_specs=pl.BlockSpec((1, tm, tn), lambda bb, i, j: (bb, i, j)),
        scratch_shapes=[pltpu.VMEM((tm, d), BF16)],
        compiler_params=_cparams(("arbitrary", "arbitrary", "arbitrary")),
        name="norm_proj",
    )(x, sh, sc, g.reshape(1, d), w, cos, sin)


def rope_tables(pos):
    half = HEAD_DIM // 2
    inv = jnp.exp(-math.log(ROPE_THETA) * jnp.arange(half, dtype=F32) / half)
    ang = pos.astype(F32)[:, None] * inv[None, :]
    cos, sin = jnp.cos(ang), jnp.sin(ang)
    return jnp.concatenate([cos, cos], axis=1), jnp.concatenate([-sin, sin], axis=1)


def _kmean_kernel(k_ref, o_ref):
    o_ref[0, 0] = jnp.mean(k_ref[0], axis=0, keepdims=True)


def moba_block_means(proj):
    b, s, _ = proj.shape
    nb = s // MOBA_BLOCK
    out = pl.pallas_call(
        _kmean_kernel,
        out_shape=jax.ShapeDtypeStruct((b, nb, 1, D_KV), F32),
        grid=(b, nb),
        in_specs=[pl.BlockSpec((1, MOBA_BLOCK, D_KV), lambda bb, i: (bb, i, COL_K // D_KV))],
        out_specs=pl.BlockSpec((1, 1, 1, D_KV), lambda bb, i: (bb, i, 0, 0)),
        compiler_params=_cparams(("arbitrary", "arbitrary")),
        name="moba_block_means",
    )(proj)
    return out.reshape(b, nb, D_KV)


def _top3_bias(gate, own_blk, nb):
    blk = lax.broadcasted_iota(jnp.int32, gate.shape, 0)
    past = blk < own_blk
    work = jnp.where(past, gate, -jnp.inf)
    sel = jnp.zeros(gate.shape, jnp.bool_)
    for _ in range(min(MOBA_TOPK, nb)):
        m = jnp.max(work, axis=0, keepdims=True)
        idx = jnp.min(jnp.where(work == m, blk, nb), axis=0, keepdims=True)
        pick = blk == idx
        sel = jnp.logical_or(sel, jnp.logical_and(pick, past))
        work = jnp.where(pick, -jnp.inf, work)
    return jnp.where(sel, 0.0, NEG).astype(F32)


def _moba_kernel(q_ref, k_ref, v_ref, km_ref, o_ref, kb_ref, vt_ref, qa_ref, m_ref, l_ref, acc_ref, sa_ref, sb_ref):
    qi = pl.program_id(2)
    nb = km_ref.shape[1]
    tq = MOBA_BLOCK
    rows = GROUP * tq
    hd = HEAD_DIM
    assert nb <= hd

    @pl.when(qi == 0)
    def _():
        lane = lax.broadcasted_iota(jnp.int32, (MOBA_BLOCK, hd), 1)

        def cp(jb, c):
            st = pl.multiple_of(jb * MOBA_BLOCK, MOBA_BLOCK)
            kb_ref[pl.ds(st, MOBA_BLOCK), 0:hd] = k_ref[0, pl.ds(st, MOBA_BLOCK), :].astype(BF16)
            kb_ref[pl.ds(st, MOBA_BLOCK), hd:2 * hd] = jnp.where(lane == jb, 1.0, 0.0).astype(BF16)
            vt_ref[jb] = v_ref[0, pl.ds(st, MOBA_BLOCK), :].T.astype(BF16)
            return c
        lax.fori_loop(0, nb, cp, 0)

    q = q_ref[0]
    qcat = jnp.concatenate([q[:, g * hd:(g + 1) * hd] for g in range(GROUP)], axis=0)
    qt = qcat.T
    gate = jnp.dot(km_ref[0], qt, precision=lax.Precision.HIGHEST, preferred_element_type=F32)
    bias = _top3_bias(gate, qi, nb)
    qa_ref[0:hd] = (qt * (hd ** -0.5)).astype(BF16)
    qa_ref[hd:2 * hd] = jnp.concatenate([bias, jnp.zeros((hd - nb, rows), F32)], axis=0).astype(BF16)

    st = pl.multiple_of(qi * MOBA_BLOCK, MOBA_BLOCK)
    s = jnp.dot(kb_ref[pl.ds(st, MOBA_BLOCK), 0:hd], qa_ref[0:hd], preferred_element_type=F32)
    key_t = lax.broadcasted_iota(jnp.int32, (MOBA_BLOCK, rows), 0)
    q_t = lax.broadcasted_iota(jnp.int32, (MOBA_BLOCK, rows), 1) % tq
    s = jnp.where(key_t <= q_t, s, NEG)
    m0 = jnp.max(s, axis=0, keepdims=True)
    p = jnp.exp(s - m0)
    m_ref[...] = m0
    l_ref[...] = jnp.sum(p, axis=0, keepdims=True)
    acc_ref[...] = jnp.dot(vt_ref[qi], p.astype(BF16), preferred_element_type=F32)

    def scores(j):
        stj = pl.multiple_of(j * MOBA_BLOCK, MOBA_BLOCK)
        return jnp.dot(kb_ref[pl.ds(stj, MOBA_BLOCK), :], qa_ref[...], preferred_element_type=F32)

    def update(sj, j):
        m_old = m_ref[...]
        m_new = jnp.maximum(m_old, jnp.max(sj, axis=0, keepdims=True))
        alpha = jnp.exp(m_old - m_new)
        pj = jnp.exp(sj - m_new)
        l_ref[...] = alpha * l_ref[...] + jnp.sum(pj, axis=0, keepdims=True)
        acc_ref[...] = alpha * acc_ref[...] + jnp.dot(vt_ref[j], pj.astype(BF16), preferred_element_type=F32)
        m_ref[...] = m_new

    last = jnp.maximum(qi - 1, 0)
    sa_ref[...] = scores(0)

    def two(jj, c):
        j0 = 2 * jj
        sb_ref[...] = scores(j0 + 1)
        update(sa_ref[...], j0)
        sa_ref[...] = scores(jnp.minimum(j0 + 2, last))
        update(sb_ref[...], j0 + 1)
        return c

    lax.fori_loop(0, qi // 2, two, 0)

    @pl.when(qi % 2 == 1)
    def _():
        update(sa_ref[...], qi - 1)

    o = (acc_ref[...] / l_ref[...]).T
    for g in range(GROUP):
        o_ref[0, :, g * HEAD_DIM:(g + 1) * HEAD_DIM] = o[g * tq:(g + 1) * tq].astype(o_ref.dtype)


def moba_prompt_attention(proj, kmean):
    b, s, _ = proj.shape
    nb = s // MOBA_BLOCK
    rows = GROUP * MOBA_BLOCK
    gw = GROUP * HEAD_DIM
    return pl.pallas_call(
        _moba_kernel,
        out_shape=jax.ShapeDtypeStruct((b, s, D_ATTN), BF16),
        grid=(b, N_KV_HEADS, nb),
        in_specs=[pl.BlockSpec((1, MOBA_BLOCK, gw), lambda bb, hk, qi: (bb, qi, hk)),
                  pl.BlockSpec((1, s, HEAD_DIM), lambda bb, hk, qi: (bb, 0, COL_K // HEAD_DIM + hk)),
                  pl.BlockSpec((1, s, HEAD_DIM), lambda bb, hk, qi: (bb, 0, COL_V // HEAD_DIM + hk)),
                  pl.BlockSpec((1, nb, HEAD_DIM), lambda bb, hk, qi: (bb, 0, hk))],
        out_specs=pl.BlockSpec((1, MOBA_BLOCK, gw), lambda bb, hk, qi: (bb, qi, hk)),
        scratch_shapes=[pltpu.VMEM((s, 2 * HEAD_DIM), BF16),
                        pltpu.VMEM((nb, HEAD_DIM, MOBA_BLOCK), BF16),
                        pltpu.VMEM((2 * HEAD_DIM, rows), BF16),
                        pltpu.VMEM((1, rows), F32),
                        pltpu.VMEM((1, rows), F32),
                        pltpu.VMEM((HEAD_DIM, rows), F32),
                        pltpu.VMEM((MOBA_BLOCK, rows), F32),
                        pltpu.VMEM((MOBA_BLOCK, rows), F32)],
        compiler_params=_cparams(("arbitrary", "arbitrary", "arbitrary")),
        name="moba_prompt_attention",
    )(proj, proj, proj, kmean)


def _page_sum_kernel(pt_ref, *refs, pages_per_step):
    del pt_ref
    o_ref = refs[pages_per_step]
    sub = VREG_SUBLANES
    for r in range(0, pages_per_step, 2):
        grp = (jnp.sum(refs[r][0].reshape(-1, sub, HEAD_DIM), axis=0)
               + jnp.sum(refs[r + 1][0].reshape(-1, sub, HEAD_DIM), axis=0))
        o_ref[0, 0, r // 2] = (grp[0:N_KV_HEADS] + grp[N_KV_HEADS:sub]) * (1.0 / MOBA_BLOCK)


def sample_block_means(cache_k3, page_table):
    db, n_pages = page_table.shape
    pps = next(c for c in (16, 8, 2) if n_pages % c == 0)
    assert n_pages % pps == 0
    bps = pps // 2
    nbp = n_pages // 2
    prow = PAGE_SIZE * N_KV_HEADS

    def page_spec(r):
        return pl.BlockSpec((1, prow, HEAD_DIM), lambda bb, p, pt: (pt[bb, p * pps + r], 0, 0))

    out = pl.pallas_call(
        functools.partial(_page_sum_kernel, pages_per_step=pps),
        out_shape=jax.ShapeDtypeStruct((db, n_pages // pps, bps, N_KV_HEADS, HEAD_DIM), F32),
        grid_spec=pltpu.PrefetchScalarGridSpec(
            num_scalar_prefetch=1,
            grid=(db, n_pages // pps),
            in_specs=[page_spec(r) for r in range(pps)],
            out_specs=pl.BlockSpec((1, 1, bps, N_KV_HEADS, HEAD_DIM), lambda bb, p, pt: (bb, p, 0, 0, 0)),
        ),
        compiler_params=_cparams(("arbitrary", "arbitrary")),
        name="sample_block_means",
    )(page_table, *([cache_k3] * pps))
    return out.reshape(db, nbp, D_KV)


def _sample_select_kernel(q_ref, km_ref, o_ref):
    nbp = km_ref.shape[1]
    q = q_ref[0]
    head_kv = lax.broadcasted_iota(jnp.int32, (N_HEADS, nbp), 0) // GROUP
    gate = jnp.zeros((N_HEADS, nbp), F32)
    for hk in range(N_KV_HEADS):
        km = km_ref[0, :, hk * HEAD_DIM:(hk + 1) * HEAD_DIM]
        gk = lax.dot_general(q, km, (((1,), (1,)), ((), ())), precision=lax.Precision.HIGHEST,
                             preferred_element_type=F32)
        gate = jnp.where(head_kv == hk, gk, gate)
    blk = lax.broadcasted_iota(jnp.int32, gate.shape, 1)
    lane = lax.broadcasted_iota(jnp.int32, (N_HEADS, LANES), 1)
    out = jnp.zeros((N_HEADS, LANES), jnp.int32)
    work = gate
    for r in range(MOBA_TOPK):
        m = jnp.max(work, axis=1, keepdims=True)
        idx = jnp.min(jnp.where(work == m, blk, nbp), axis=1, keepdims=True)
        out = jnp.where(lane == r, idx, out)
        work = jnp.where(blk == idx, -jnp.inf, work)
    o_ref[0] = out


def sample_select(q_s, kmean_s):
    db = q_s.shape[0]
    nbp = kmean_s.shape[1]
    return pl.pallas_call(
        _sample_select_kernel,
        out_shape=jax.ShapeDtypeStruct((db, N_HEADS, LANES), jnp.int32),
        grid=(db,),
        in_specs=[pl.BlockSpec((1, N_HEADS, HEAD_DIM), lambda bb: (bb, 0, 0)),
                  pl.BlockSpec((1, nbp, D_KV), lambda bb: (bb, 0, 0))],
        out_specs=pl.BlockSpec((1, N_HEADS, LANES), lambda bb: (bb, 0, 0)),
        compiler_params=_cparams(("arbitrary",)),
        name="sample_select",
    )(q_s, kmean_s)


def _sample_attn_kernel(pt_ref, sel_ref, q_ref, *refs, n_pages):
    del pt_ref, sel_ref
    kp_refs = refs[:n_pages]
    vp_refs = refs[n_pages:2 * n_pages]
    kn_ref, vn_ref, o_ref, s_ref = refs[2 * n_pages:]
    h = pl.program_id(1)
    hk = h // GROUP
    prow = PAGE_SIZE * N_KV_HEADS
    qh = q_ref[0, pl.ds(h, 1), :] * (HEAD_DIM ** -0.5)
    qb = jnp.broadcast_to(qh, (HEAD_DIM, HEAD_DIM)).astype(BF16)

    def pick_kv(row):
        out = jnp.zeros((1, HEAD_DIM), F32)
        for kk in range(N_KV_HEADS):
            out = out + jnp.where(hk == kk, row[:, kk * HEAD_DIM:(kk + 1) * HEAD_DIM], 0.0)
        return out

    s_self = jnp.sum(pick_kv(kn_ref[0]) * qh, axis=1, keepdims=True)
    row_kv = lax.broadcasted_iota(jnp.int32, (prow, HEAD_DIM), 0) % N_KV_HEADS
    m = jnp.broadcast_to(s_self, (1, HEAD_DIM))
    for j in range(n_pages):
        sj = lax.dot_general(kp_refs[j][0].astype(BF16), qb, (((1,), (1,)), ((), ())), preferred_element_type=F32)
        sj = jnp.where(row_kv == hk, sj, NEG)
        s_ref[j] = sj
        m = jnp.maximum(m, jnp.max(sj, axis=0, keepdims=True))
    p_self = jnp.exp(s_self - m)
    l = p_self
    acc = p_self * pick_kv(vn_ref[0])
    for j in range(n_pages):
        pj = jnp.exp(s_ref[j] - m)
        l = l + jnp.sum(pj, axis=0, keepdims=True)
        acc = acc + jnp.sum(pj * vp_refs[j][0], axis=0, keepdims=True)
    o_ref[0, 0] = acc / l


def sample_attention(q_s, k_new, v_new, cache_k3, cache_v3, page_table, sel):
    db = q_s.shape[0]
    pages_per_blk = MOBA_BLOCK // PAGE_SIZE
    n_pages = MOBA_TOPK * pages_per_blk
    prow = PAGE_SIZE * N_KV_HEADS

    def page_spec(j):
        def idx(bb, h, pt, sl):
            return (pt[bb, sl[bb, h * MOBA_TOPK + j // pages_per_blk] * pages_per_blk + j % pages_per_blk], 0, 0)
        return pl.BlockSpec((1, prow, HEAD_DIM), idx)

    out = pl.pallas_call(
        functools.partial(_sample_attn_kernel, n_pages=n_pages),
        out_shape=jax.ShapeDtypeStruct((db, N_HEADS, 1, HEAD_DIM), F32),
        grid_spec=pltpu.PrefetchScalarGridSpec(
            num_scalar_prefetch=2,
            grid=(db, N_HEADS),
            in_specs=([pl.BlockSpec((1, N_HEADS, HEAD_DIM), lambda bb, h, pt, sl: (bb, 0, 0))]
                      + [page_spec(j) for j in range(n_pages)] + [page_spec(j) for j in range(n_pages)]
                      + [pl.BlockSpec((1, 1, D_KV), lambda bb, h, pt, sl: (bb, 0, 0)),
                         pl.BlockSpec((1, 1, D_KV), lambda bb, h, pt, sl: (bb, 0, 0))]),
            out_specs=pl.BlockSpec((1, 1, 1, HEAD_DIM), lambda bb, h, pt, sl: (bb, h, 0, 0)),
            scratch_shapes=[pltpu.VMEM((n_pages, prow, HEAD_DIM), F32)],
        ),
        compiler_params=_cparams(("arbitrary", "arbitrary")),
        name="sample_attention",
    )(page_table, sel, q_s, *([cache_k3] * n_pages), *([cache_v3] * n_pages), k_new, v_new)
    return out.reshape(db, D_ATTN)


def _lru_gates(xc, wa_ref, wx_ref, ba, bx, lam):
    nh = xc.shape[1] // LRU_BLOCK
    xb = xc.astype(BF16)
    ra, rx = [], []
    for hh in range(nh):
        xs = xb[:, hh * LRU_BLOCK:(hh + 1) * LRU_BLOCK]
        ra.append(jnp.dot(xs, wa_ref[hh], preferred_element_type=F32))
        rx.append(jnp.dot(xs, wx_ref[hh], preferred_element_type=F32))
    r = jax.nn.sigmoid(jnp.concatenate(ra, axis=1) + ba)
    gi = jax.nn.sigmoid(jnp.concatenate(rx, axis=1) + bx)
    log_a = (-LRU_C * r) * jax.nn.softplus(-lam)
    a = jnp.exp(log_a)
    bxs = jnp.sqrt(1.0 - jnp.exp(2.0 * log_a)) * (gi * xc)
    return a, bxs


def _lru_kernel(xl_ref, gl_ref, cprev_ref, h0_ref, cw_ref, cb_ref, wa_ref, wx_ref, ba_ref, bx_ref, lam_ref,
                o_ref, hl_ref, xprev_ref, hc_ref, a_ref, b_ref):
    ti = pl.program_id(2)
    tt = xl_ref.shape[1]
    sub = VREG_SUBLANES

    @pl.when(ti == 0)
    def _():
        xprev_ref[...] = cprev_ref[0]
        hc_ref[...] = jnp.broadcast_to(h0_ref[0], hc_ref.shape)

    x = xl_ref[0]
    cw = cw_ref[...]
    xc = cb_ref[...] + x * cw[CONV_W - 1:CONV_W]
    first = jnp.concatenate([xprev_ref[...], x[0:sub]], axis=0)
    xc_first = cb_ref[...] + x[0:sub] * cw[CONV_W - 1:CONV_W]
    for d in range(1, CONV_W):
        wj = cw[CONV_W - 1 - d:CONV_W - d]
        xc = xc + pltpu.roll(x, d, 0) * wj
        xc_first = xc_first + pltpu.roll(first, d, 0)[sub:2 * sub] * wj
    xc = jnp.concatenate([xc_first, xc[sub:]], axis=0)
    xprev_ref[...] = x[tt - sub:tt]

    a, bxs = _lru_gates(xc, wa_ref, wx_ref, ba_ref[...], bx_ref[...], lam_ref[...])

    rowm = lax.broadcasted_iota(jnp.int32, a.shape, 0) % sub
    for d in (1, 2, 4):
        ok = rowm >= d
        a_sh = pltpu.roll(a, d, 0)
        b_sh = pltpu.roll(bxs, d, 0)
        bxs = jnp.where(ok, a * b_sh + bxs, bxs)
        a = jnp.where(ok, a * a_sh, a)
    a_ref[...] = a
    b_ref[...] = bxs

    def grp(gidx, hprev):
        st = pl.multiple_of(gidx * sub, sub)
        hg = a_ref[pl.ds(st, sub), :] * hprev + b_ref[pl.ds(st, sub), :]
        b_ref[pl.ds(st, sub), :] = hg
        return jnp.broadcast_to(hg[sub - 1:sub], hprev.shape)

    hlast = lax.fori_loop(0, tt // sub, grp, hc_ref[...])
    hc_ref[...] = hlast
    o_ref[0] = (b_ref[...] * _gelu(gl_ref[0])).astype(o_ref.dtype)

    @pl.when(ti == pl.num_programs(2) - 1)
    def _():
        hl_ref[0] = hlast[0:1]


def lru_prompt(proj, conv_prev8, h0, conv_w, conv_b, wa_bf, wx_bf, ba, bx, lam):
    b, s, _ = proj.shape
    c = D_LRU
    tc = 512
    tt = min(512, s)
    assert s % tt == 0
    hpt = tc // LRU_BLOCK
    vec = lambda: pl.BlockSpec((1, tc), lambda bb, ci, ti: (0, ci))
    return pl.pallas_call(
        _lru_kernel,
        out_shape=[jax.ShapeDtypeStruct((b, s, c), BF16), jax.ShapeDtypeStruct((b, 1, c), F32)],
        grid=(b, c // tc, s // tt),
        in_specs=[pl.BlockSpec((1, tt, tc), lambda bb, ci, ti: (bb, ti, COL_XL // tc + ci)),
                  pl.BlockSpec((1, tt, tc), lambda bb, ci, ti: (bb, ti, COL_GL // tc + ci)),
                  pl.BlockSpec((1, VREG_SUBLANES, tc), lambda bb, ci, ti: (bb, 0, ci)),
                  pl.BlockSpec((1, 1, tc), lambda bb, ci, ti: (bb, 0, ci)),
                  pl.BlockSpec((CONV_W, tc), lambda bb, ci, ti: (0, ci)),
                  vec(),
                  pl.BlockSpec((hpt, LRU_BLOCK, LRU_BLOCK), lambda bb, ci, ti: (ci, 0, 0)),
                  pl.BlockSpec((hpt, LRU_BLOCK, LRU_BLOCK), lambda bb, ci, ti: (ci, 0, 0)),
                  vec(), vec(), vec()],
        out_specs=[pl.BlockSpec((1, tt, tc), lambda bb, ci, ti: (bb, ti, ci)),
                   pl.BlockSpec((1, 1, tc), lambda bb, ci, ti: (bb, 0, ci))],
        scratch_shapes=[pltpu.VMEM((VREG_SUBLANES, tc), F32), pltpu.VMEM((VREG_SUBLANES, tc), F32),
                        pltpu.VMEM((tt, tc), F32), pltpu.VMEM((tt, tc), F32)],
        compiler_params=_cparams(("arbitrary", "arbitrary", "arbitrary")),
        name="lru_prompt",
    )(proj, proj, conv_prev8, h0, conv_w, conv_b.reshape(1, c), wa_bf, wx_bf,
      ba.reshape(1, c), bx.reshape(1, c), lam.reshape(1, c))


def _lru_step_kernel(xl_ref, gl_ref, cprev_ref, h0_ref, cw_ref, cb_ref, wa_ref, wx_ref, ba_ref, bx_ref, lam_ref,
                     o_ref, hl_ref):
    x = xl_ref[0]
    cw = cw_ref[...]
    xc = cb_ref[...] + x * cw[CONV_W - 1:CONV_W]
    for j in range(CONV_W - 1):
        xc = xc + cprev_ref[j] * cw[j:j + 1]
    a, bxs = _lru_gates(xc, wa_ref, wx_ref, ba_ref[...], bx_ref[...], lam_ref[...])
    h = a * h0_ref[...] + bxs
    hl_ref[...] = h
    o_ref[...] = (h * _gelu(gl_ref[0])).astype(o_ref.dtype)


def lru_step(proj_s, conv_prev_t, h0, conv_w, conv_b, wa_bf, wx_bf, ba, bx, lam):
    _, db, _ = proj_s.shape
    c = D_LRU
    tc = 1024
    hpt = tc // LRU_BLOCK
    vec = lambda: pl.BlockSpec((1, tc), lambda ci: (0, ci))
    return pl.pallas_call(
        _lru_step_kernel,
        out_shape=[jax.ShapeDtypeStruct((db, c), BF16), jax.ShapeDtypeStruct((db, c), F32)],
        grid=(c // tc,),
        in_specs=[pl.BlockSpec((1, db, tc), lambda ci: (0, 0, COL_XL // tc + ci)),
                  pl.BlockSpec((1, db, tc), lambda ci: (0, 0, COL_GL // tc + ci)),
                  pl.BlockSpec((CONV_W - 1, db, tc), lambda ci: (0, 0, ci)),
                  pl.BlockSpec((db, tc), lambda ci: (0, ci)),
                  pl.BlockSpec((CONV_W, tc), lambda ci: (0, ci)),
                  vec(),
                  pl.BlockSpec((hpt, LRU_BLOCK, LRU_BLOCK), lambda ci: (ci, 0, 0)),
                  pl.BlockSpec((hpt, LRU_BLOCK, LRU_BLOCK), lambda ci: (ci, 0, 0)),
                  vec(), vec(), vec()],
        out_specs=[pl.BlockSpec((db, tc), lambda ci: (0, ci)), pl.BlockSpec((db, tc), lambda ci: (0, ci))],
        compiler_params=_cparams(("arbitrary",)),
        name="lru_step",
    )(proj_s, proj_s, conv_prev_t, h0, conv_w, conv_b.reshape(1, c), wa_bf, wx_bf,
      ba.reshape(1, c), bx.reshape(1, c), lam.reshape(1, c))


def _merge_kernel(ol_ref, oa_ref, g0_ref, g1_ref, wl_ref, wa_ref, o_ref):
    yl = jnp.dot(ol_ref[0], wl_ref[...], preferred_element_type=F32)
    ya = jnp.dot(oa_ref[0], wa_ref[...], preferred_element_type=F32)
    o_ref[0] = (jax.nn.sigmoid(g0_ref[0]) * yl + jax.nn.sigmoid(g1_ref[0]) * ya).astype(o_ref.dtype)


def branch_merge(o_lru, o_attn, proj, wl_bf, wa_bf):
    b, s, d = o_lru.shape
    tm = min(1024, s)
    tn = 512
    return pl.pallas_call(
        _merge_kernel,
        out_shape=jax.ShapeDtypeStruct((b, s, d), BF16),
        grid=(b, s // tm, d // tn),
        in_specs=[pl.BlockSpec((1, tm, d), lambda bb, i, j: (bb, i, 0)),
                  pl.BlockSpec((1, tm, d), lambda bb, i, j: (bb, i, 0)),
                  pl.BlockSpec((1, tm, tn), lambda bb, i, j: (bb, i, COL_GB // tn + j)),
                  pl.BlockSpec((1, tm, tn), lambda bb, i, j: (bb, i, (COL_GB + D_MODEL) // tn + j)),
                  pl.BlockSpec((d, tn), lambda bb, i, j: (0, j)),
                  pl.BlockSpec((d, tn), lambda bb, i, j: (0, j))],
        out_specs=pl.BlockSpec((1, tm, tn), lambda bb, i, j: (bb, i, j)),
        compiler_params=_cparams(("arbitrary", "arbitrary", "arbitrary")),
        name="branch_merge",
    )(o_lru, o_attn, proj, proj, wl_bf, wa_bf)


def _out_proj_kernel(m_ref, x_ref, gt_ref, w_ref, o_ref):
    o_ref[0] = x_ref[0] + gt_ref[0] * jnp.dot(m_ref[0], w_ref[...], preferred_element_type=F32)


def out_project(merged, x, gt, w_bf):
    b, s, d = x.shape
    tm = min(1024, s)
    tn = 512
    return pl.pallas_call(
        _out_proj_kernel,
        out_shape=jax.ShapeDtypeStruct((b, s, d), F32),
        grid=(b, s // tm, d // tn),
        in_specs=[pl.BlockSpec((1, tm, d), lambda bb, i, j: (bb, i, 0)),
                  pl.BlockSpec((1, tm, tn), lambda bb, i, j: (bb, i, j)),
                  _mod_spec(gt, tm, tn),
                  pl.BlockSpec((d, tn), lambda bb, i, j: (0, j))],
        out_specs=pl.BlockSpec((1, tm, tn), lambda bb, i, j: (bb, i, j)),
        compiler_params=_cparams(("arbitrary", "arbitrary", "arbitrary")),
        name="out_project",
    )(merged, x, gt, w_bf)


def _topk_rows(s, k, n):
    row = lax.broadcasted_iota(jnp.int32, s.shape, 0)
    slot = lax.broadcasted_iota(jnp.int32, (k, s.shape[1]), 0)
    vals = jnp.zeros((k, s.shape[1]), F32)
    idxs = jnp.zeros((k, s.shape[1]), jnp.int32)
    for r in range(k):
        m = jnp.max(s, axis=0, keepdims=True)
        am = jnp.min(jnp.where(s == m, row, n), axis=0, keepdims=True)
        vals = jnp.where(slot == r, m, vals)
        idxs = jnp.where(slot == r, am, idxs)
        s = jnp.where(row == am, -jnp.inf, s)
    return vals, idxs


def _peer_topk_kernel(q_ref, sk_ref, e_ref, g_ref):
    tt = q_ref.shape[0]
    kk = PEER_TOPK
    for hh in range(PEER_HEADS):
        sv, si = [], []
        for p in range(2):
            c0 = (hh * 2 + p) * PEER_HALF
            qc = q_ref[:, c0:c0 + PEER_HALF]
            s = lax.dot_general(sk_ref[hh, p], qc, (((1,), (1,)), ((), ())), precision=lax.Precision.HIGHEST,
                                preferred_element_type=F32)
            v, i = _topk_rows(s, kk, PEER_N_KEYS)
            sv.append(v)
            si.append(i)
        half = kk // 2
        cand = jnp.concatenate([sv[0][0:1] + sv[1]]
                               + [sv[0][a:a + 1] + sv[1][0:half] for a in range(1, half)]
                               + [sv[0][half:kk] + sv[1][0:1]], axis=0)
        cidx = jnp.concatenate([si[0][0:1] * PEER_N_KEYS + si[1]]
                               + [si[0][a:a + 1] * PEER_N_KEYS + si[1][0:half] for a in range(1, half)]
                               + [si[0][half:kk] * PEER_N_KEYS + si[1][0:1]], axis=0)
        ncand = cand.shape[0]
        row = lax.broadcasted_iota(jnp.int32, cand.shape, 0)
        slot = lax.broadcasted_iota(jnp.int32, (kk, tt), 0)
        fv = jnp.zeros((kk, tt), F32)
        eid = jnp.zeros((kk, tt), jnp.int32)
        for r in range(kk):
            m = jnp.max(cand, axis=0, keepdims=True)
            am = jnp.min(jnp.where(cand == m, row, ncand), axis=0, keepdims=True)
            pick = row == am
            fv = jnp.where(slot == r, m, fv)
            eid = jnp.where(slot == r, jnp.max(jnp.where(pick, cidx, -1), axis=0, keepdims=True), eid)
            cand = jnp.where(pick, -jnp.inf, cand)
        ex = jnp.exp(fv - fv[0:1])
        g_ref[hh * kk:(hh + 1) * kk, :] = ex / jnp.sum(ex, axis=0, keepdims=True)
        e_ref[hh * kk:(hh + 1) * kk, :] = eid


def peer_topk(qp, sub_keys):
    n = qp.shape[0]
    tt = 256 if n % 256 == 0 else n
    return pl.pallas_call(
        _peer_topk_kernel,
        out_shape=[jax.ShapeDtypeStruct((PEER_SEL, n), jnp.int32), jax.ShapeDtypeStruct((PEER_SEL, n), F32)],
        grid=(n // tt,),
        in_specs=[pl.BlockSpec((tt, qp.shape[1]), lambda i: (i, 0)),
                  pl.BlockSpec(sub_keys.shape, lambda i: (0, 0, 0, 0))],
        out_specs=[pl.BlockSpec((PEER_SEL, tt), lambda i: (0, i)), pl.BlockSpec((PEER_SEL, tt), lambda i: (0, i))],
        compiler_params=_cparams(("arbitrary",)),
        name="peer_topk",
    )(qp, sub_keys)


PEER_TOK_TILE = 8


def _peer_gather_kernel(e_ref, e1_ref, e2_ref, x_ref, sh_ref, sc_ref, gt_ref, g2_ref, gf_ref, gate_ref, uv_hbm, o_ref,
                        buf0, buf1, buf2, sem, *, n):
    i = pl.program_id(0)
    nrow = PEER_TOK_TILE * PEER_SEL
    bufs = (buf0, buf1, buf2)
    nbuf = len(bufs)

    def row_copy(idx_ref, r, dst):
        return pltpu.make_async_copy(uv_hbm.at[idx_ref[r]], bufs[dst].at[r], sem.at[dst])

    def tile_wait(which):
        pltpu.make_async_copy(uv_hbm.at[pl.ds(0, nrow)], bufs[which], sem.at[which]).wait()

    @pl.when(i == 0)
    def _():
        def first(r, c):
            row_copy(e_ref, r, 0).start()
            row_copy(e1_ref, r, 1).start()
            return c
        lax.fori_loop(0, nrow, first, 0, unroll=8)

    sub_id = lax.broadcasted_iota(jnp.int32, (ROW_SUB, LANES), 0)
    lane_id = lax.broadcasted_iota(jnp.int32, (ROW_SUB, LANES), 1)
    diag = (lane_id % ROW_SUB) == sub_id
    mat_r = lax.broadcasted_iota(jnp.int32, (LANES, LANES), 0)
    mat_c = lax.broadcasted_iota(jnp.int32, (LANES, LANES), 1)
    group_sum = (mat_r // ROW_SUB == mat_c // ROW_SUB).astype(F32)
    spread = (mat_r % VREG_SUBLANES == mat_c // ROW_SUB).astype(F32)

    def hi_lo(x):
        hi = x.astype(BF16)
        return jnp.concatenate([hi, (x - hi.astype(F32)).astype(BF16)], axis=0)

    def fold(x):
        return x[0:ROW_SUB] + x[ROW_SUB:2 * ROW_SUB]

    def tile(cur):
        ahead = (cur + 2) % nbuf
        buf = bufs[cur]
        tile_wait(cur)
        toks = range(PEER_TOK_TILE)

        def issue(t):
            for k in range(PEER_SEL):
                row_copy(e2_ref, t * PEER_SEL + k, ahead).start(priority=k % 2)

        def rms(v, g):
            return v * lax.rsqrt(jnp.mean(v * v, axis=-1, keepdims=True) + EPS) * g

        h_all = rms(x_ref[...], g2_ref[...]) * (1.0 + sc_ref[0]) + sh_ref[0]

        zs, gx = [], []
        for t in toks:
            issue(t)
            h = jnp.concatenate([h_all[t:t + 1, r * LANES:(r + 1) * LANES] for r in range(ROW_SUB)], axis=0)
            ub = buf[pl.ds(t * PEER_SEL, PEER_SEL), pl.ds(0, ROW_SUB), :].reshape(PEER_SEL * ROW_SUB, LANES)
            y = fold(lax.dot_general(hi_lo(h), ub, (((1,), (1,)), ((), ())), preferred_element_type=F32))
            zs.append(jnp.concatenate(
                [jnp.sum(jnp.where(diag, y[:, r * LANES:(r + 1) * LANES], 0.0), axis=0, keepdims=True)
                 for r in range(ROW_SUB)], axis=0))
            gx.append(jnp.where(lane_id // VREG_SUBLANES == sub_id, gate_ref[t:t + 1, :], 0.0))
        act = jnp.dot(jnp.concatenate(zs, axis=0), group_sum, precision=lax.Precision.HIGHEST,
                      preferred_element_type=F32)
        gex = jnp.dot(jnp.concatenate(gx, axis=0), spread, precision=lax.Precision.HIGHEST,
                      preferred_element_type=F32)
        w_all = gex * _gelu(act)
        for t in toks:
            whl = hi_lo(w_all[t * ROW_SUB:(t + 1) * ROW_SUB]).astype(F32)
            wexp = jnp.concatenate(
                [jnp.concatenate([jnp.where(diag, whl[r:r + 1], 0.0),
                                  jnp.where(diag, whl[ROW_SUB + r:ROW_SUB + r + 1], 0.0)], axis=0)
                 for r in range(ROW_SUB)], axis=1).astype(BF16)
            vb = buf[pl.ds(t * PEER_SEL, PEER_SEL), pl.ds(ROW_SUB, ROW_SUB), :].reshape(PEER_SEL * ROW_SUB, LANES)
            ff = fold(jnp.dot(wexp, vb, preferred_element_type=F32))
            for s in range(ROW_SUB):
                o_ref[t:t + 1, s * LANES:(s + 1) * LANES] = ff[s:s + 1]
        o_ref[...] = rms(x_ref[...] + gt_ref[0] * o_ref[...], gf_ref[...])

        @pl.when(i == n - 1)
        def _():
            tile_wait((cur + 1) % nbuf)
            tile_wait(ahead)

    for c in range(nbuf):
        @pl.when(i % nbuf == c)
        def _(c=c):
            tile(c)


def peer_gather(eidx, gates, x1, sh, sc, gt, norm_g, final_g, uv):
    b, s, d = x1.shape
    n = b * s
    tk = PEER_TOK_TILE
    nrow = tk * PEER_SEL
    nt = n // tk
    per_row = sh.shape[1] != 1
    assert s % tk == 0
    tiles_per_batch = s // tk
    if per_row:
        mod_spec = pl.BlockSpec((1, tk, d), lambda i: (0, i, 0))
    else:
        mod_spec = pl.BlockSpec((1, 1, d), lambda i: (i // tiles_per_batch, 0, 0))
    vec_spec = pl.BlockSpec((1, d), lambda i: (0, 0))
    out = pl.pallas_call(
        functools.partial(_peer_gather_kernel, n=nt),
        out_shape=jax.ShapeDtypeStruct((n, d), F32),
        grid=(nt,),
        in_specs=[pl.BlockSpec((nrow,), lambda i: (i,), memory_space=pltpu.SMEM),
                  pl.BlockSpec((nrow,), lambda i: (jnp.minimum(i + 1, nt - 1),), memory_space=pltpu.SMEM),
                  pl.BlockSpec((nrow,), lambda i: (jnp.minimum(i + 2, nt - 1),), memory_space=pltpu.SMEM),
                  pl.BlockSpec((tk, d), lambda i: (i, 0)),
                  mod_spec, mod_spec, mod_spec, vec_spec, vec_spec,
                  pl.BlockSpec((tk, PEER_SEL), lambda i: (i, 0)),
                  pl.BlockSpec(memory_space=pl.ANY)],
        out_specs=pl.BlockSpec((tk, d), lambda i: (i, 0)),
        scratch_shapes=[pltpu.VMEM((nrow, 2 * ROW_SUB, LANES), BF16)] * 3 + [pltpu.SemaphoreType.DMA((3,))],
        compiler_params=_cparams(("arbitrary",)),
        name="peer_gather",
    )(eidx.reshape(-1), eidx.reshape(-1), eidx.reshape(-1), x1.reshape(n, d), sh, sc, gt, norm_g.reshape(1, d),
      final_g.reshape(1, d), gates, uv)
    return out.reshape(b, s, d)


def _peer_block(x1, sh2, sc2, gt2, norm2_g, wq, sub_keys, uv, final_g, cos, sin):
    b, s, d = x1.shape
    n = b * s
    qp = norm_proj(x1, sh2, sc2, norm2_g, wq, cos, sin, rope_cols=0)
    qp2 = qp.reshape(n, -1)
    n_pad = -(-n // LANES) * LANES
    if n_pad != n:
        qp2 = jnp.pad(qp2, ((0, n_pad - n), (0, 0)))
    e_t, g_t = peer_topk(qp2, sub_keys)
    return peer_gather(e_t.T[:n], g_t.T[:n], x1, sh2, sc2, gt2, norm2_g, final_g, uv)


def kernel(x_prompt, x_sample, c_prompt, c_sample, cache_k, cache_v, state_lru, state_conv, page_table, norm1_g, w_ada, b_ada, w_in, conv_w, conv_b, lru_wa, lru_ba, lru_wx, lru_bx, lru_lam, w_br_lru, w_br_attn, w_out, norm2_g, peer_wq, peer_subkeys, peer_u, peer_v, final_g):
    assert w_ada.shape[0] == 1, "single layer"
    b, s, d = x_prompt.shape
    db = x_sample.shape[0]
    n_pages = page_table.shape[1]
    past_len = n_pages * PAGE_SIZE
    assert s % MOBA_BLOCK == 0 and past_len % MOBA_BLOCK == 0 and x_sample.shape[1] == 1

    wl_bf, wa_bf, wo_bf = w_br_lru[0].astype(BF16), w_br_attn[0].astype(BF16), w_out[0].astype(BF16)
    lwa_bf, lwx_bf = lru_wa[0].astype(BF16), lru_wx[0].astype(BF16)
    uv = jnp.concatenate([peer_u[0].reshape(-1, ROW_SUB, LANES), peer_v[0].reshape(-1, ROW_SUB, LANES)],
                         axis=1).astype(BF16)

    mod = ada_project(jnp.concatenate([c_prompt, c_sample], axis=0), w_ada[0], b_ada[0])
    mod_p = [m.reshape(b, 1, d) for m in jnp.split(mod[:b], N_MOD, axis=-1)]
    mod_s = [m.reshape(1, db, d) for m in jnp.split(mod[b:], N_MOD, axis=-1)]

    cos_p, sin_p = rope_tables(jnp.arange(s, dtype=jnp.int32))
    proj = norm_proj(x_prompt, mod_p[0], mod_p[1], norm1_g[0], w_in[0], cos_p, sin_p, rope_cols=D_ATTN + D_KV)
    kmean = moba_block_means(proj)
    o_attn = moba_prompt_attention(proj, kmean)
    o_lru, h_last_p = lru_prompt(proj, jnp.zeros((b, VREG_SUBLANES, D_LRU), F32), jnp.zeros((b, 1, D_LRU), F32),
                                 conv_w[0], conv_b[0], lwa_bf, lwx_bf, lru_ba[0], lru_bx[0], lru_lam[0])
    merged = branch_merge(o_lru, o_attn, proj, wl_bf, wa_bf)
    x1 = out_project(merged, x_prompt, mod_p[2], wo_bf)
    y_prompt = _peer_block(x1, mod_p[3], mod_p[4], mod_p[5], norm2_g[0], peer_wq[0], peer_subkeys[0], uv,
                           final_g, cos_p, sin_p)
    k_prompt = proj[:, :, COL_K:COL_K + D_KV].reshape(1, b, s, N_KV_HEADS, HEAD_DIM)
    v_prompt = proj[:, :, COL_V:COL_V + D_KV].reshape(1, b, s, N_KV_HEADS, HEAD_DIM)
    conv_prompt = proj[:, s - (CONV_W - 1):, COL_XL:COL_XL + D_LRU].reshape(1, b, CONV_W - 1, D_LRU)

    xs = x_sample.reshape(1, db, d)
    cos_s, sin_s = rope_tables(jnp.full((db,), past_len, jnp.int32))
    proj_s = norm_proj(xs, mod_s[0], mod_s[1], norm1_g[0], w_in[0], cos_s, sin_s, rope_cols=D_ATTN + D_KV)
    q_s = proj_s[0, :, COL_Q:COL_Q + D_ATTN].reshape(db, N_HEADS, HEAD_DIM)
    k_new = proj_s[0, :, COL_K:COL_K + D_KV].reshape(db, 1, D_KV)
    v_new = proj_s[0, :, COL_V:COL_V + D_KV].reshape(db, 1, D_KV)
    n_pool = cache_k.shape[1]
    cache_k3 = cache_k[0].reshape(n_pool, PAGE_SIZE * N_KV_HEADS, HEAD_DIM)
    cache_v3 = cache_v[0].reshape(n_pool, PAGE_SIZE * N_KV_HEADS, HEAD_DIM)
    kmean_s = sample_block_means(cache_k3, page_table)
    sel = sample_select(q_s, kmean_s)[:, :, :MOBA_TOPK].reshape(db, N_HEADS * MOBA_TOPK)
    o_attn_s = sample_attention(q_s, k_new, v_new, cache_k3, cache_v3, page_table, sel)
    o_lru_s, h_last_s = lru_step(proj_s, jnp.transpose(state_conv[0], (1, 0, 2)), state_lru[0],
                                 conv_w[0], conv_b[0], lwa_bf, lwx_bf, lru_ba[0], lru_bx[0], lru_lam[0])
    merged_s = branch_merge(o_lru_s.reshape(1, db, d), o_attn_s.astype(BF16).reshape(1, db, d), proj_s, wl_bf, wa_bf)
    x1_s = out_project(merged_s, xs, mod_s[2], wo_bf)
    y_sample = _peer_block(x1_s, mod_s[3], mod_s[4], mod_s[5], norm2_g[0], peer_wq[0], peer_subkeys[0], uv,
                           final_g, cos_s, sin_s)
    xl_s = proj_s[0, :, COL_XL:COL_XL + D_LRU]
    conv_sample = jnp.concatenate([state_conv[0][:, 1:], xl_s[:, None, :]], axis=1)[None]

    return (y_prompt, y_sample.reshape(db, 1, d), k_prompt, v_prompt,
            h_last_p.reshape(1, b, D_LRU), conv_prompt,
            k_new.reshape(1, db, 1, N_KV_HEADS, HEAD_DIM), v_new.reshape(1, db, 1, N_KV_HEADS, HEAD_DIM),
            h_last_s.reshape(1, db, D_LRU), conv_sample)
```

```python
import functools
import math

import jax
import jax.numpy as jnp
from jax import lax
from jax.experimental import pallas as pl
from jax.experimental.pallas import tpu as pltpu

D_MODEL = 2048
PAGE_SIZE = 128
N_HEADS = 16
N_KV_HEADS = 4
HEAD_DIM = 128
GROUP = N_HEADS // N_KV_HEADS
D_ATTN = N_HEADS * HEAD_DIM
D_KV = N_KV_HEADS * HEAD_DIM
ROPE_THETA = 10000.0
MOBA_BLOCK = 256
MOBA_TOPK = 3
D_LRU = D_MODEL
LRU_HEADS = 16
LRU_BLOCK = D_LRU // LRU_HEADS
CONV_W = 4
LRU_C = 8.0
PEER_HEADS = 8
PEER_N_KEYS = 128
PEER_HALF = 128
PEER_TOPK = 16
PEER_SEL = PEER_HEADS * PEER_TOPK
N_MOD = 6
EPS = 1e-6

COL_Q = 0
COL_K = D_ATTN
COL_V = D_ATTN + D_KV
COL_XL = D_ATTN + 2 * D_KV
COL_GL = COL_XL + D_LRU
COL_GB = COL_GL + D_LRU
IN_COLS = COL_GB + 2 * D_MODEL

VREG_SUBLANES = 8
LANES = 128
ROW_SUB = D_MODEL // LANES
NEG = -1e30
BF16 = jnp.bfloat16
F32 = jnp.float32
V7X_VMEM_BYTES = 64 * 1024 * 1024
VMEM_LIMIT = V7X_VMEM_BYTES * 13 // 16


def _cparams(sem):
    return pltpu.CompilerParams(dimension_semantics=sem, vmem_limit_bytes=VMEM_LIMIT)


def _gelu(x):
    return 0.5 * x * (1.0 + lax.erf(x * (1.0 / math.sqrt(2.0))))


def _mod_spec(mod, tm, tn, tiled_cols=True):
    col = (lambda j: j) if tiled_cols else (lambda j: 0)
    if mod.shape[1] == 1:
        return pl.BlockSpec((1, 1, tn), lambda b, i, j: (b, 0, col(j)))
    return pl.BlockSpec((1, tm, tn), lambda b, i, j: (b, i, col(j)))


def _ada_kernel(c_ref, w_ref, b_ref, o_ref):
    o_ref[...] = jnp.dot(c_ref[...].astype(BF16), w_ref[...].astype(BF16),
                         preferred_element_type=F32) + b_ref[...]


def ada_project(c_all, w_ada, b_ada):
    m, d = c_all.shape
    n = w_ada.shape[1]
    tn = 1024
    return pl.pallas_call(
        _ada_kernel,
        out_shape=jax.ShapeDtypeStruct((m, n), F32),
        grid=(n // tn,),
        in_specs=[pl.BlockSpec((m, d), lambda j: (0, 0)),
                  pl.BlockSpec((d, tn), lambda j: (0, j)),
                  pl.BlockSpec((1, tn), lambda j: (0, j))],
        out_specs=pl.BlockSpec((m, tn), lambda j: (0, j)),
        compiler_params=_cparams(("arbitrary",)),
        name="ada_project",
    )(c_all, w_ada, b_ada.reshape(1, n))


def _norm_proj_kernel(x_ref, sh_ref, sc_ref, g_ref, w_ref, cos_ref, sin_ref, o_ref, hs_ref, *, rope_tiles):
    j = pl.program_id(2)

    @pl.when(j == 0)
    def _():
        x = x_ref[0]
        y = x * lax.rsqrt(jnp.mean(x * x, axis=-1, keepdims=True) + EPS) * g_ref[...]
        hs_ref[...] = (y * (1.0 + sc_ref[0]) + sh_ref[0]).astype(BF16)

    acc = jnp.dot(hs_ref[...], w_ref[...].astype(BF16), preferred_element_type=F32)

    if rope_tiles:
        @pl.when(j < rope_tiles)
        def _():
            cos = cos_ref[...]
            sin = sin_ref[...]
            parts = []
            for hh in range(acc.shape[1] // HEAD_DIM):
                a = acc[:, hh * HEAD_DIM:(hh + 1) * HEAD_DIM]
                parts.append(a * cos + pltpu.roll(a, HEAD_DIM // 2, 1) * sin)
            o_ref[0] = jnp.concatenate(parts, axis=1)

        @pl.when(j >= rope_tiles)
        def _():
            o_ref[0] = acc
    else:
        o_ref[0] = acc


def norm_proj(x, sh, sc, g, w, cos, sin, *, rope_cols):
    b, s, d = x.shape
    n = w.shape[1]
    tm = min(1024, s)
    tn = 512
    assert s % tm == 0 and n % tn == 0 and rope_cols % tn == 0
    return pl.pallas_call(
        functools.partial(_norm_proj_kernel, rope_tiles=rope_cols // tn),
        out_shape=jax.ShapeDtypeStruct((b, s, n), F32),
        grid=(b, s // tm, n // tn),
        in_specs=[pl.BlockSpec((1, tm, d), lambda bb, i, j: (bb, i, 0)),
                  _mod_spec(sh, tm, d, False), _mod_spec(sc, tm, d, False),
                  pl.BlockSpec((1, d), lambda bb, i, j: (0, 0)),
                  pl.BlockSpec((d, tn), lambda bb, i, j: (0, j)),
                  pl.BlockSpec((tm, HEAD_DIM), lambda bb, i, j: (i, 0)),
                  pl.BlockSpec((tm, HEAD_DIM), lambda bb, i, j: (i, 0))],
        out_specs=pl.BlockSpec((1, tm, tn), lambda bb, i, j: (bb, i, j)),
        scratch_shapes=[pltpu.VMEM((tm, d), BF16)],
        compiler_params=_cparams(("arbitrary", "arbitrary", "arbitrary")),
        name="norm_proj",
    )(x, sh, sc, g.reshape(1, d), w, cos, sin)


def rope_tables(pos):
    half = HEAD_DIM // 2
    inv = jnp.exp(-math.log(ROPE_THETA) * jnp.arange(half, dtype=F32) / half)
    ang = pos.astype(F32)[:, None] * inv[None, :]
    cos, sin = jnp.cos(ang), jnp.sin(ang)
    return jnp.concatenate([cos, cos], axis=1), jnp.concatenate([-sin, sin], axis=1)


def _kmean_kernel(k_ref, o_ref):
    o_ref[0, 0] = jnp.mean(k_ref[0], axis=0, keepdims=True)


def moba_block_means(proj):
    b, s, _ = proj.shape
    nb = s // MOBA_BLOCK
    out = pl.pallas_call(
        _kmean_kernel,
        out_shape=jax.ShapeDtypeStruct((b, nb, 1, D_KV), F32),
        grid=(b, nb),
        in_specs=[pl.BlockSpec((1, MOBA_BLOCK, D_KV), lambda bb, i: (bb, i, COL_K // D_KV))],
        out_specs=pl.BlockSpec((1, 1, 1, D_KV), lambda bb, i: (bb, i, 0, 0)),
        compiler_params=_cparams(("arbitrary", "arbitrary")),
        name="moba_block_means",
    )(proj)
    return out.reshape(b, nb, D_KV)


def _top3_bias(gate, own_blk, nb):
    blk = lax.broadcasted_iota(jnp.int32, gate.shape, 0)
    past = blk < own_blk
    work = jnp.where(past, gate, -jnp.inf)
    sel = jnp.zeros(gate.shape, jnp.bool_)
    for _ in range(min(MOBA_TOPK, nb)):
        m = jnp.max(work, axis=0, keepdims=True)
        idx = jnp.min(jnp.where(work == m, blk, nb), axis=0, keepdims=True)
        pick = blk == idx
        sel = jnp.logical_or(sel, jnp.logical_and(pick, past))
        work = jnp.where(pick, -jnp.inf, work)
    return jnp.where(sel, 0.0, NEG).astype(F32)


def _moba_kernel(q_ref, k_ref, v_ref, km_ref, o_ref, kb_ref, vt_ref, qa_ref, m_ref, l_ref, acc_ref, sa_ref, sb_ref):
    qi = pl.program_id(2)
    nb = km_ref.shape[1]
    tq = MOBA_BLOCK
    rows = GROUP * tq
    hd = HEAD_DIM
    assert nb <= hd

    @pl.when(qi == 0)
    def _():
        lane = lax.broadcasted_iota(jnp.int32, (MOBA_BLOCK, hd), 1)

        def cp(jb, c):
            st = pl.multiple_of(jb * MOBA_BLOCK, MOBA_BLOCK)
            kb_ref[pl.ds(st, MOBA_BLOCK), 0:hd] = k_ref[0, pl.ds(st, MOBA_BLOCK), :].astype(BF16)
            kb_ref[pl.ds(st, MOBA_BLOCK), hd:2 * hd] = jnp.where(lane == jb, 1.0, 0.0).astype(BF16)
            vt_ref[jb] = v_ref[0, pl.ds(st, MOBA_BLOCK), :].T.astype(BF16)
            return c
        lax.fori_loop(0, nb, cp, 0)

    q = q_ref[0]
    qcat = jnp.concatenate([q[:, g * hd:(g + 1) * hd] for g in range(GROUP)], axis=0)
    qt = qcat.T
    gate = jnp.dot(km_ref[0], qt, precision=lax.Precision.HIGHEST, preferred_element_type=F32)
    bias = _top3_bias(gate, qi, nb)
    qa_ref[0:hd] = (qt * (hd ** -0.5)).astype(BF16)
    qa_ref[hd:2 * hd] = jnp.concatenate([bias, jnp.zeros((hd - nb, rows), F32)], axis=0).astype(BF16)

    st = pl.multiple_of(qi * MOBA_BLOCK, MOBA_BLOCK)
    s = jnp.dot(kb_ref[pl.ds(st, MOBA_BLOCK), 0:hd], qa_ref[0:hd], preferred_element_type=F32)
    key_t = lax.broadcasted_iota(jnp.int32, (MOBA_BLOCK, rows), 0)
    q_t = lax.broadcasted_iota(jnp.int32, (MOBA_BLOCK, rows), 1) % tq
    s = jnp.where(key_t <= q_t, s, NEG)
    m0 = jnp.max(s, axis=0, keepdims=True)
    p = jnp.exp(s - m0)
    m_ref[...] = m0
    l_ref[...] = jnp.sum(p, axis=0, keepdims=True)
    acc_ref[...] = jnp.dot(vt_ref[qi], p.astype(BF16), preferred_element_type=F32)

    def scores(j):
        stj = pl.multiple_of(j * MOBA_BLOCK, MOBA_BLOCK)
        return jnp.dot(kb_ref[pl.ds(stj, MOBA_BLOCK), :], qa_ref[...], preferred_element_type=F32)

    def update(sj, j):
        m_old = m_ref[...]
        m_new = jnp.maximum(m_old, jnp.max(sj, axis=0, keepdims=True))
        alpha = jnp.exp(m_old - m_new)
        pj = jnp.exp(sj - m_new)
        l_ref[...] = alpha * l_ref[...] + jnp.sum(pj, axis=0, keepdims=True)
        acc_ref[...] = alpha * acc_ref[...] + jnp.dot(vt_ref[j], pj.astype(BF16), preferred_element_type=F32)
        m_ref[...] = m_new

    last = jnp.maximum(qi - 1, 0)
    sa_ref[...] = scores(0)

    def two(jj, c):
        j0 = 2 * jj
        sb_ref[...] = scores(j0 + 1)
        update(sa_ref[...], j0)
        sa_ref[...] = scores(jnp.minimum(j0 + 2, last))
        update(sb_ref[...], j0 + 1)
        return c

    lax.fori_loop(0, qi // 2, two, 0)

    @pl.when(qi % 2 == 1)
    def _():
        update(sa_ref[...], qi - 1)

    o = (acc_ref[...] / l_ref[...]).T
    for g in range(GROUP):
        o_ref[0, :, g * HEAD_DIM:(g + 1) * HEAD_DIM] = o[g * tq:(g + 1) * tq].astype(o_ref.dtype)


def moba_prompt_attention(proj, kmean):
    b, s, _ = proj.shape
    nb = s // MOBA_BLOCK
    rows = GROUP * MOBA_BLOCK
    gw = GROUP * HEAD_DIM
    return pl.pallas_call(
        _moba_kernel,
        out_shape=jax.ShapeDtypeStruct((b, s, D_ATTN), BF16),
        grid=(b, N_KV_HEADS, nb),
        in_specs=[pl.BlockSpec((1, MOBA_BLOCK, gw), lambda bb, hk, qi: (bb, qi, hk)),
                  pl.BlockSpec((1, s, HEAD_DIM), lambda bb, hk, qi: (bb, 0, COL_K // HEAD_DIM + hk)),
                  pl.BlockSpec((1, s, HEAD_DIM), lambda bb, hk, qi: (bb, 0, COL_V // HEAD_DIM + hk)),
                  pl.BlockSpec((1, nb, HEAD_DIM), lambda bb, hk, qi: (bb, 0, hk))],
        out_specs=pl.BlockSpec((1, MOBA_BLOCK, gw), lambda bb, hk, qi: (bb, qi, hk)),
        scratch_shapes=[pltpu.VMEM((s, 2 * HEAD_DIM), BF16),
                        pltpu.VMEM((nb, HEAD_DIM, MOBA_BLOCK), BF16),
                        pltpu.VMEM((2 * HEAD_DIM, rows), BF16),
                        pltpu.VMEM((1, rows), F32),
                        pltpu.VMEM((1, rows), F32),
                        pltpu.VMEM((HEAD_DIM, rows), F32),
                        pltpu.VMEM((MOBA_BLOCK, rows), F32),
                        pltpu.VMEM((MOBA_BLOCK, rows), F32)],
        compiler_params=_cparams(("arbitrary", "arbitrary", "arbitrary")),
        name="moba_prompt_attention",
    )(proj, proj, proj, kmean)


def _page_sum_kernel(pt_ref, *refs, pages_per_step):
    del pt_ref
    o_ref = refs[pages_per_step]
    sub = VREG_SUBLANES
    for r in range(0, pages_per_step, 2):
        grp = (jnp.sum(refs[r][0].reshape(-1, sub, HEAD_DIM), axis=0)
               + jnp.sum(refs[r + 1][0].reshape(-1, sub, HEAD_DIM), axis=0))
        o_ref[0, 0, r // 2] = (grp[0:N_KV_HEADS] + grp[N_KV_HEADS:sub]) * (1.0 / MOBA_BLOCK)


def sample_block_means(cache_k3, page_table):
    db, n_pages = page_table.shape
    pps = next(c for c in (16, 8, 2) if n_pages % c == 0)
    assert n_pages % pps == 0
    bps = pps // 2
    nbp = n_pages // 2
    prow = PAGE_SIZE * N_KV_HEADS

    def page_spec(r):
        return pl.BlockSpec((1, prow, HEAD_DIM), lambda bb, p, pt: (pt[bb, p * pps + r], 0, 0))

    out = pl.pallas_call(
        functools.partial(_page_sum_kernel, pages_per_step=pps),
        out_shape=jax.ShapeDtypeStruct((db, n_pages // pps, bps, N_KV_HEADS, HEAD_DIM), F32),
        grid_spec=pltpu.PrefetchScalarGridSpec(
            num_scalar_prefetch=1,
            grid=(db, n_pages // pps),
            in_specs=[page_spec(r) for r in range(pps)],
            out_specs=pl.BlockSpec((1, 1, bps, N_KV_HEADS, HEAD_DIM), lambda bb, p, pt: (bb, p, 0, 0, 0)),
        ),
        compiler_params=_cparams(("arbitrary", "arbitrary")),
        name="sample_block_means",
    )(page_table, *([cache_k3] * pps))
    return out.reshape(db, nbp, D_KV)


def _sample_select_kernel(q_ref, km_ref, o_ref):
    nbp = km_ref.shape[1]
    q = q_ref[0]
    head_kv = lax.broadcasted_iota(jnp.int32, (N_HEADS, nbp), 0) // GROUP
    gate = jnp.zeros((N_HEADS, nbp), F32)
    for hk in range(N_KV_HEADS):
        km = km_ref[0, :, hk * HEAD_DIM:(hk + 1) * HEAD_DIM]
        gk = lax.dot_general(q, km, (((1,), (1,)), ((), ())), precision=lax.Precision.HIGHEST,
                             preferred_element_type=F32)
        gate = jnp.where(head_kv == hk, gk, gate)
    blk = lax.broadcasted_iota(jnp.int32, gate.shape, 1)
    lane = lax.broadcasted_iota(jnp.int32, (N_HEADS, LANES), 1)
    out = jnp.zeros((N_HEADS, LANES), jnp.int32)
    work = gate
    for r in range(MOBA_TOPK):
        m = jnp.max(work, axis=1, keepdims=True)
        idx = jnp.min(jnp.where(work == m, blk, nbp), axis=1, keepdims=True)
        out = jnp.where(lane == r, idx, out)
        work = jnp.where(blk == idx, -jnp.inf, work)
    o_ref[0] = out


def sample_select(q_s, kmean_s):
    db = q_s.shape[0]
    nbp = kmean_s.shape[1]
    return pl.pallas_call(
        _sample_select_kernel,
        out_shape=jax.ShapeDtypeStruct((db, N_HEADS, LANES), jnp.int32),
        grid=(db,),
        in_specs=[pl.BlockSpec((1, N_HEADS, HEAD_DIM), lambda bb: (bb, 0, 0)),
                  pl.BlockSpec((1, nbp, D_KV), lambda bb: (bb, 0, 0))],
        out_specs=pl.BlockSpec((1, N_HEADS, LANES), lambda bb: (bb, 0, 0)),
        compiler_params=_cparams(("arbitrary",)),
        name="sample_select",
    )(q_s, kmean_s)


def _sample_attn_kernel(pt_ref, sel_ref, q_ref, *refs, n_pages):
    del pt_ref, sel_ref
    kp_refs = refs[:n_pages]
    vp_refs = refs[n_pages:2 * n_pages]
    kn_ref, vn_ref, o_ref, s_ref = refs[2 * n_pages:]
    h = pl.program_id(1)
    hk = h // GROUP
    prow = PAGE_SIZE * N_KV_HEADS
    qh = q_ref[0, pl.ds(h, 1), :] * (HEAD_DIM ** -0.5)
    qb = jnp.broadcast_to(qh, (HEAD_DIM, HEAD_DIM)).astype(BF16)

    def pick_kv(row):
        out = jnp.zeros((1, HEAD_DIM), F32)
        for kk in range(N_KV_HEADS):
            out = out + jnp.where(hk == kk, row[:, kk * HEAD_DIM:(kk + 1) * HEAD_DIM], 0.0)
        return out

    s_self = jnp.sum(pick_kv(kn_ref[0]) * qh, axis=1, keepdims=True)
    row_kv = lax.broadcasted_iota(jnp.int32, (prow, HEAD_DIM), 0) % N_KV_HEADS
    m = jnp.broadcast_to(s_self, (1, HEAD_DIM))
    for j in range(n_pages):
        sj = lax.dot_general(kp_refs[j][0].astype(BF16), qb, (((1,), (1,)), ((), ())), preferred_element_type=F32)
        sj = jnp.where(row_kv == hk, sj, NEG)
        s_ref[j] = sj
        m = jnp.maximum(m, jnp.max(sj, axis=0, keepdims=True))
    p_self = jnp.exp(s_self - m)
    l = p_self
    acc = p_self * pick_kv(vn_ref[0])
    for j in range(n_pages):
        pj = jnp.exp(s_ref[j] - m)
        l = l + jnp.sum(pj, axis=0, keepdims=True)
        acc = acc + jnp.sum(pj * vp_refs[j][0], axis=0, keepdims=True)
    o_ref[0, 0] = acc / l


def sample_attention(q_s, k_new, v_new, cache_k3, cache_v3, page_table, sel):
    db = q_s.shape[0]
    pages_per_blk = MOBA_BLOCK // PAGE_SIZE
    n_pages = MOBA_TOPK * pages_per_blk
    prow = PAGE_SIZE * N_KV_HEADS

    def page_spec(j):
        def idx(bb, h, pt, sl):
            return (pt[bb, sl[bb, h * MOBA_TOPK + j // pages_per_blk] * pages_per_blk + j % pages_per_blk], 0, 0)
        return pl.BlockSpec((1, prow, HEAD_DIM), idx)

    out = pl.pallas_call(
        functools.partial(_sample_attn_kernel, n_pages=n_pages),
        out_shape=jax.ShapeDtypeStruct((db, N_HEADS, 1, HEAD_DIM), F32),
        grid_spec=pltpu.PrefetchScalarGridSpec(
            num_scalar_prefetch=2,
            grid=(db, N_HEADS),
            in_specs=([pl.BlockSpec((1, N_HEADS, HEAD_DIM), lambda bb, h, pt, sl: (bb, 0, 0))]
                      + [page_spec(j) for j in range(n_pages)] + [page_spec(j) for j in range(n_pages)]
                      + [pl.BlockSpec((1, 1, D_KV), lambda bb, h, pt, sl: (bb, 0, 0)),
                         pl.BlockSpec((1, 1, D_KV), lambda bb, h, pt, sl: (bb, 0, 0))]),
            out_specs=pl.BlockSpec((1, 1, 1, HEAD_DIM), lambda bb, h, pt, sl: (bb, h, 0, 0)),
            scratch_shapes=[pltpu.VMEM((n_pages, prow, HEAD_DIM), F32)],
        ),
        compiler_params=_cparams(("arbitrary", "arbitrary")),
        name="sample_attention",
    )(page_table, sel, q_s, *([cache_k3] * n_pages), *([cache_v3] * n_pages), k_new, v_new)
    return out.reshape(db, D_ATTN)


def _lru_gates(xc, wa_ref, wx_ref, ba, bx, lam):
    nh = xc.shape[1] // LRU_BLOCK
    xb = xc.astype(BF16)
    ra, rx = [], []
    for hh in range(nh):
        xs = xb[:, hh * LRU_BLOCK:(hh + 1) * LRU_BLOCK]
        ra.append(jnp.dot(xs, wa_ref[hh], preferred_element_type=F32))
        rx.append(jnp.dot(xs, wx_ref[hh], preferred_element_type=F32))
    r = jax.nn.sigmoid(jnp.concatenate(ra, axis=1) + ba)
    gi = jax.nn.sigmoid(jnp.concatenate(rx, axis=1) + bx)
    log_a = (-LRU_C * r) * jax.nn.softplus(-lam)
    a = jnp.exp(log_a)
    bxs = jnp.sqrt(1.0 - jnp.exp(2.0 * log_a)) * (gi * xc)
    return a, bxs


def _lru_kernel(xl_ref, gl_ref, cprev_ref, h0_ref, cw_ref, cb_ref, wa_ref, wx_ref, ba_ref, bx_ref, lam_ref,
                o_ref, hl_ref, xprev_ref, hc_ref, a_ref, b_ref):
    ti = pl.program_id(2)
    tt = xl_ref.shape[1]
    sub = VREG_SUBLANES

    @pl.when(ti == 0)
    def _():
        xprev_ref[...] = cprev_ref[0]
        hc_ref[...] = jnp.broadcast_to(h0_ref[0], hc_ref.shape)

    x = xl_ref[0]
    cw = cw_ref[...]
    xc = cb_ref[...] + x * cw[CONV_W - 1:CONV_W]
    first = jnp.concatenate([xprev_ref[...], x[0:sub]], axis=0)
    xc_first = cb_ref[...] + x[0:sub] * cw[CONV_W - 1:CONV_W]
    for d in range(1, CONV_W):
        wj = cw[CONV_W - 1 - d:CONV_W - d]
        xc = xc + pltpu.roll(x, d, 0) * wj
        xc_first = xc_first + pltpu.roll(first, d, 0)[sub:2 * sub] * wj
    xc = jnp.concatenate([xc_first, xc[sub:]], axis=0)
    xprev_ref[...] = x[tt - sub:tt]

    a, bxs = _lru_gates(xc, wa_ref, wx_ref, ba_ref[...], bx_ref[...], lam_ref[...])

    rowm = lax.broadcasted_iota(jnp.int32, a.shape, 0) % sub
    for d in (1, 2, 4):
        ok = rowm >= d
        a_sh = pltpu.roll(a, d, 0)
        b_sh = pltpu.roll(bxs, d, 0)
        bxs = jnp.where(ok, a * b_sh + bxs, bxs)
        a = jnp.where(ok, a * a_sh, a)
    a_ref[...] = a
    b_ref[...] = bxs

    def grp(gidx, hprev):
        st = pl.multiple_of(gidx * sub, sub)
        hg = a_ref[pl.ds(st, sub), :] * hprev + b_ref[pl.ds(st, sub), :]
        b_ref[pl.ds(st, sub), :] = hg
        return jnp.broadcast_to(hg[sub - 1:sub], hprev.shape)

    hlast = lax.fori_loop(0, tt // sub, grp, hc_ref[...])
    hc_ref[...] = hlast
    o_ref[0] = (b_ref[...] * _gelu(gl_ref[0])).astype(o_ref.dtype)

    @pl.when(ti == pl.num_programs(2) - 1)
    def _():
        hl_ref[0] = hlast[0:1]


def lru_prompt(proj, conv_prev8, h0, conv_w, conv_b, wa_bf, wx_bf, ba, bx, lam):
    b, s, _ = proj.shape
    c = D_LRU
    tc = 512
    tt = min(512, s)
    assert s % tt == 0
    hpt = tc // LRU_BLOCK
    vec = lambda: pl.BlockSpec((1, tc), lambda bb, ci, ti: (0, ci))
    return pl.pallas_call(
        _lru_kernel,
        out_shape=[jax.ShapeDtypeStruct((b, s, c), BF16), jax.ShapeDtypeStruct((b, 1, c), F32)],
        grid=(b, c // tc, s // tt),
        in_specs=[pl.BlockSpec((1, tt, tc), lambda bb, ci, ti: (bb, ti, COL_XL // tc + ci)),
                  pl.BlockSpec((1, tt, tc), lambda bb, ci, ti: (bb, ti, COL_GL // tc + ci)),
                  pl.BlockSpec((1, VREG_SUBLANES, tc), lambda bb, ci, ti: (bb, 0, ci)),
                  pl.BlockSpec((1, 1, tc), lambda bb, ci, ti: (bb, 0, ci)),
                  pl.BlockSpec((CONV_W, tc), lambda bb, ci, ti: (0, ci)),
                  vec(),
                  pl.BlockSpec((hpt, LRU_BLOCK, LRU_BLOCK), lambda bb, ci, ti: (ci, 0, 0)),
                  pl.BlockSpec((hpt, LRU_BLOCK, LRU_BLOCK), lambda bb, ci, ti: (ci, 0, 0)),
                  vec(), vec(), vec()],
        out_specs=[pl.BlockSpec((1, tt, tc), lambda bb, ci, ti: (bb, ti, ci)),
                   pl.BlockSpec((1, 1, tc), lambda bb, ci, ti: (bb, 0, ci))],
        scratch_shapes=[pltpu.VMEM((VREG_SUBLANES, tc), F32), pltpu.VMEM((VREG_SUBLANES, tc), F32),
                        pltpu.VMEM((tt, tc), F32), pltpu.VMEM((tt, tc), F32)],
        compiler_params=_cparams(("arbitrary", "arbitrary", "arbitrary")),
        name="lru_prompt",
    )(proj, proj, conv_prev8, h0, conv_w, conv_b.reshape(1, c), wa_bf, wx_bf,
      ba.reshape(1, c), bx.reshape(1, c), lam.reshape(1, c))


def _lru_step_kernel(xl_ref, gl_ref, cprev_ref, h0_ref, cw_ref, cb_ref, wa_ref, wx_ref, ba_ref, bx_ref, lam_ref,
                     o_ref, hl_ref):
    x = xl_ref[0]
    cw = cw_ref[...]
    xc = cb_ref[...] + x * cw[CONV_W - 1:CONV_W]
    for j in range(CONV_W - 1):
        xc = xc + cprev_ref[j] * cw[j:j + 1]
    a, bxs = _lru_gates(xc, wa_ref, wx_ref, ba_ref[...], bx_ref[...], lam_ref[...])
    h = a * h0_ref[...] + bxs
    hl_ref[...] = h
    o_ref[...] = (h * _gelu(gl_ref[0])).astype(o_ref.dtype)


def lru_step(proj_s, conv_prev_t, h0, conv_w, conv_b, wa_bf, wx_bf, ba, bx, lam):
    _, db, _ = proj_s.shape
    c = D_LRU
    tc = 1024
    hpt = tc // LRU_BLOCK
    vec = lambda: pl.BlockSpec((1, tc), lambda ci: (0, ci))
    return pl.pallas_call(
        _lru_step_kernel,
        out_shape=[jax.ShapeDtypeStruct((db, c), BF16), jax.ShapeDtypeStruct((db, c), F32)],
        grid=(c // tc,),
        in_specs=[pl.BlockSpec((1, db, tc), lambda ci: (0, 0, COL_XL // tc + ci)),
                  pl.BlockSpec((1, db, tc), lambda ci: (0, 0, COL_GL // tc + ci)),
                  pl.BlockSpec((CONV_W - 1, db, tc), lambda ci: (0, 0, ci)),
                  pl.BlockSpec((db, tc), lambda ci: (0, ci)),
                  pl.BlockSpec((CONV_W, tc), lambda ci: (0, ci)),
                  vec(),
                  pl.BlockSpec((hpt, LRU_BLOCK, LRU_BLOCK), lambda ci: (ci, 0, 0)),
                  pl.BlockSpec((hpt, LRU_BLOCK, LRU_BLOCK), lambda ci: (ci, 0, 0)),
                  vec(), vec(), vec()],
        out_specs=[pl.BlockSpec((db, tc), lambda ci: (0, ci)), pl.BlockSpec((db, tc), lambda ci: (0, ci))],
        compiler_params=_cparams(("arbitrary",)),
        name="lru_step",
    )(proj_s, proj_s, conv_prev_t, h0, conv_w, conv_b.reshape(1, c), wa_bf, wx_bf,
      ba.reshape(1, c), bx.reshape(1, c), lam.reshape(1, c))


def _merge_kernel(ol_ref, oa_ref, g0_ref, g1_ref, wl_ref, wa_ref, o_ref):
    yl = jnp.dot(ol_ref[0], wl_ref[...], preferred_element_type=F32)
    ya = jnp.dot(oa_ref[0], wa_ref[...], preferred_element_type=F32)
    o_ref[0] = (jax.nn.sigmoid(g0_ref[0]) * yl + jax.nn.sigmoid(g1_ref[0]) * ya).astype(o_ref.dtype)


def branch_merge(o_lru, o_attn, proj, wl_bf, wa_bf):
    b, s, d = o_lru.shape
    tm = min(1024, s)
    tn = 512
    return pl.pallas_call(
        _merge_kernel,
        out_shape=jax.ShapeDtypeStruct((b, s, d), BF16),
        grid=(b, s // tm, d // tn),
        in_specs=[pl.BlockSpec((1, tm, d), lambda bb, i, j: (bb, i, 0)),
                  pl.BlockSpec((1, tm, d), lambda bb, i, j: (bb, i, 0)),
                  pl.BlockSpec((1, tm, tn), lambda bb, i, j: (bb, i, COL_GB // tn + j)),
                  pl.BlockSpec((1, tm, tn), lambda bb, i, j: (bb, i, (COL_GB + D_MODEL) // tn + j)),
                  pl.BlockSpec((d, tn), lambda bb, i, j: (0, j)),
                  pl.BlockSpec((d, tn), lambda bb, i, j: (0, j))],
        out_specs=pl.BlockSpec((1, tm, tn), lambda bb, i, j: (bb, i, j)),
        compiler_params=_cparams(("arbitrary", "arbitrary", "arbitrary")),
        name="branch_merge",
    )(o_lru, o_attn, proj, proj, wl_bf, wa_bf)


def _out_proj_kernel(m_ref, x_ref, gt_ref, w_ref, o_ref):
    o_ref[0] = x_ref[0] + gt_ref[0] * jnp.dot(m_ref[0], w_ref[...], preferred_element_type=F32)


def out_project(merged, x, gt, w_bf):
    b, s, d = x.shape
    tm = min(1024, s)
    tn = 512
    return pl.pallas_call(
        _out_proj_kernel,
        out_shape=jax.ShapeDtypeStruct((b, s, d), F32),
        grid=(b, s // tm, d // tn),
        in_specs=[pl.BlockSpec((1, tm, d), lambda bb, i, j: (bb, i, 0)),
                  pl.BlockSpec((1, tm, tn), lambda bb, i, j: (bb, i, j)),
                  _mod_spec(gt, tm, tn),
                  pl.BlockSpec((d, tn), lambda bb, i, j: (0, j))],
        out_specs=pl.BlockSpec((1, tm, tn), lambda bb, i, j: (bb, i, j)),
        compiler_params=_cparams(("arbitrary", "arbitrary", "arbitrary")),
        name="out_project",
    )(merged, x, gt, w_bf)


def _topk_rows(s, k, n):
    row = lax.broadcasted_iota(jnp.int32, s.shape, 0)
    slot = lax.broadcasted_iota(jnp.int32, (k, s.shape[1]), 0)
    vals = jnp.zeros((k, s.shape[1]), F32)
    idxs = jnp.zeros((k, s.shape[1]), jnp.int32)
    for r in range(k):
        m = jnp.max(s, axis=0, keepdims=True)
        am = jnp.min(jnp.where(s == m, row, n), axis=0, keepdims=True)
        vals = jnp.where(slot == r, m, vals)
        idxs = jnp.where(slot == r, am, idxs)
        s = jnp.where(row == am, -jnp.inf, s)
    return vals, idxs


def _peer_topk_kernel(q_ref, sk_ref, e_ref, g_ref):
    tt = q_ref.shape[0]
    kk = PEER_TOPK
    for hh in range(PEER_HEADS):
        sv, si = [], []
        for p in range(2):
            c0 = (hh * 2 + p) * PEER_HALF
            qc = q_ref[:, c0:c0 + PEER_HALF]
            s = lax.dot_general(sk_ref[hh, p], qc, (((1,), (1,)), ((), ())), precision=lax.Precision.HIGHEST,
                                preferred_element_type=F32)
            v, i = _topk_rows(s, kk, PEER_N_KEYS)
            sv.append(v)
            si.append(i)
        half = kk // 2
        cand = jnp.concatenate([sv[0][0:1] + sv[1]]
                               + [sv[0][a:a + 1] + sv[1][0:half] for a in range(1, half)]
                               + [sv[0][half:kk] + sv[1][0:1]], axis=0)
        cidx = jnp.concatenate([si[0][0:1] * PEER_N_KEYS + si[1]]
                               + [si[0][a:a + 1] * PEER_N_KEYS + si[1][0:half] for a in range(1, half)]
                               + [si[0][half:kk] * PEER_N_KEYS + si[1][0:1]], axis=0)
        ncand = cand.shape[0]
        row = lax.broadcasted_iota(jnp.int32, cand.shape, 0)
        slot = lax.broadcasted_iota(jnp.int32, (kk, tt), 0)
        fv = jnp.zeros((kk, tt), F32)
        eid = jnp.zeros((kk, tt), jnp.int32)
        for r in range(kk):
            m = jnp.max(cand, axis=0, keepdims=True)
            am = jnp.min(jnp.where(cand == m, row, ncand), axis=0, keepdims=True)
            pick = row == am
            fv = jnp.where(slot == r, m, fv)
            eid = jnp.where(slot == r, jnp.max(jnp.where(pick, cidx, -1), axis=0, keepdims=True), eid)
            cand = jnp.where(pick, -jnp.inf, cand)
        ex = jnp.exp(fv - fv[0:1])
        g_ref[hh * kk:(hh + 1) * kk, :] = ex / jnp.sum(ex, axis=0, keepdims=True)
        e_ref[hh * kk:(hh + 1) * kk, :] = eid


def peer_topk(qp, sub_keys):
    n = qp.shape[0]
    tt = 256 if n % 256 == 0 else n
    return pl.pallas_call(
        _peer_topk_kernel,
        out_shape=[jax.ShapeDtypeStruct((PEER_SEL, n), jnp.int32), jax.ShapeDtypeStruct((PEER_SEL, n), F32)],
        grid=(n // tt,),
        in_specs=[pl.BlockSpec((tt, qp.shape[1]), lambda i: (i, 0)),
                  pl.BlockSpec(sub_keys.shape, lambda i: (0, 0, 0, 0))],
        out_specs=[pl.BlockSpec((PEER_SEL, tt), lambda i: (0, i)), pl.BlockSpec((PEER_SEL, tt), lambda i: (0, i))],
        compiler_params=_cparams(("arbitrary",)),
        name="peer_topk",
    )(qp, sub_keys)


PEER_TOK_TILE = 8


def _peer_gather_kernel(e_ref, e1_ref, e2_ref, x_ref, sh_ref, sc_ref, gt_ref, g2_ref, gf_ref, gate_ref, uv_hbm, o_ref,
                        buf0, buf1, buf2, sem, *, n):
    i = pl.program_id(0)
    nrow = PEER_TOK_TILE * PEER_SEL
    bufs = (buf0, buf1, buf2)
    nbuf = len(bufs)

    def row_copy(idx_ref, r, dst):
        return pltpu.make_async_copy(uv_hbm.at[idx_ref[r]], bufs[dst].at[r], sem.at[dst])

    def tile_wait(which):
        pltpu.make_async_copy(uv_hbm.at[pl.ds(0, nrow)], bufs[which], sem.at[which]).wait()

    @pl.when(i == 0)
    def _():
        def first(r, c):
            row_copy(e_ref, r, 0).start()
            row_copy(e1_ref, r, 1).start()
            return c
        lax.fori_loop(0, nrow, first, 0, unroll=8)

    sub_id = lax.broadcasted_iota(jnp.int32, (ROW_SUB, LANES), 0)
    lane_id = lax.broadcasted_iota(jnp.int32, (ROW_SUB, LANES), 1)
    diag = (lane_id % ROW_SUB) == sub_id
    mat_r = lax.broadcasted_iota(jnp.int32, (LANES, LANES), 0)
    mat_c = lax.broadcasted_iota(jnp.int32, (LANES, LANES), 1)
    group_sum = (mat_r // ROW_SUB == mat_c // ROW_SUB).astype(F32)
    spread = (mat_r % VREG_SUBLANES == mat_c // ROW_SUB).astype(F32)

    def hi_lo(x):
        hi = x.astype(BF16)
        return jnp.concatenate([hi, (x - hi.astype(F32)).astype(BF16)], axis=0)

    def fold(x):
        return x[0:ROW_SUB] + x[ROW_SUB:2 * ROW_SUB]

    def tile(cur):
        ahead = (cur + 2) % nbuf
        buf = bufs[cur]
        tile_wait(cur)
        toks = range(PEER_TOK_TILE)

        def issue(t):
            for k in range(PEER_SEL):
                row_copy(e2_ref, t * PEER_SEL + k, ahead).start(priority=k % 2)

        def rms(v, g):
            return v * lax.rsqrt(jnp.mean(v * v, axis=-1, keepdims=True) + EPS) * g

        h_all = rms(x_ref[...], g2_ref[...]) * (1.0 + sc_ref[0]) + sh_ref[0]

        zs, gx = [], []
        for t in toks:
            issue(t)
            h = jnp.concatenate([h_all[t:t + 1, r * LANES:(r + 1) * LANES] for r in range(ROW_SUB)], axis=0)
            ub = buf[pl.ds(t * PEER_SEL, PEER_SEL), pl.ds(0, ROW_SUB), :].reshape(PEER_SEL * ROW_SUB, LANES)
            y = fold(lax.dot_general(hi_lo(h), ub, (((1,), (1,)), ((), ())), preferred_element_type=F32))
            zs.append(jnp.concatenate(
                [jnp.sum(jnp.where(diag, y[:, r * LANES:(r + 1) * LANES], 0.0), axis=0, keepdims=True)
                 for r in range(ROW_SUB)], axis=0))
            gx.append(jnp.where(lane_id // VREG_SUBLANES == sub_id, gate_ref[t:t + 1, :], 0.0))
        act = jnp.dot(jnp.concatenate(zs, axis=0), group_sum, precision=lax.Precision.HIGHEST,
                      preferred_element_type=F32)
        gex = jnp.dot(jnp.concatenate(gx, axis=0), spread, precision=lax.Precision.HIGHEST,
                      preferred_element_type=F32)
        w_all = gex * _gelu(act)
        for t in toks:
            whl = hi_lo(w_all[t * ROW_SUB:(t + 1) * ROW_SUB]).astype(F32)
            wexp = jnp.concatenate(
                [jnp.concatenate([jnp.where(diag, whl[r:r + 1], 0.0),
                                  jnp.where(diag, whl[ROW_SUB + r:ROW_SUB + r + 1], 0.0)], axis=0)
                 for r in range(ROW_SUB)], axis=1).astype(BF16)
            vb = buf[pl.ds(t * PEER_SEL, PEER_SEL), pl.ds(ROW_SUB, ROW_SUB), :].reshape(PEER_SEL * ROW_SUB, LANES)
            ff = fold(jnp.dot(wexp, vb, preferred_element_type=F32))
            for s in range(ROW_SUB):
                o_ref[t:t + 1, s * LANES:(s + 1) * LANES] = ff[s:s + 1]
        o_ref[...] = rms(x_ref[...] + gt_ref[0] * o_ref[...], gf_ref[...])

        @pl.when(i == n - 1)
        def _():
            tile_wait((cur + 1) % nbuf)
            tile_wait(ahead)

    for c in range(nbuf):
        @pl.when(i % nbuf == c)
        def _(c=c):
            tile(c)


def peer_gather(eidx, gates, x1, sh, sc, gt, norm_g, final_g, uv):
    b, s, d = x1.shape
    n = b * s
    tk = PEER_TOK_TILE
    nrow = tk * PEER_SEL
    nt = n // tk
    per_row = sh.shape[1] != 1
    assert s % tk == 0
    tiles_per_batch = s // tk
    if per_row:
        mod_spec = pl.BlockSpec((1, tk, d), lambda i: (0, i, 0))
    else:
        mod_spec = pl.BlockSpec((1, 1, d), lambda i: (i // tiles_per_batch, 0, 0))
    vec_spec = pl.BlockSpec((1, d), lambda i: (0, 0))
    out = pl.pallas_call(
        functools.partial(_peer_gather_kernel, n=nt),
        out_shape=jax.ShapeDtypeStruct((n, d), F32),
        grid=(nt,),
        in_specs=[pl.BlockSpec((nrow,), lambda i: (0,), memory_space=pltpu.SMEM),
                  pl.BlockSpec((nrow,), lambda i: (min(1, nt - 1),), memory_space=pltpu.SMEM),
                  pl.BlockSpec((nrow,), lambda i: (jnp.minimum(i + 2, nt - 1),), memory_space=pltpu.SMEM),
                  pl.BlockSpec((tk, d), lambda i: (i, 0)),
                  mod_spec, mod_spec, mod_spec, vec_spec, vec_spec,
                  pl.BlockSpec((tk, PEER_SEL), lambda i: (i, 0)),
                  pl.BlockSpec(memory_space=pl.ANY)],
        out_specs=pl.BlockSpec((tk, d), lambda i: (i, 0)),
        scratch_shapes=[pltpu.VMEM((nrow, 2 * ROW_SUB, LANES), BF16)] * 3 + [pltpu.SemaphoreType.DMA((3,))],
        compiler_params=_cparams(("arbitrary",)),
        name="peer_gather",
    )(eidx.reshape(-1), eidx.reshape(-1), eidx.reshape(-1), x1.reshape(n, d), sh, sc, gt, norm_g.reshape(1, d),
      final_g.reshape(1, d), gates, uv)
    return out.reshape(b, s, d)


def _peer_block(x1, sh2, sc2, gt2, norm2_g, wq, sub_keys, uv, final_g, cos, sin):
    b, s, d = x1.shape
    n = b * s
    qp = norm_proj(x1, sh2, sc2, norm2_g, wq, cos, sin, rope_cols=0)
    qp2 = qp.reshape(n, -1)
    n_pad = -(-n // LANES) * LANES
    if n_pad != n:
        qp2 = jnp.pad(qp2, ((0, n_pad - n), (0, 0)))
    e_t, g_t = peer_topk(qp2, sub_keys)
    return peer_gather(e_t.T[:n], g_t.T[:n], x1, sh2, sc2, gt2, norm2_g, final_g, uv)


def kernel(x_prompt, x_sample, c_prompt, c_sample, cache_k, cache_v, state_lru, state_conv, page_table, norm1_g, w_ada, b_ada, w_in, conv_w, conv_b, lru_wa, lru_ba, lru_wx, lru_bx, lru_lam, w_br_lru, w_br_attn, w_out, norm2_g, peer_wq, peer_subkeys, peer_u, peer_v, final_g):
    assert w_ada.shape[0] == 1, "single layer"
    b, s, d = x_prompt.shape
    db = x_sample.shape[0]
    n_pages = page_table.shape[1]
    past_len = n_pages * PAGE_SIZE
    assert s % MOBA_BLOCK == 0 and past_len % MOBA_BLOCK == 0 and x_sample.shape[1] == 1

    wl_bf, wa_bf, wo_bf = w_br_lru[0].astype(BF16), w_br_attn[0].astype(BF16), w_out[0].astype(BF16)
    lwa_bf, lwx_bf = lru_wa[0].astype(BF16), lru_wx[0].astype(BF16)
    uv = jnp.concatenate([peer_u[0].reshape(-1, ROW_SUB, LANES), peer_v[0].reshape(-1, ROW_SUB, LANES)],
                         axis=1).astype(BF16)

    mod = ada_project(jnp.concatenate([c_prompt, c_sample], axis=0), w_ada[0], b_ada[0])
    mod_p = [m.reshape(b, 1, d) for m in jnp.split(mod[:b], N_MOD, axis=-1)]
    mod_s = [m.reshape(1, db, d) for m in jnp.split(mod[b:], N_MOD, axis=-1)]

    cos_p, sin_p = rope_tables(jnp.arange(s, dtype=jnp.int32))
    proj = norm_proj(x_prompt, mod_p[0], mod_p[1], norm1_g[0], w_in[0], cos_p, sin_p, rope_cols=D_ATTN + D_KV)
    kmean = moba_block_means(proj)
    o_attn = moba_prompt_attention(proj, kmean)
    o_lru, h_last_p = lru_prompt(proj, jnp.zeros((b, VREG_SUBLANES, D_LRU), F32), jnp.zeros((b, 1, D_LRU), F32),
                                 conv_w[0], conv_b[0], lwa_bf, lwx_bf, lru_ba[0], lru_bx[0], lru_lam[0])
    merged = branch_merge(o_lru, o_attn, proj, wl_bf, wa_bf)
    x1 = out_project(merged, x_prompt, mod_p[2], wo_bf)
    y_prompt = _peer_block(x1, mod_p[3], mod_p[4], mod_p[5], norm2_g[0], peer_wq[0], peer_subkeys[0], uv,
                           final_g, cos_p, sin_p)
    k_prompt = proj[:, :, COL_K:COL_K + D_KV].reshape(1, b, s, N_KV_HEADS, HEAD_DIM)
    v_prompt = proj[:, :, COL_V:COL_V + D_KV].reshape(1, b, s, N_KV_HEADS, HEAD_DIM)
    conv_prompt = proj[:, s - (CONV_W - 1):, COL_XL:COL_XL + D_LRU].reshape(1, b, CONV_W - 1, D_LRU)

    xs = x_sample.reshape(1, db, d)
    cos_s, sin_s = rope_tables(jnp.full((db,), past_len, jnp.int32))
    proj_s = norm_proj(xs, mod_s[0], mod_s[1], norm1_g[0], w_in[0], cos_s, sin_s, rope_cols=D_ATTN + D_KV)
    q_s = proj_s[0, :, COL_Q:COL_Q + D_ATTN].reshape(db, N_HEADS, HEAD_DIM)
    k_new = proj_s[0, :, COL_K:COL_K + D_KV].reshape(db, 1, D_KV)
    v_new = proj_s[0, :, COL_V:COL_V + D_KV].reshape(db, 1, D_KV)
    n_pool = cache_k.shape[1]
    cache_k3 = cache_k[0].reshape(n_pool, PAGE_SIZE * N_KV_HEADS, HEAD_DIM)
    cache_v3 = cache_v[0].reshape(n_pool, PAGE_SIZE * N_KV_HEADS, HEAD_DIM)
    kmean_s = sample_block_means(cache_k3, page_table)
    sel = sample_select(q_s, kmean_s)[:, :, :MOBA_TOPK].reshape(db, N_HEADS * MOBA_TOPK)
    o_attn_s = sample_attention(q_s, k_new, v_new, cache_k3, cache_v3, page_table, sel)
    o_lru_s, h_last_s = lru_step(proj_s, jnp.transpose(state_conv[0], (1, 0, 2)), state_lru[0],
                                 conv_w[0], conv_b[0], lwa_bf, lwx_bf, lru_ba[0], lru_bx[0], lru_lam[0])
    merged_s = branch_merge(o_lru_s.reshape(1, db, d), o_attn_s.astype(BF16).reshape(1, db, d), proj_s, wl_bf, wa_bf)
    x1_s = out_project(merged_s, xs, mod_s[2], wo_bf)
    y_sample = _peer_block(x1_s, mod_s[3], mod_s[4], mod_s[5], norm2_g[0], peer_wq[0], peer_subkeys[0], uv,
                           final_g, cos_s, sin_s)
    xl_s = proj_s[0, :, COL_XL:COL_XL + D_LRU]
    conv_sample = jnp.concatenate([state_conv[0][:, 1:], xl_s[:, None, :]], axis=1)[None]

    return (y_prompt, y_sample.reshape(db, 1, d), k_prompt, v_prompt,
            h_last_p.reshape(1, b, D_LRU), conv_prompt,
            k_new.reshape(1, db, 1, N_KV_HEADS, HEAD_DIM), v_new.reshape(1, db, 1, N_KV_HEADS, HEAD_DIM),
            h_last_s.reshape(1, db, D_LRU), conv_sample)
```

```python
import functools
import math

import jax
import jax.numpy as jnp
from jax import lax
from jax.experimental import pallas as pl
from jax.experimental.pallas import tpu as pltpu

D_MODEL = 2048
PAGE_SIZE = 128
N_HEADS = 16
N_KV_HEADS = 4
HEAD_DIM = 128
GROUP = N_HEADS // N_KV_HEADS
D_ATTN = N_HEADS * HEAD_DIM
D_KV = N_KV_HEADS * HEAD_DIM
ROPE_THETA = 10000.0
MOBA_BLOCK = 256
MOBA_TOPK = 3
D_LRU = D_MODEL
LRU_HEADS = 16
LRU_BLOCK = D_LRU // LRU_HEADS
CONV_W = 4
LRU_C = 8.0
PEER_HEADS = 8
PEER_N_KEYS = 128
PEER_HALF = 128
PEER_TOPK = 16
PEER_SEL = PEER_HEADS * PEER_TOPK
N_MOD = 6
EPS = 1e-6

COL_Q = 0
COL_K = D_ATTN
COL_V = D_ATTN + D_KV
COL_XL = D_ATTN + 2 * D_KV
COL_GL = COL_XL + D_LRU
COL_GB = COL_GL + D_LRU
IN_COLS = COL_GB + 2 * D_MODEL

VREG_SUBLANES = 8
LANES = 128
ROW_SUB = D_MODEL // LANES
NEG = -1e30
BF16 = jnp.bfloat16
F32 = jnp.float32
V7X_VMEM_BYTES = 64 * 1024 * 1024
VMEM_LIMIT = V7X_VMEM_BYTES * 13 // 16


def _cparams(sem):
    return pltpu.CompilerParams(dimension_semantics=sem, vmem_limit_bytes=VMEM_LIMIT)


def _gelu(x):
    return 0.5 * x * (1.0 + lax.erf(x * (1.0 / math.sqrt(2.0))))


def _mod_spec(mod, tm, tn, tiled_cols=True):
    col = (lambda j: j) if tiled_cols else (lambda j: 0)
    if mod.shape[1] == 1:
        return pl.BlockSpec((1, 1, tn), lambda b, i, j: (b, 0, col(j)))
    return pl.BlockSpec((1, tm, tn), lambda b, i, j: (b, i, col(j)))


def _ada_kernel(c_ref, w_ref, b_ref, o_ref):
    o_ref[...] = jnp.dot(c_ref[...].astype(BF16), w_ref[...].astype(BF16),
                         preferred_element_type=F32) + b_ref[...]


def ada_project(c_all, w_ada, b_ada):
    m, d = c_all.shape
    n = w_ada.shape[1]
    tn = 1024
    return pl.pallas_call(
        _ada_kernel,
        out_shape=jax.ShapeDtypeStruct((m, n), F32),
        grid=(n // tn,),
        in_specs=[pl.BlockSpec((m, d), lambda j: (0, 0)),
                  pl.BlockSpec((d, tn), lambda j: (0, j)),
                  pl.BlockSpec((1, tn), lambda j: (0, j))],
        out_specs=pl.BlockSpec((m, tn), lambda j: (0, j)),
        compiler_params=_cparams(("arbitrary",)),
        name="ada_project",
    )(c_all, w_ada, b_ada.reshape(1, n))


def _norm_proj_kernel(x_ref, sh_ref, sc_ref, g_ref, w_ref, cos_ref, sin_ref, o_ref, hs_ref, *, rope_tiles):
    j = pl.program_id(2)

    @pl.when(j == 0)
    def _():
        x = x_ref[0]
        y = x * lax.rsqrt(jnp.mean(x * x, axis=-1, keepdims=True) + EPS) * g_ref[...]
        hs_ref[...] = (y * (1.0 + sc_ref[0]) + sh_ref[0]).astype(BF16)

    acc = jnp.dot(hs_ref[...], w_ref[...].astype(BF16), preferred_element_type=F32)

    if rope_tiles:
        @pl.when(j < rope_tiles)
        def _():
            cos = cos_ref[...]
            sin = sin_ref[...]
            parts = []
            for hh in range(acc.shape[1] // HEAD_DIM):
                a = acc[:, hh * HEAD_DIM:(hh + 1) * HEAD_DIM]
                parts.append(a * cos + pltpu.roll(a, HEAD_DIM // 2, 1) * sin)
            o_ref[0] = jnp.concatenate(parts, axis=1)

        @pl.when(j >= rope_tiles)
        def _():
            o_ref[0] = acc
    else:
        o_ref[0] = acc


def norm_proj(x, sh, sc, g, w, cos, sin, *, rope_cols):
    b, s, d = x.shape
    n = w.shape[1]
    tm = min(1024, s)
    tn = 512
    assert s % tm == 0 and n % tn == 0 and rope_cols % tn == 0
    return pl.pallas_call(
        functools.partial(_norm_proj_kernel, rope_tiles=rope_cols // tn),
        out_shape=jax.ShapeDtypeStruct((b, s, n), F32),
        grid=(b, s // tm, n // tn),
        in_specs=[pl.BlockSpec((1, tm, d), lambda bb, i, j: (bb, i, 0)),
                  _mod_spec(sh, tm, d, False), _mod_spec(sc, tm, d, False),
                  pl.BlockSpec((1, d), lambda bb, i, j: (0, 0)),
                  pl.BlockSpec((d, tn), lambda bb, i, j: (0, j)),
                  pl.BlockSpec((tm, HEAD_DIM), lambda bb, i, j: (i, 0)),
                  pl.BlockSpec((tm, HEAD_DIM), lambda bb, i, j: (i, 0))],
        out_specs=pl.BlockSpec((1, tm, tn), lambda bb, i, j: (bb, i, j)),
        scratch_shapes=[pltpu.VMEM((tm, d), BF16)],
        compiler_params=_cparams(("arbitrary", "arbitrary", "arbitrary")),
        name="norm_proj",
    )(x, sh, sc, g.reshape(1, d), w, cos, sin)


def rope_tables(pos):
    half = HEAD_DIM // 2
    inv = jnp.exp(-math.log(ROPE_THETA) * jnp.arange(half, dtype=F32) / half)
    ang = pos.astype(F32)[:, None] * inv[None, :]
    cos, sin = jnp.cos(ang), jnp.sin(ang)
    return jnp.concatenate([cos, cos], axis=1), jnp.concatenate([-sin, sin], axis=1)


def _kmean_kernel(k_ref, o_ref):
    o_ref[0, 0] = jnp.mean(k_ref[0], axis=0, keepdims=True)


def moba_block_means(proj):
    b, s, _ = proj.shape
    nb = s // MOBA_BLOCK
    out = pl.pallas_call(
        _kmean_kernel,
        out_shape=jax.ShapeDtypeStruct((b, nb, 1, D_KV), F32),
        grid=(b, nb),
        in_specs=[pl.BlockSpec((1, MOBA_BLOCK, D_KV), lambda bb, i: (bb, i, COL_K // D_KV))],
        out_specs=pl.BlockSpec((1, 1, 1, D_KV), lambda bb, i: (bb, i, 0, 0)),
        compiler_params=_cparams(("arbitrary", "arbitrary")),
        name="moba_block_means",
    )(proj)
    return out.reshape(b, nb, D_KV)


def _top3_bias(gate, own_blk, nb):
    blk = lax.broadcasted_iota(jnp.int32, gate.shape, 0)
    past = blk < own_blk
    work = jnp.where(past, gate, -jnp.inf)
    sel = jnp.zeros(gate.shape, jnp.bool_)
    for _ in range(min(MOBA_TOPK, nb)):
        m = jnp.max(work, axis=0, keepdims=True)
        idx = jnp.min(jnp.where(work == m, blk, nb), axis=0, keepdims=True)
        pick = blk == idx
        sel = jnp.logical_or(sel, jnp.logical_and(pick, past))
        work = jnp.where(pick, -jnp.inf, work)
    return jnp.where(sel, 0.0, NEG).astype(F32)


def _moba_kernel(q_ref, k_ref, v_ref, km_ref, o_ref, kb_ref, vt_ref, qa_ref, m_ref, l_ref, acc_ref, sa_ref, sb_ref):
    qi = pl.program_id(2)
    nb = km_ref.shape[1]
    tq = MOBA_BLOCK
    rows = GROUP * tq
    hd = HEAD_DIM
    assert nb <= hd

    @pl.when(qi == 0)
    def _():
        lane = lax.broadcasted_iota(jnp.int32, (MOBA_BLOCK, hd), 1)

        def cp(jb, c):
            st = pl.multiple_of(jb * MOBA_BLOCK, MOBA_BLOCK)
            kb_ref[pl.ds(st, MOBA_BLOCK), 0:hd] = k_ref[0, pl.ds(st, MOBA_BLOCK), :].astype(BF16)
            kb_ref[pl.ds(st, MOBA_BLOCK), hd:2 * hd] = jnp.where(lane == jb, 1.0, 0.0).astype(BF16)
            vt_ref[jb] = v_ref[0, pl.ds(st, MOBA_BLOCK), :].T.astype(BF16)
            return c
        lax.fori_loop(0, nb, cp, 0)

    q = q_ref[0]
    qcat = jnp.concatenate([q[:, g * hd:(g + 1) * hd] for g in range(GROUP)], axis=0)
    qt = qcat.T
    gate = jnp.dot(km_ref[0], qt, precision=lax.Precision.HIGHEST, preferred_element_type=F32)
    bias = _top3_bias(gate, qi, nb)
    qa_ref[0:hd] = (qt * (hd ** -0.5)).astype(BF16)
    qa_ref[hd:2 * hd] = jnp.concatenate([bias, jnp.zeros((hd - nb, rows), F32)], axis=0).astype(BF16)

    st = pl.multiple_of(qi * MOBA_BLOCK, MOBA_BLOCK)
    s = jnp.dot(kb_ref[pl.ds(st, MOBA_BLOCK), 0:hd], qa_ref[0:hd], preferred_element_type=F32)
    key_t = lax.broadcasted_iota(jnp.int32, (MOBA_BLOCK, rows), 0)
    q_t = lax.broadcasted_iota(jnp.int32, (MOBA_BLOCK, rows), 1) % tq
    s = jnp.where(key_t <= q_t, s, NEG)
    m0 = jnp.max(s, axis=0, keepdims=True)
    p = jnp.exp(s - m0)
    m_ref[...] = m0
    l_ref[...] = jnp.sum(p, axis=0, keepdims=True)
    acc_ref[...] = jnp.dot(vt_ref[qi], p.astype(BF16), preferred_element_type=F32)

    def scores(j):
        stj = pl.multiple_of(j * MOBA_BLOCK, MOBA_BLOCK)
        return jnp.dot(kb_ref[pl.ds(stj, MOBA_BLOCK), :], qa_ref[...], preferred_element_type=F32)

    def update(sj, j):
        m_old = m_ref[...]
        m_new = jnp.maximum(m_old, jnp.max(sj, axis=0, keepdims=True))
        alpha = jnp.exp(m_old - m_new)
        pj = jnp.exp(sj - m_new)
        l_ref[...] = alpha * l_ref[...] + jnp.sum(pj, axis=0, keepdims=True)
        acc_ref[...] = alpha * acc_ref[...] + jnp.dot(vt_ref[j], pj.astype(BF16), preferred_element_type=F32)
        m_ref[...] = m_new

    last = jnp.maximum(qi - 1, 0)
    sa_ref[...] = scores(0)

    def two(jj, c):
        j0 = 2 * jj
        sb_ref[...] = scores(j0 + 1)
        update(sa_ref[...], j0)
        sa_ref[...] = scores(jnp.minimum(j0 + 2, last))
        update(sb_ref[...], j0 + 1)
        return c

    lax.fori_loop(0, qi // 2, two, 0)

    @pl.when(qi % 2 == 1)
    def _():
        update(sa_ref[...], qi - 1)

    o = (acc_ref[...] / l_ref[...]).T
    for g in range(GROUP):
        o_ref[0, :, g * HEAD_DIM:(g + 1) * HEAD_DIM] = o[g * tq:(g + 1) * tq].astype(o_ref.dtype)


def moba_prompt_attention(proj, kmean):
    b, s, _ = proj.shape
    nb = s // MOBA_BLOCK
    rows = GROUP * MOBA_BLOCK
    gw = GROUP * HEAD_DIM
    return pl.pallas_call(
        _moba_kernel,
        out_shape=jax.ShapeDtypeStruct((b, s, D_ATTN), BF16),
        grid=(b, N_KV_HEADS, nb),
        in_specs=[pl.BlockSpec((1, MOBA_BLOCK, gw), lambda bb, hk, qi: (bb, qi, hk)),
                  pl.BlockSpec((1, s, HEAD_DIM), lambda bb, hk, qi: (bb, 0, COL_K // HEAD_DIM + hk)),
                  pl.BlockSpec((1, s, HEAD_DIM), lambda bb, hk, qi: (bb, 0, COL_V // HEAD_DIM + hk)),
                  pl.BlockSpec((1, nb, HEAD_DIM), lambda bb, hk, qi: (bb, 0, hk))],
        out_specs=pl.BlockSpec((1, MOBA_BLOCK, gw), lambda bb, hk, qi: (bb, qi, hk)),
        scratch_shapes=[pltpu.VMEM((s, 2 * HEAD_DIM), BF16),
                        pltpu.VMEM((nb, HEAD_DIM, MOBA_BLOCK), BF16),
                        pltpu.VMEM((2 * HEAD_DIM, rows), BF16),
                        pltpu.VMEM((1, rows), F32),
                        pltpu.VMEM((1, rows), F32),
                        pltpu.VMEM((HEAD_DIM, rows), F32),
                        pltpu.VMEM((MOBA_BLOCK, rows), F32),
                        pltpu.VMEM((MOBA_BLOCK, rows), F32)],
        compiler_params=_cparams(("arbitrary", "arbitrary", "arbitrary")),
        name="moba_prompt_attention",
    )(proj, proj, proj, kmean)


def _page_sum_kernel(pt_ref, *refs, pages_per_step):
    del pt_ref
    o_ref = refs[pages_per_step]
    sub = VREG_SUBLANES
    for r in range(0, pages_per_step, 2):
        grp = (jnp.sum(refs[r][0].reshape(-1, sub, HEAD_DIM), axis=0)
               + jnp.sum(refs[r + 1][0].reshape(-1, sub, HEAD_DIM), axis=0))
        o_ref[0, 0, r // 2] = (grp[0:N_KV_HEADS] + grp[N_KV_HEADS:sub]) * (1.0 / MOBA_BLOCK)


def sample_block_means(cache_k3, page_table):
    db, n_pages = page_table.shape
    pps = next(c for c in (16, 8, 2) if n_pages % c == 0)
    assert n_pages % pps == 0
    bps = pps // 2
    nbp = n_pages // 2
    prow = PAGE_SIZE * N_KV_HEADS

    def page_spec(r):
        return pl.BlockSpec((1, prow, HEAD_DIM), lambda bb, p, pt: (pt[bb, p * pps + r], 0, 0))

    out = pl.pallas_call(
        functools.partial(_page_sum_kernel, pages_per_step=pps),
        out_shape=jax.ShapeDtypeStruct((db, n_pages // pps, bps, N_KV_HEADS, HEAD_DIM), F32),
        grid_spec=pltpu.PrefetchScalarGridSpec(
            num_scalar_prefetch=1,
            grid=(db, n_pages // pps),
            in_specs=[page_spec(r) for r in range(pps)],
            out_specs=pl.BlockSpec((1, 1, bps, N_KV_HEADS, HEAD_DIM), lambda bb, p, pt: (bb, p, 0, 0, 0)),
        ),
        compiler_params=_cparams(("arbitrary", "arbitrary")),
        name="sample_block_means",
    )(page_table, *([cache_k3] * pps))
    return out.reshape(db, nbp, D_KV)


def _sample_select_kernel(q_ref, km_ref, o_ref):
    nbp = km_ref.shape[1]
    q = q_ref[0]
    head_kv = lax.broadcasted_iota(jnp.int32, (N_HEADS, nbp), 0) // GROUP
    gate = jnp.zeros((N_HEADS, nbp), F32)
    for hk in range(N_KV_HEADS):
        km = km_ref[0, :, hk * HEAD_DIM:(hk + 1) * HEAD_DIM]
        gk = lax.dot_general(q, km, (((1,), (1,)), ((), ())), precision=lax.Precision.HIGHEST,
                             preferred_element_type=F32)
        gate = jnp.where(head_kv == hk, gk, gate)
    blk = lax.broadcasted_iota(jnp.int32, gate.shape, 1)
    lane = lax.broadcasted_iota(jnp.int32, (N_HEADS, LANES), 1)
    out = jnp.zeros((N_HEADS, LANES), jnp.int32)
    work = gate
    for r in range(MOBA_TOPK):
        m = jnp.max(work, axis=1, keepdims=True)
        idx = jnp.min(jnp.where(work == m, blk, nbp), axis=1, keepdims=True)
        out = jnp.where(lane == r, idx, out)
        work = jnp.where(blk == idx, -jnp.inf, work)
    o_ref[0] = out


def sample_select(q_s, kmean_s):
    db = q_s.shape[0]
    nbp = kmean_s.shape[1]
    return pl.pallas_call(
        _sample_select_kernel,
        out_shape=jax.ShapeDtypeStruct((db, N_HEADS, LANES), jnp.int32),
        grid=(db,),
        in_specs=[pl.BlockSpec((1, N_HEADS, HEAD_DIM), lambda bb: (bb, 0, 0)),
                  pl.BlockSpec((1, nbp, D_KV), lambda bb: (bb, 0, 0))],
        out_specs=pl.BlockSpec((1, N_HEADS, LANES), lambda bb: (bb, 0, 0)),
        compiler_params=_cparams(("arbitrary",)),
        name="sample_select",
    )(q_s, kmean_s)


def _sample_attn_kernel(pt_ref, sel_ref, q_ref, *refs, n_pages):
    del pt_ref, sel_ref
    kp_refs = refs[:n_pages]
    vp_refs = refs[n_pages:2 * n_pages]
    kn_ref, vn_ref, o_ref, s_ref = refs[2 * n_pages:]
    h = pl.program_id(1)
    hk = h // GROUP
    prow = PAGE_SIZE * N_KV_HEADS
    qh = q_ref[0, pl.ds(h, 1), :] * (HEAD_DIM ** -0.5)
    qb = jnp.broadcast_to(qh, (HEAD_DIM, HEAD_DIM)).astype(BF16)

    def pick_kv(row):
        out = jnp.zeros((1, HEAD_DIM), F32)
        for kk in range(N_KV_HEADS):
            out = out + jnp.where(hk == kk, row[:, kk * HEAD_DIM:(kk + 1) * HEAD_DIM], 0.0)
        return out

    s_self = jnp.sum(pick_kv(kn_ref[0]) * qh, axis=1, keepdims=True)
    row_kv = lax.broadcasted_iota(jnp.int32, (prow, HEAD_DIM), 0) % N_KV_HEADS
    m = jnp.broadcast_to(s_self, (1, HEAD_DIM))
    for j in range(n_pages):
        sj = lax.dot_general(kp_refs[j][0].astype(BF16), qb, (((1,), (1,)), ((), ())), preferred_element_type=F32)
        sj = jnp.where(row_kv == hk, sj, NEG)
        s_ref[j] = sj
        m = jnp.maximum(m, jnp.max(sj, axis=0, keepdims=True))
    p_self = jnp.exp(s_self - m)
    l = p_self
    acc = p_self * pick_kv(vn_ref[0])
    for j in range(n_pages):
        pj = jnp.exp(s_ref[j] - m)
        l = l + jnp.sum(pj, axis=0, keepdims=True)
        acc = acc + jnp.sum(pj * vp_refs[j][0], axis=0, keepdims=True)
    o_ref[0, 0] = acc / l


def sample_attention(q_s, k_new, v_new, cache_k3, cache_v3, page_table, sel):
    db = q_s.shape[0]
    pages_per_blk = MOBA_BLOCK // PAGE_SIZE
    n_pages = MOBA_TOPK * pages_per_blk
    prow = PAGE_SIZE * N_KV_HEADS

    def page_spec(j):
        def idx(bb, h, pt, sl):
            return (pt[bb, sl[bb, h * MOBA_TOPK + j // pages_per_blk] * pages_per_blk + j % pages_per_blk], 0, 0)
        return pl.BlockSpec((1, prow, HEAD_DIM), idx)

    out = pl.pallas_call(
        functools.partial(_sample_attn_kernel, n_pages=n_pages),
        out_shape=jax.ShapeDtypeStruct((db, N_HEADS, 1, HEAD_DIM), F32),
        grid_spec=pltpu.PrefetchScalarGridSpec(
            num_scalar_prefetch=2,
            grid=(db, N_HEADS),
            in_specs=([pl.BlockSpec((1, N_HEADS, HEAD_DIM), lambda bb, h, pt, sl: (bb, 0, 0))]
                      + [page_spec(j) for j in range(n_pages)] + [page_spec(j) for j in range(n_pages)]
                      + [pl.BlockSpec((1, 1, D_KV), lambda bb, h, pt, sl: (bb, 0, 0)),
                         pl.BlockSpec((1, 1, D_KV), lambda bb, h, pt, sl: (bb, 0, 0))]),
            out_specs=pl.BlockSpec((1, 1, 1, HEAD_DIM), lambda bb, h, pt, sl: (bb, h, 0, 0)),
            scratch_shapes=[pltpu.VMEM((n_pages, prow, HEAD_DIM), F32)],
        ),
        compiler_params=_cparams(("arbitrary", "arbitrary")),
        name="sample_attention",
    )(page_table, sel, q_s, *([cache_k3] * n_pages), *([cache_v3] * n_pages), k_new, v_new)
    return out.reshape(db, D_ATTN)


def _lru_gates(xc, wa_ref, wx_ref, ba, bx, lam):
    nh = xc.shape[1] // LRU_BLOCK
    xb = xc.astype(BF16)
    ra, rx = [], []
    for hh in range(nh):
        xs = xb[:, hh * LRU_BLOCK:(hh + 1) * LRU_BLOCK]
        ra.append(jnp.dot(xs, wa_ref[hh], preferred_element_type=F32))
        rx.append(jnp.dot(xs, wx_ref[hh], preferred_element_type=F32))
    r = jax.nn.sigmoid(jnp.concatenate(ra, axis=1) + ba)
    gi = jax.nn.sigmoid(jnp.concatenate(rx, axis=1) + bx)
    log_a = (-LRU_C * r) * jax.nn.softplus(-lam)
    a = jnp.exp(log_a)
    bxs = jnp.sqrt(1.0 - jnp.exp(2.0 * log_a)) * (gi * xc)
    return a, bxs


def _lru_kernel(xl_ref, gl_ref, cprev_ref, h0_ref, cw_ref, cb_ref, wa_ref, wx_ref, ba_ref, bx_ref, lam_ref,
                o_ref, hl_ref, xprev_ref, hc_ref, a_ref, b_ref):
    ti = pl.program_id(2)
    tt = xl_ref.shape[1]
    sub = VREG_SUBLANES

    @pl.when(ti == 0)
    def _():
        xprev_ref[...] = cprev_ref[0]
        hc_ref[...] = jnp.broadcast_to(h0_ref[0], hc_ref.shape)

    x = xl_ref[0]
    cw = cw_ref[...]
    xc = cb_ref[...] + x * cw[CONV_W - 1:CONV_W]
    first = jnp.concatenate([xprev_ref[...], x[0:sub]], axis=0)
    xc_first = cb_ref[...] + x[0:sub] * cw[CONV_W - 1:CONV_W]
    for d in range(1, CONV_W):
        wj = cw[CONV_W - 1 - d:CONV_W - d]
        xc = xc + pltpu.roll(x, d, 0) * wj
        xc_first = xc_first + pltpu.roll(first, d, 0)[sub:2 * sub] * wj
    xc = jnp.concatenate([xc_first, xc[sub:]], axis=0)
    xprev_ref[...] = x[tt - sub:tt]

    a, bxs = _lru_gates(xc, wa_ref, wx_ref, ba_ref[...], bx_ref[...], lam_ref[...])

    rowm = lax.broadcasted_iota(jnp.int32, a.shape, 0) % sub
    for d in (1, 2, 4):
        ok = rowm >= d
        a_sh = pltpu.roll(a, d, 0)
        b_sh = pltpu.roll(bxs, d, 0)
        bxs = jnp.where(ok, a * b_sh + bxs, bxs)
        a = jnp.where(ok, a * a_sh, a)
    a_ref[...] = a
    b_ref[...] = bxs

    def grp(gidx, hprev):
        st = pl.multiple_of(gidx * sub, sub)
        hg = a_ref[pl.ds(st, sub), :] * hprev + b_ref[pl.ds(st, sub), :]
        b_ref[pl.ds(st, sub), :] = hg
        return jnp.broadcast_to(hg[sub - 1:sub], hprev.shape)

    hlast = lax.fori_loop(0, tt // sub, grp, hc_ref[...])
    hc_ref[...] = hlast
    o_ref[0] = (b_ref[...] * _gelu(gl_ref[0])).astype(o_ref.dtype)

    @pl.when(ti == pl.num_programs(2) - 1)
    def _():
        hl_ref[0] = hlast[0:1]


def lru_prompt(proj, conv_prev8, h0, conv_w, conv_b, wa_bf, wx_bf, ba, bx, lam):
    b, s, _ = proj.shape
    c = D_LRU
    tc = 512
    tt = min(512, s)
    assert s % tt == 0
    hpt = tc // LRU_BLOCK
    vec = lambda: pl.BlockSpec((1, tc), lambda bb, ci, ti: (0, ci))
    return pl.pallas_call(
        _lru_kernel,
        out_shape=[jax.ShapeDtypeStruct((b, s, c), BF16), jax.ShapeDtypeStruct((b, 1, c), F32)],
        grid=(b, c // tc, s // tt),
        in_specs=[pl.BlockSpec((1, tt, tc), lambda bb, ci, ti: (bb, ti, COL_XL // tc + ci)),
                  pl.BlockSpec((1, tt, tc), lambda bb, ci, ti: (bb, ti, COL_GL // tc + ci)),
                  pl.BlockSpec((1, VREG_SUBLANES, tc), lambda bb, ci, ti: (bb, 0, ci)),
                  pl.BlockSpec((1, 1, tc), lambda bb, ci, ti: (bb, 0, ci)),
                  pl.BlockSpec((CONV_W, tc), lambda bb, ci, ti: (0, ci)),
                  vec(),
                  pl.BlockSpec((hpt, LRU_BLOCK, LRU_BLOCK), lambda bb, ci, ti: (ci, 0, 0)),
                  pl.BlockSpec((hpt, LRU_BLOCK, LRU_BLOCK), lambda bb, ci, ti: (ci, 0, 0)),
                  vec(), vec(), vec()],
        out_specs=[pl.BlockSpec((1, tt, tc), lambda bb, ci, ti: (bb, ti, ci)),
                   pl.BlockSpec((1, 1, tc), lambda bb, ci, ti: (bb, 0, ci))],
        scratch_shapes=[pltpu.VMEM((VREG_SUBLANES, tc), F32), pltpu.VMEM((VREG_SUBLANES, tc), F32),
                        pltpu.VMEM((tt, tc), F32), pltpu.VMEM((tt, tc), F32)],
        compiler_params=_cparams(("arbitrary", "arbitrary", "arbitrary")),
        name="lru_prompt",
    )(proj, proj, conv_prev8, h0, conv_w, conv_b.reshape(1, c), wa_bf, wx_bf,
      ba.reshape(1, c), bx.reshape(1, c), lam.reshape(1, c))


def _lru_step_kernel(xl_ref, gl_ref, cprev_ref, h0_ref, cw_ref, cb_ref, wa_ref, wx_ref, ba_ref, bx_ref, lam_ref,
                     o_ref, hl_ref):
    x = xl_ref[0]
    cw = cw_ref[...]
    xc = cb_ref[...] + x * cw[CONV_W - 1:CONV_W]
    for j in range(CONV_W - 1):
        xc = xc + cprev_ref[j] * cw[j:j + 1]
    a, bxs = _lru_gates(xc, wa_ref, wx_ref, ba_ref[...], bx_ref[...], lam_ref[...])
    h = a * h0_ref[...] + bxs
    hl_ref[...] = h
    o_ref[...] = (h * _gelu(gl_ref[0])).astype(o_ref.dtype)


def lru_step(proj_s, conv_prev_t, h0, conv_w, conv_b, wa_bf, wx_bf, ba, bx, lam):
    _, db, _ = proj_s.shape
    c = D_LRU
    tc = 1024
    hpt = tc // LRU_BLOCK
    vec = lambda: pl.BlockSpec((1, tc), lambda ci: (0, ci))
    return pl.pallas_call(
        _lru_step_kernel,
        out_shape=[jax.ShapeDtypeStruct((db, c), BF16), jax.ShapeDtypeStruct((db, c), F32)],
        grid=(c // tc,),
        in_specs=[pl.BlockSpec((1, db, tc), lambda ci: (0, 0, COL_XL // tc + ci)),
                  pl.BlockSpec((1, db, tc), lambda ci: (0, 0, COL_GL // tc + ci)),
                  pl.BlockSpec((CONV_W - 1, db, tc), lambda ci: (0, 0, ci)),
                  pl.BlockSpec((db, tc), lambda ci: (0, ci)),
                  pl.BlockSpec((CONV_W, tc), lambda ci: (0, ci)),
                  vec(),
                  pl.BlockSpec((hpt, LRU_BLOCK, LRU_BLOCK), lambda ci: (ci, 0, 0)),
                  pl.BlockSpec((hpt, LRU_BLOCK, LRU_BLOCK), lambda ci: (ci, 0, 0)),
                  vec(), vec(), vec()],
        out_specs=[pl.BlockSpec((db, tc), lambda ci: (0, ci)), pl.BlockSpec((db, tc), lambda ci: (0, ci))],
        compiler_params=_cparams(("arbitrary",)),
        name="lru_step",
    )(proj_s, proj_s, conv_prev_t, h0, conv_w, conv_b.reshape(1, c), wa_bf, wx_bf,
      ba.reshape(1, c), bx.reshape(1, c), lam.reshape(1, c))


def _merge_kernel(ol_ref, oa_ref, g0_ref, g1_ref, wl_ref, wa_ref, o_ref):
    yl = jnp.dot(ol_ref[0], wl_ref[...], preferred_element_type=F32)
    ya = jnp.dot(oa_ref[0], wa_ref[...], preferred_element_type=F32)
    o_ref[0] = (jax.nn.sigmoid(g0_ref[0]) * yl + jax.nn.sigmoid(g1_ref[0]) * ya).astype(o_ref.dtype)


def branch_merge(o_lru, o_attn, proj, wl_bf, wa_bf):
    b, s, d = o_lru.shape
    tm = min(1024, s)
    tn = 512
    return pl.pallas_call(
        _merge_kernel,
        out_shape=jax.ShapeDtypeStruct((b, s, d), BF16),
        grid=(b, s // tm, d // tn),
        in_specs=[pl.BlockSpec((1, tm, d), lambda bb, i, j: (bb, i, 0)),
                  pl.BlockSpec((1, tm, d), lambda bb, i, j: (bb, i, 0)),
                  pl.BlockSpec((1, tm, tn), lambda bb, i, j: (bb, i, COL_GB // tn + j)),
                  pl.BlockSpec((1, tm, tn), lambda bb, i, j: (bb, i, (COL_GB + D_MODEL) // tn + j)),
                  pl.BlockSpec((d, tn), lambda bb, i, j: (0, j)),
                  pl.BlockSpec((d, tn), lambda bb, i, j: (0, j))],
        out_specs=pl.BlockSpec((1, tm, tn), lambda bb, i, j: (bb, i, j)),
        compiler_params=_cparams(("arbitrary", "arbitrary", "arbitrary")),
        name="branch_merge",
    )(o_lru, o_attn, proj, proj, wl_bf, wa_bf)


def _out_proj_kernel(m_ref, x_ref, gt_ref, w_ref, o_ref):
    o_ref[0] = x_ref[0] + gt_ref[0] * jnp.dot(m_ref[0], w_ref[...], preferred_element_type=F32)


def out_project(merged, x, gt, w_bf):
    b, s, d = x.shape
    tm = min(1024, s)
    tn = 512
    return pl.pallas_call(
        _out_proj_kernel,
        out_shape=jax.ShapeDtypeStruct((b, s, d), F32),
        grid=(b, s // tm, d // tn),
        in_specs=[pl.BlockSpec((1, tm, d), lambda bb, i, j: (bb, i, 0)),
                  pl.BlockSpec((1, tm, tn), lambda bb, i, j: (bb, i, j)),
                  _mod_spec(gt, tm, tn),
                  pl.BlockSpec((d, tn), lambda bb, i, j: (0, j))],
        out_specs=pl.BlockSpec((1, tm, tn), lambda bb, i, j: (bb, i, j)),
        compiler_params=_cparams(("arbitrary", "arbitrary", "arbitrary")),
        name="out_project",
    )(merged, x, gt, w_bf)


def _topk_rows(s, k, n):
    row = lax.broadcasted_iota(jnp.int32, s.shape, 0)
    slot = lax.broadcasted_iota(jnp.int32, (k, s.shape[1]), 0)
    vals = jnp.zeros((k, s.shape[1]), F32)
    idxs = jnp.zeros((k, s.shape[1]), jnp.int32)
    for r in range(k):
        m = jnp.max(s, axis=0, keepdims=True)
        am = jnp.min(jnp.where(s == m, row, n), axis=0, keepdims=True)
        vals = jnp.where(slot == r, m, vals)
        idxs = jnp.where(slot == r, am, idxs)
        s = jnp.where(row == am, -jnp.inf, s)
    return vals, idxs


def _peer_topk_kernel(q_ref, sk_ref, e_ref, g_ref):
    tt = q_ref.shape[0]
    kk = PEER_TOPK
    for hh in range(PEER_HEADS):
        sv, si = [], []
        for p in range(2):
            c0 = (hh * 2 + p) * PEER_HALF
            qc = q_ref[:, c0:c0 + PEER_HALF]
            s = lax.dot_general(sk_ref[hh, p], qc, (((1,), (1,)), ((), ())), precision=lax.Precision.HIGHEST,
                                preferred_element_type=F32)
            v, i = _topk_rows(s, kk, PEER_N_KEYS)
            sv.append(v)
            si.append(i)
        half = kk // 2
        cand = jnp.concatenate([sv[0][0:1] + sv[1]]
                               + [sv[0][a:a + 1] + sv[1][0:half] for a in range(1, half)]
                               + [sv[0][half:kk] + sv[1][0:1]], axis=0)
        cidx = jnp.concatenate([si[0][0:1] * PEER_N_KEYS + si[1]]
                               + [si[0][a:a + 1] * PEER_N_KEYS + si[1][0:half] for a in range(1, half)]
                               + [si[0][half:kk] * PEER_N_KEYS + si[1][0:1]], axis=0)
        ncand = cand.shape[0]
        row = lax.broadcasted_iota(jnp.int32, cand.shape, 0)
        slot = lax.broadcasted_iota(jnp.int32, (kk, tt), 0)
        fv = jnp.zeros((kk, tt), F32)
        eid = jnp.zeros((kk, tt), jnp.int32)
        for r in range(kk):
            m = jnp.max(cand, axis=0, keepdims=True)
            am = jnp.min(jnp.where(cand == m, row, ncand), axis=0, keepdims=True)
            pick = row == am
            fv = jnp.where(slot == r, m, fv)
            eid = jnp.where(slot == r, jnp.max(jnp.where(pick, cidx, -1), axis=0, keepdims=True), eid)
            cand = jnp.where(pick, -jnp.inf, cand)
        ex = jnp.exp(fv - fv[0:1])
        g_ref[hh * kk:(hh + 1) * kk, :] = ex / jnp.sum(ex, axis=0, keepdims=True)
        e_ref[hh * kk:(hh + 1) * kk, :] = eid


def peer_topk(qp, sub_keys):
    n = qp.shape[0]
    tt = 256 if n % 256 == 0 else n
    return pl.pallas_call(
        _peer_topk_kernel,
        out_shape=[jax.ShapeDtypeStruct((PEER_SEL, n), jnp.int32), jax.ShapeDtypeStruct((PEER_SEL, n), F32)],
        grid=(n // tt,),
        in_specs=[pl.BlockSpec((tt, qp.shape[1]), lambda i: (i, 0)),
                  pl.BlockSpec(sub_keys.shape, lambda i: (0, 0, 0, 0))],
        out_specs=[pl.BlockSpec((PEER_SEL, tt), lambda i: (0, i)), pl.BlockSpec((PEER_SEL, tt), lambda i: (0, i))],
        compiler_params=_cparams(("arbitrary",)),
        name="peer_topk",
    )(qp, sub_keys)


PEER_TOK_TILE = 8
PEER_IDX_TILES = 16


def _peer_gather_kernel(e_ref, e1_ref, e2_ref, x_ref, sh_ref, sc_ref, gt_ref, g2_ref, gf_ref, gate_ref, uv_hbm, o_ref,
                        buf0, buf1, buf2, sem, *, n):
    i = pl.program_id(0)
    nrow = PEER_TOK_TILE * PEER_SEL
    bufs = (buf0, buf1, buf2)
    nbuf = len(bufs)

    def row_copy(idx_ref, r, dst, base=0):
        return pltpu.make_async_copy(uv_hbm.at[idx_ref[base + r]], bufs[dst].at[r], sem.at[dst])

    ahead_tile = jnp.minimum(i + 2, n - 1)
    ahead_base = lax.rem(ahead_tile, PEER_IDX_TILES) * nrow

    def tile_wait(which):
        pltpu.make_async_copy(uv_hbm.at[pl.ds(0, nrow)], bufs[which], sem.at[which]).wait()

    @pl.when(i == 0)
    def _():
        def first(r, c):
            row_copy(e_ref, r, 0).start()
            row_copy(e1_ref, r, 1).start()
            return c
        lax.fori_loop(0, nrow, first, 0, unroll=8)

    sub_id = lax.broadcasted_iota(jnp.int32, (ROW_SUB, LANES), 0)
    lane_id = lax.broadcasted_iota(jnp.int32, (ROW_SUB, LANES), 1)
    diag = (lane_id % ROW_SUB) == sub_id
    mat_r = lax.broadcasted_iota(jnp.int32, (LANES, LANES), 0)
    mat_c = lax.broadcasted_iota(jnp.int32, (LANES, LANES), 1)
    group_sum = (mat_r // ROW_SUB == mat_c // ROW_SUB).astype(F32)
    spread = (mat_r % VREG_SUBLANES == mat_c // ROW_SUB).astype(F32)

    def hi_lo(x):
        hi = x.astype(BF16)
        return jnp.concatenate([hi, (x - hi.astype(F32)).astype(BF16)], axis=0)

    def fold(x):
        return x[0:ROW_SUB] + x[ROW_SUB:2 * ROW_SUB]

    def tile(cur):
        ahead = (cur + 2) % nbuf
        buf = bufs[cur]
        tile_wait(cur)
        toks = range(PEER_TOK_TILE)

        def issue(t):
            for k in range(PEER_SEL):
                row_copy(e2_ref, t * PEER_SEL + k, ahead, ahead_base).start(priority=k % 2)

        def rms(v, g):
            return v * lax.rsqrt(jnp.mean(v * v, axis=-1, keepdims=True) + EPS) * g

        h_all = rms(x_ref[...], g2_ref[...]) * (1.0 + sc_ref[0]) + sh_ref[0]

        zs, gx = [], []
        for t in toks:
            issue(t)
            h = jnp.concatenate([h_all[t:t + 1, r * LANES:(r + 1) * LANES] for r in range(ROW_SUB)], axis=0)
            ub = buf[pl.ds(t * PEER_SEL, PEER_SEL), pl.ds(0, ROW_SUB), :].reshape(PEER_SEL * ROW_SUB, LANES)
            y = fold(lax.dot_general(hi_lo(h), ub, (((1,), (1,)), ((), ())), preferred_element_type=F32))
            zs.append(jnp.concatenate(
                [jnp.sum(jnp.where(diag, y[:, r * LANES:(r + 1) * LANES], 0.0), axis=0, keepdims=True)
                 for r in range(ROW_SUB)], axis=0))
            gx.append(jnp.where(lane_id // VREG_SUBLANES == sub_id, gate_ref[t:t + 1, :], 0.0))
        act = jnp.dot(jnp.concatenate(zs, axis=0), group_sum, precision=lax.Precision.HIGHEST,
                      preferred_element_type=F32)
        gex = jnp.dot(jnp.concatenate(gx, axis=0), spread, precision=lax.Precision.HIGHEST,
                      preferred_element_type=F32)
        w_all = gex * _gelu(act)
        for t in toks:
            whl = hi_lo(w_all[t * ROW_SUB:(t + 1) * ROW_SUB]).astype(F32)
            wexp = jnp.concatenate(
                [jnp.concatenate([jnp.where(diag, whl[r:r + 1], 0.0),
                                  jnp.where(diag, whl[ROW_SUB + r:ROW_SUB + r + 1], 0.0)], axis=0)
                 for r in range(ROW_SUB)], axis=1).astype(BF16)
            vb = buf[pl.ds(t * PEER_SEL, PEER_SEL), pl.ds(ROW_SUB, ROW_SUB), :].reshape(PEER_SEL * ROW_SUB, LANES)
            ff = fold(jnp.dot(wexp, vb, preferred_element_type=F32))
            for s in range(ROW_SUB):
                o_ref[t:t + 1, s * LANES:(s + 1) * LANES] = ff[s:s + 1]
        o_ref[...] = rms(x_ref[...] + gt_ref[0] * o_ref[...], gf_ref[...])

        @pl.when(i == n - 1)
        def _():
            tile_wait((cur + 1) % nbuf)
            tile_wait(ahead)

    for c in range(nbuf):
        @pl.when(i % nbuf == c)
        def _(c=c):
            tile(c)


def peer_gather(eidx, gates, x1, sh, sc, gt, norm_g, final_g, uv):
    b, s, d = x1.shape
    n = b * s
    tk = PEER_TOK_TILE
    nrow = tk * PEER_SEL
    nt = n // tk
    per_row = sh.shape[1] != 1
    assert s % tk == 0
    tiles_per_batch = s // tk
    if per_row:
        mod_spec = pl.BlockSpec((1, tk, d), lambda i: (0, i, 0))
    else:
        mod_spec = pl.BlockSpec((1, 1, d), lambda i: (i // tiles_per_batch, 0, 0))
    vec_spec = pl.BlockSpec((1, d), lambda i: (0, 0))
    ids = eidx.reshape(-1)
    out = pl.pallas_call(
        functools.partial(_peer_gather_kernel, n=nt),
        out_shape=jax.ShapeDtypeStruct((n, d), F32),
        grid=(nt,),
        in_specs=[pl.BlockSpec((nrow,), lambda i: (0,), memory_space=pltpu.SMEM),
                  pl.BlockSpec((nrow,), lambda i: (min(1, nt - 1),), memory_space=pltpu.SMEM),
                  pl.BlockSpec((PEER_IDX_TILES * nrow,), lambda i: (jnp.minimum(i + 2, nt - 1) // PEER_IDX_TILES,),
                               memory_space=pltpu.SMEM),
                  pl.BlockSpec((tk, d), lambda i: (i, 0)),
                  mod_spec, mod_spec, mod_spec, vec_spec, vec_spec,
                  pl.BlockSpec((tk, PEER_SEL), lambda i: (i, 0)),
                  pl.BlockSpec(memory_space=pl.ANY)],
        out_specs=pl.BlockSpec((tk, d), lambda i: (i, 0)),
        scratch_shapes=[pltpu.VMEM((nrow, 2 * ROW_SUB, LANES), BF16)] * 3 + [pltpu.SemaphoreType.DMA((3,))],
        compiler_params=_cparams(("arbitrary",)),
        name="peer_gather",
    )(ids, ids, jnp.pad(ids, (0, -ids.size % (PEER_IDX_TILES * nrow))), x1.reshape(n, d), sh, sc, gt, norm_g.reshape(1, d),
      final_g.reshape(1, d), gates, uv)
    return out.reshape(b, s, d)


def _peer_block(x1, sh2, sc2, gt2, norm2_g, wq, sub_keys, uv, final_g, cos, sin):
    b, s, d = x1.shape
    n = b * s
    qp = norm_proj(x1, sh2, sc2, norm2_g, wq, cos, sin, rope_cols=0)
    qp2 = qp.reshape(n, -1)
    n_pad = -(-n // LANES) * LANES
    if n_pad != n:
        qp2 = jnp.pad(qp2, ((0, n_pad - n), (0, 0)))
    e_t, g_t = peer_topk(qp2, sub_keys)
    return peer_gather(e_t.T[:n], g_t.T[:n], x1, sh2, sc2, gt2, norm2_g, final_g, uv)


def kernel(x_prompt, x_sample, c_prompt, c_sample, cache_k, cache_v, state_lru, state_conv, page_table, norm1_g, w_ada, b_ada, w_in, conv_w, conv_b, lru_wa, lru_ba, lru_wx, lru_bx, lru_lam, w_br_lru, w_br_attn, w_out, norm2_g, peer_wq, peer_subkeys, peer_u, peer_v, final_g):
    assert w_ada.shape[0] == 1, "single layer"
    b, s, d = x_prompt.shape
    db = x_sample.shape[0]
    n_pages = page_table.shape[1]
    past_len = n_pages * PAGE_SIZE
    assert s % MOBA_BLOCK == 0 and past_len % MOBA_BLOCK == 0 and x_sample.shape[1] == 1

    wl_bf, wa_bf, wo_bf = w_br_lru[0].astype(BF16), w_br_attn[0].astype(BF16), w_out[0].astype(BF16)
    lwa_bf, lwx_bf = lru_wa[0].astype(BF16), lru_wx[0].astype(BF16)
    uv = jnp.concatenate([peer_u[0].reshape(-1, ROW_SUB, LANES), peer_v[0].reshape(-1, ROW_SUB, LANES)],
                         axis=1).astype(BF16)

    mod = ada_project(jnp.concatenate([c_prompt, c_sample], axis=0), w_ada[0], b_ada[0])
    mod_p = [m.reshape(b, 1, d) for m in jnp.split(mod[:b], N_MOD, axis=-1)]
    mod_s = [m.reshape(1, db, d) for m in jnp.split(mod[b:], N_MOD, axis=-1)]

    cos_p, sin_p = rope_tables(jnp.arange(s, dtype=jnp.int32))
    proj = norm_proj(x_prompt, mod_p[0], mod_p[1], norm1_g[0], w_in[0], cos_p, sin_p, rope_cols=D_ATTN + D_KV)
    kmean = moba_block_means(proj)
    o_attn = moba_prompt_attention(proj, kmean)
    o_lru, h_last_p = lru_prompt(proj, jnp.zeros((b, VREG_SUBLANES, D_LRU), F32), jnp.zeros((b, 1, D_LRU), F32),
                                 conv_w[0], conv_b[0], lwa_bf, lwx_bf, lru_ba[0], lru_bx[0], lru_lam[0])
    merged = branch_merge(o_lru, o_attn, proj, wl_bf, wa_bf)
    x1 = out_project(merged, x_prompt, mod_p[2], wo_bf)
    y_prompt = _peer_block(x1, mod_p[3], mod_p[4], mod_p[5], norm2_g[0], peer_wq[0], peer_subkeys[0], uv,
                           final_g, cos_p, sin_p)
    k_prompt = proj[:, :, COL_K:COL_K + D_KV].reshape(1, b, s, N_KV_HEADS, HEAD_DIM)
    v_prompt = proj[:, :, COL_V:COL_V + D_KV].reshape(1, b, s, N_KV_HEADS, HEAD_DIM)
    conv_prompt = proj[:, s - (CONV_W - 1):, COL_XL:COL_XL + D_LRU].reshape(1, b, CONV_W - 1, D_LRU)

    xs = x_sample.reshape(1, db, d)
    cos_s, sin_s = rope_tables(jnp.full((db,), past_len, jnp.int32))
    proj_s = norm_proj(xs, mod_s[0], mod_s[1], norm1_g[0], w_in[0], cos_s, sin_s, rope_cols=D_ATTN + D_KV)
    q_s = proj_s[0, :, COL_Q:COL_Q + D_ATTN].reshape(db, N_HEADS, HEAD_DIM)
    k_new = proj_s[0, :, COL_K:COL_K + D_KV].reshape(db, 1, D_KV)
    v_new = proj_s[0, :, COL_V:COL_V + D_KV].reshape(db, 1, D_KV)
    n_pool = cache_k.shape[1]
    cache_k3 = cache_k[0].reshape(n_pool, PAGE_SIZE * N_KV_HEADS, HEAD_DIM)
    cache_v3 = cache_v[0].reshape(n_pool, PAGE_SIZE * N_KV_HEADS, HEAD_DIM)
    kmean_s = sample_block_means(cache_k3, page_table)
    sel = sample_select(q_s, kmean_s)[:, :, :MOBA_TOPK].reshape(db, N_HEADS * MOBA_TOPK)
    o_attn_s = sample_attention(q_s, k_new, v_new, cache_k3, cache_v3, page_table, sel)
    o_lru_s, h_last_s = lru_step(proj_s, jnp.transpose(state_conv[0], (1, 0, 2)), state_lru[0],
                                 conv_w[0], conv_b[0], lwa_bf, lwx_bf, lru_ba[0], lru_bx[0], lru_lam[0])
    merged_s = branch_merge(o_lru_s.reshape(1, db, d), o_attn_s.astype(BF16).reshape(1, db, d), proj_s, wl_bf, wa_bf)
    x1_s = out_project(merged_s, xs, mod_s[2], wo_bf)
    y_sample = _peer_block(x1_s, mod_s[3], mod_s[4], mod_s[5], norm2_g[0], peer_wq[0], peer_subkeys[0], uv,
                           final_g, cos_s, sin_s)
    xl_s = proj_s[0, :, COL_XL:COL_XL + D_LRU]
    conv_sample = jnp.concatenate([state_conv[0][:, 1:], xl_s[:, None, :]], axis=1)[None]

    return (y_prompt, y_sample.reshape(db, 1, d), k_prompt, v_prompt,
            h_last_p.reshape(1, b, D_LRU), conv_prompt,
            k_new.reshape(1, db, 1, N_KV_HEADS, HEAD_DIM), v_new.reshape(1, db, 1, N_KV_HEADS, HEAD_DIM),
            h_last_s.reshape(1, db, D_LRU), conv_sample)
```

```python
import functools
import math

import jax
import jax.numpy as jnp
from jax import lax
from jax.experimental import pallas as pl
from jax.experimental.pallas import tpu as pltpu

D_MODEL = 2048
PAGE_SIZE = 128
N_HEADS = 16
N_KV_HEADS = 4
HEAD_DIM = 128
GROUP = N_HEADS // N_KV_HEADS
D_ATTN = N_HEADS * HEAD_DIM
D_KV = N_KV_HEADS * HEAD_DIM
ROPE_THETA = 10000.0
MOBA_BLOCK = 256
MOBA_TOPK = 3
D_LRU = D_MODEL
LRU_HEADS = 16
LRU_BLOCK = D_LRU // LRU_HEADS
CONV_W = 4
LRU_C = 8.0
PEER_HEADS = 8
PEER_N_KEYS = 128
PEER_HALF = 128
PEER_TOPK = 16
PEER_SEL = PEER_HEADS * PEER_TOPK
N_MOD = 6
EPS = 1e-6

COL_Q = 0
COL_K = D_ATTN
COL_V = D_ATTN + D_KV
COL_XL = D_ATTN + 2 * D_KV
COL_GL = COL_XL + D_LRU
COL_GB = COL_GL + D_LRU
IN_COLS = COL_GB + 2 * D_MODEL

VREG_SUBLANES = 8
LANES = 128
ROW_SUB = D_MODEL // LANES
NEG = -1e30
BF16 = jnp.bfloat16
F32 = jnp.float32
V7X_VMEM_BYTES = 64 * 1024 * 1024
VMEM_LIMIT = V7X_VMEM_BYTES * 13 // 16


def _cparams(sem):
    return pltpu.CompilerParams(dimension_semantics=sem, vmem_limit_bytes=VMEM_LIMIT)


def _gelu(x):
    return 0.5 * x * (1.0 + lax.erf(x * (1.0 / math.sqrt(2.0))))


def _mod_spec(mod, tm, tn, tiled_cols=True):
    col = (lambda j: j) if tiled_cols else (lambda j: 0)
    if mod.shape[1] == 1:
        return pl.BlockSpec((1, 1, tn), lambda b, i, j: (b, 0, col(j)))
    return pl.BlockSpec((1, tm, tn), lambda b, i, j: (b, i, col(j)))


def _ada_kernel(c_ref, w_ref, b_ref, o_ref):
    o_ref[...] = jnp.dot(c_ref[...].astype(BF16), w_ref[...].astype(BF16),
                         preferred_element_type=F32) + b_ref[...]


def ada_project(c_all, w_ada, b_ada):
    m, d = c_all.shape
    n = w_ada.shape[1]
    tn = 1024
    return pl.pallas_call(
        _ada_kernel,
        out_shape=jax.ShapeDtypeStruct((m, n), F32),
        grid=(n // tn,),
        in_specs=[pl.BlockSpec((m, d), lambda j: (0, 0)),
                  pl.BlockSpec((d, tn), lambda j: (0, j)),
                  pl.BlockSpec((1, tn), lambda j: (0, j))],
        out_specs=pl.BlockSpec((m, tn), lambda j: (0, j)),
        compiler_params=_cparams(("arbitrary",)),
        name="ada_project",
    )(c_all, w_ada, b_ada.reshape(1, n))


def _norm_proj_kernel(x_ref, sh_ref, sc_ref, g_ref, w_ref, cos_ref, sin_ref, o_ref, hs_ref, *, rope_tiles):
    j = pl.program_id(2)

    @pl.when(j == 0)
    def _():
        x = x_ref[0]
        y = x * lax.rsqrt(jnp.mean(x * x, axis=-1, keepdims=True) + EPS) * g_ref[...]
        hs_ref[...] = (y * (1.0 + sc_ref[0]) + sh_ref[0]).astype(BF16)

    acc = jnp.dot(hs_ref[...], w_ref[...].astype(BF16), preferred_element_type=F32)

    if rope_tiles:
        @pl.when(j < rope_tiles)
        def _():
            cos = cos_ref[...]
            sin = sin_ref[...]
            parts = []
            for hh in range(acc.shape[1] // HEAD_DIM):
                a = acc[:, hh * HEAD_DIM:(hh + 1) * HEAD_DIM]
                parts.append(a * cos + pltpu.roll(a, HEAD_DIM // 2, 1) * sin)
            o_ref[0] = jnp.concatenate(parts, axis=1)

        @pl.when(j >= rope_tiles)
        def _():
            o_ref[0] = acc
    else:
        o_ref[0] = acc


def norm_proj(x, sh, sc, g, w, cos, sin, *, rope_cols):
    b, s, d = x.shape
    n = w.shape[1]
    tm = min(1024, s)
    tn = 512
    assert s % tm == 0 and n % tn == 0 and rope_cols % tn == 0
    return pl.pallas_call(
        functools.partial(_norm_proj_kernel, rope_tiles=rope_cols // tn),
        out_shape=jax.ShapeDtypeStruct((b, s, n), F32),
        grid=(b, s // tm, n // tn),
        in_specs=[pl.BlockSpec((1, tm, d), lambda bb, i, j: (bb, i, 0)),
                  _mod_spec(sh, tm, d, False), _mod_spec(sc, tm, d, False),
                  pl.BlockSpec((1, d), lambda bb, i, j: (0, 0)),
                  pl.BlockSpec((d, tn), lambda bb, i, j: (0, j)),
                  pl.BlockSpec((tm, HEAD_DIM), lambda bb, i, j: (i, 0)),
                  pl.BlockSpec((tm, HEAD_DIM), lambda bb, i, j: (i, 0))],
        out_specs=pl.BlockSpec((1, tm, tn), lambda bb, i, j: (bb, i, j)),
        scratch_shapes=[pltpu.VMEM((tm, d), BF16)],
        compiler_params=_cparams(("arbitrary", "arbitrary", "arbitrary")),
        name="norm_proj",
    )(x, sh, sc, g.reshape(1, d), w, cos, sin)


def rope_tables(pos):
    half = HEAD_DIM // 2
    inv = jnp.exp(-math.log(ROPE_THETA) * jnp.arange(half, dtype=F32) / half)
    ang = pos.astype(F32)[:, None] * inv[None, :]
    cos, sin = jnp.cos(ang), jnp.sin(ang)
    return jnp.concatenate([cos, cos], axis=1), jnp.concatenate([-sin, sin], axis=1)


def _kmean_kernel(k_ref, o_ref):
    o_ref[0, 0] = jnp.mean(k_ref[0], axis=0, keepdims=True)


def moba_block_means(proj):
    b, s, _ = proj.shape
    nb = s // MOBA_BLOCK
    out = pl.pallas_call(
        _kmean_kernel,
        out_shape=jax.ShapeDtypeStruct((b, nb, 1, D_KV), F32),
        grid=(b, nb),
        in_specs=[pl.BlockSpec((1, MOBA_BLOCK, D_KV), lambda bb, i: (bb, i, COL_K // D_KV))],
        out_specs=pl.BlockSpec((1, 1, 1, D_KV), lambda bb, i: (bb, i, 0, 0)),
        compiler_params=_cparams(("arbitrary", "arbitrary")),
        name="moba_block_means",
    )(proj)
    return out.reshape(b, nb, D_KV)


def _top3_bias(gate, own_blk, nb):
    blk = lax.broadcasted_iota(jnp.int32, gate.shape, 0)
    past = blk < own_blk
    work = jnp.where(past, gate, -jnp.inf)
    sel = jnp.zeros(gate.shape, jnp.bool_)
    for _ in range(min(MOBA_TOPK, nb)):
        m = jnp.max(work, axis=0, keepdims=True)
        idx = jnp.min(jnp.where(work == m, blk, nb), axis=0, keepdims=True)
        pick = blk == idx
        sel = jnp.logical_or(sel, jnp.logical_and(pick, past))
        work = jnp.where(pick, -jnp.inf, work)
    return jnp.where(sel, 0.0, NEG).astype(F32)


def _moba_kernel(q_ref, k_ref, v_ref, km_ref, o_ref, kb_ref, vt_ref, qa_ref, m_ref, l_ref, acc_ref, sa_ref, sb_ref):
    qi = pl.program_id(2)
    nb = km_ref.shape[1]
    tq = MOBA_BLOCK
    rows = GROUP * tq
    hd = HEAD_DIM
    assert nb <= hd

    @pl.when(qi == 0)
    def _():
        lane = lax.broadcasted_iota(jnp.int32, (MOBA_BLOCK, hd), 1)

        def cp(jb, c):
            st = pl.multiple_of(jb * MOBA_BLOCK, MOBA_BLOCK)
            kb_ref[pl.ds(st, MOBA_BLOCK), 0:hd] = k_ref[0, pl.ds(st, MOBA_BLOCK), :].astype(BF16)
            kb_ref[pl.ds(st, MOBA_BLOCK), hd:2 * hd] = jnp.where(lane == jb, 1.0, 0.0).astype(BF16)
            vt_ref[jb] = v_ref[0, pl.ds(st, MOBA_BLOCK), :].T.astype(BF16)
            return c
        lax.fori_loop(0, nb, cp, 0)

    q = q_ref[0]
    qcat = jnp.concatenate([q[:, g * hd:(g + 1) * hd] for g in range(GROUP)], axis=0)
    qt = qcat.T
    gate = jnp.dot(km_ref[0], qt, precision=lax.Precision.HIGHEST, preferred_element_type=F32)
    bias = _top3_bias(gate, qi, nb)
    qa_ref[0:hd] = (qt * (hd ** -0.5)).astype(BF16)
    qa_ref[hd:2 * hd] = jnp.concatenate([bias, jnp.zeros((hd - nb, rows), F32)], axis=0).astype(BF16)

    st = pl.multiple_of(qi * MOBA_BLOCK, MOBA_BLOCK)
    s = jnp.dot(kb_ref[pl.ds(st, MOBA_BLOCK), 0:hd], qa_ref[0:hd], preferred_element_type=F32)
    key_t = lax.broadcasted_iota(jnp.int32, (MOBA_BLOCK, rows), 0)
    q_t = lax.broadcasted_iota(jnp.int32, (MOBA_BLOCK, rows), 1) % tq
    s = jnp.where(key_t <= q_t, s, NEG)
    m0 = jnp.max(s, axis=0, keepdims=True)
    p = jnp.exp(s - m0)
    m_ref[...] = m0
    l_ref[...] = jnp.sum(p, axis=0, keepdims=True)
    acc_ref[...] = jnp.dot(vt_ref[qi], p.astype(BF16), preferred_element_type=F32)

    def scores(j):
        stj = pl.multiple_of(j * MOBA_BLOCK, MOBA_BLOCK)
        return jnp.dot(kb_ref[pl.ds(stj, MOBA_BLOCK), :], qa_ref[...], preferred_element_type=F32)

    def update(sj, j):
        m_old = m_ref[...]
        m_new = jnp.maximum(m_old, jnp.max(sj, axis=0, keepdims=True))
        alpha = jnp.exp(m_old - m_new)
        pj = jnp.exp(sj - m_new)
        l_ref[...] = alpha * l_ref[...] + jnp.sum(pj, axis=0, keepdims=True)
        acc_ref[...] = alpha * acc_ref[...] + jnp.dot(vt_ref[j], pj.astype(BF16), preferred_element_type=F32)
        m_ref[...] = m_new

    last = jnp.maximum(qi - 1, 0)
    sa_ref[...] = scores(0)

    def two(jj, c):
        j0 = 2 * jj
        sb_ref[...] = scores(j0 + 1)
        update(sa_ref[...], j0)
        sa_ref[...] = scores(jnp.minimum(j0 + 2, last))
        update(sb_ref[...], j0 + 1)
        return c

    lax.fori_loop(0, qi // 2, two, 0)

    @pl.when(qi % 2 == 1)
    def _():
        update(sa_ref[...], qi - 1)

    o = (acc_ref[...] / l_ref[...]).T
    for g in range(GROUP):
        o_ref[0, :, g * HEAD_DIM:(g + 1) * HEAD_DIM] = o[g * tq:(g + 1) * tq].astype(o_ref.dtype)


def moba_prompt_attention(proj, kmean):
    b, s, _ = proj.shape
    nb = s // MOBA_BLOCK
    rows = GROUP * MOBA_BLOCK
    gw = GROUP * HEAD_DIM
    return pl.pallas_call(
        _moba_kernel,
        out_shape=jax.ShapeDtypeStruct((b, s, D_ATTN), BF16),
        grid=(b, N_KV_HEADS, nb),
        in_specs=[pl.BlockSpec((1, MOBA_BLOCK, gw), lambda bb, hk, qi: (bb, qi, hk)),
                  pl.BlockSpec((1, s, HEAD_DIM), lambda bb, hk, qi: (bb, 0, COL_K // HEAD_DIM + hk)),
                  pl.BlockSpec((1, s, HEAD_DIM), lambda bb, hk, qi: (bb, 0, COL_V // HEAD_DIM + hk)),
                  pl.BlockSpec((1, nb, HEAD_DIM), lambda bb, hk, qi: (bb, 0, hk))],
        out_specs=pl.BlockSpec((1, MOBA_BLOCK, gw), lambda bb, hk, qi: (bb, qi, hk)),
        scratch_shapes=[pltpu.VMEM((s, 2 * HEAD_DIM), BF16),
                        pltpu.VMEM((nb, HEAD_DIM, MOBA_BLOCK), BF16),
                        pltpu.VMEM((2 * HEAD_DIM, rows), BF16),
                        pltpu.VMEM((1, rows), F32),
                        pltpu.VMEM((1, rows), F32),
                        pltpu.VMEM((HEAD_DIM, rows), F32),
                        pltpu.VMEM((MOBA_BLOCK, rows), F32),
                        pltpu.VMEM((MOBA_BLOCK, rows), F32)],
        compiler_params=_cparams(("arbitrary", "arbitrary", "arbitrary")),
        name="moba_prompt_attention",
    )(proj, proj, proj, kmean)


def _page_sum_kernel(pt_ref, *refs, pages_per_step):
    del pt_ref
    o_ref = refs[pages_per_step]
    sub = VREG_SUBLANES
    for r in range(0, pages_per_step, 2):
        grp = (jnp.sum(refs[r][0].reshape(-1, sub, HEAD_DIM), axis=0)
               + jnp.sum(refs[r + 1][0].reshape(-1, sub, HEAD_DIM), axis=0))
        o_ref[0, 0, r // 2] = (grp[0:N_KV_HEADS] + grp[N_KV_HEADS:sub]) * (1.0 / MOBA_BLOCK)


def sample_block_means(cache_k3, page_table):
    db, n_pages = page_table.shape
    pps = next(c for c in (16, 8, 2) if n_pages % c == 0)
    assert n_pages % pps == 0
    bps = pps // 2
    nbp = n_pages // 2
    prow = PAGE_SIZE * N_KV_HEADS

    def page_spec(r):
        return pl.BlockSpec((1, prow, HEAD_DIM), lambda bb, p, pt: (pt[bb, p * pps + r], 0, 0))

    out = pl.pallas_call(
        functools.partial(_page_sum_kernel, pages_per_step=pps),
        out_shape=jax.ShapeDtypeStruct((db, n_pages // pps, bps, N_KV_HEADS, HEAD_DIM), F32),
        grid_spec=pltpu.PrefetchScalarGridSpec(
            num_scalar_prefetch=1,
            grid=(db, n_pages // pps),
            in_specs=[page_spec(r) for r in range(pps)],
            out_specs=pl.BlockSpec((1, 1, bps, N_KV_HEADS, HEAD_DIM), lambda bb, p, pt: (bb, p, 0, 0, 0)),
        ),
        compiler_params=_cparams(("arbitrary", "arbitrary")),
        name="sample_block_means",
    )(page_table, *([cache_k3] * pps))
    return out.reshape(db, nbp, D_KV)


def _sample_select_kernel(q_ref, km_ref, o_ref):
    nbp = km_ref.shape[1]
    q = q_ref[0]
    head_kv = lax.broadcasted_iota(jnp.int32, (N_HEADS, nbp), 0) // GROUP
    gate = jnp.zeros((N_HEADS, nbp), F32)
    for hk in range(N_KV_HEADS):
        km = km_ref[0, :, hk * HEAD_DIM:(hk + 1) * HEAD_DIM]
        gk = lax.dot_general(q, km, (((1,), (1,)), ((), ())), precision=lax.Precision.HIGHEST,
                             preferred_element_type=F32)
        gate = jnp.where(head_kv == hk, gk, gate)
    blk = lax.broadcasted_iota(jnp.int32, gate.shape, 1)
    lane = lax.broadcasted_iota(jnp.int32, (N_HEADS, LANES), 1)
    out = jnp.zeros((N_HEADS, LANES), jnp.int32)
    work = gate
    for r in range(MOBA_TOPK):
        m = jnp.max(work, axis=1, keepdims=True)
        idx = jnp.min(jnp.where(work == m, blk, nbp), axis=1, keepdims=True)
        out = jnp.where(lane == r, idx, out)
        work = jnp.where(blk == idx, -jnp.inf, work)
    o_ref[0] = out


def sample_select(q_s, kmean_s):
    db = q_s.shape[0]
    nbp = kmean_s.shape[1]
    return pl.pallas_call(
        _sample_select_kernel,
        out_shape=jax.ShapeDtypeStruct((db, N_HEADS, LANES), jnp.int32),
        grid=(db,),
        in_specs=[pl.BlockSpec((1, N_HEADS, HEAD_DIM), lambda bb: (bb, 0, 0)),
                  pl.BlockSpec((1, nbp, D_KV), lambda bb: (bb, 0, 0))],
        out_specs=pl.BlockSpec((1, N_HEADS, LANES), lambda bb: (bb, 0, 0)),
        compiler_params=_cparams(("arbitrary",)),
        name="sample_select",
    )(q_s, kmean_s)


def _sample_attn_kernel(pt_ref, sel_ref, q_ref, *refs, n_pages):
    del pt_ref, sel_ref
    kp_refs = refs[:n_pages]
    vp_refs = refs[n_pages:2 * n_pages]
    kn_ref, vn_ref, o_ref, s_ref = refs[2 * n_pages:]
    for hh in range(SAMPLE_HEADS_PER_STEP):
        _sample_attn_head(q_ref, kp_refs[hh * (n_pages // SAMPLE_HEADS_PER_STEP):], vp_refs[hh * (n_pages // SAMPLE_HEADS_PER_STEP):],
                          kn_ref, vn_ref, o_ref, s_ref, pl.program_id(1) * SAMPLE_HEADS_PER_STEP + hh, hh,
                          n_pages // SAMPLE_HEADS_PER_STEP)


SAMPLE_HEADS_PER_STEP = 2


def _sample_attn_head(q_ref, kp_refs, vp_refs, kn_ref, vn_ref, o_ref, s_ref, h, hh, n_pages):
    hk = h // GROUP
    prow = PAGE_SIZE * N_KV_HEADS
    qh = q_ref[0, pl.ds(h, 1), :] * (HEAD_DIM ** -0.5)
    qb = jnp.broadcast_to(qh, (HEAD_DIM, HEAD_DIM)).astype(BF16)

    def pick_kv(row):
        out = jnp.zeros((1, HEAD_DIM), F32)
        for kk in range(N_KV_HEADS):
            out = out + jnp.where(hk == kk, row[:, kk * HEAD_DIM:(kk + 1) * HEAD_DIM], 0.0)
        return out

    s_self = jnp.sum(pick_kv(kn_ref[0]) * qh, axis=1, keepdims=True)
    row_kv = lax.broadcasted_iota(jnp.int32, (prow, HEAD_DIM), 0) % N_KV_HEADS
    m = jnp.broadcast_to(s_self, (1, HEAD_DIM))
    for j in range(n_pages):
        sj = lax.dot_general(kp_refs[j][0].astype(BF16), qb, (((1,), (1,)), ((), ())), preferred_element_type=F32)
        sj = jnp.where(row_kv == hk, sj, NEG)
        s_ref[j] = sj
        m = jnp.maximum(m, jnp.max(sj, axis=0, keepdims=True))
    p_self = jnp.exp(s_self - m)
    l = p_self
    acc = p_self * pick_kv(vn_ref[0])
    for j in range(n_pages):
        pj = jnp.exp(s_ref[j] - m)
        l = l + jnp.sum(pj, axis=0, keepdims=True)
        acc = acc + jnp.sum(pj * vp_refs[j][0], axis=0, keepdims=True)
    o_ref[0, hh] = acc / l


def sample_attention(q_s, k_new, v_new, cache_k3, cache_v3, page_table, sel):
    db = q_s.shape[0]
    pages_per_blk = MOBA_BLOCK // PAGE_SIZE
    n_pages = MOBA_TOPK * pages_per_blk
    prow = PAGE_SIZE * N_KV_HEADS

    hps = SAMPLE_HEADS_PER_STEP

    def page_spec(jj):
        hh, j = divmod(jj, n_pages)

        def idx(bb, hp, pt, sl):
            h = hp * hps + hh
            return (pt[bb, sl[bb, h * MOBA_TOPK + j // pages_per_blk] * pages_per_blk + j % pages_per_blk], 0, 0)
        return pl.BlockSpec((1, prow, HEAD_DIM), idx)

    out = pl.pallas_call(
        functools.partial(_sample_attn_kernel, n_pages=hps * n_pages),
        out_shape=jax.ShapeDtypeStruct((db, N_HEADS, 1, HEAD_DIM), F32),
        grid_spec=pltpu.PrefetchScalarGridSpec(
            num_scalar_prefetch=2,
            grid=(db, N_HEADS // hps),
            in_specs=([pl.BlockSpec((1, N_HEADS, HEAD_DIM), lambda bb, h, pt, sl: (bb, 0, 0))]
                      + [page_spec(j) for j in range(hps * n_pages)] + [page_spec(j) for j in range(hps * n_pages)]
                      + [pl.BlockSpec((1, 1, D_KV), lambda bb, h, pt, sl: (bb, 0, 0)),
                         pl.BlockSpec((1, 1, D_KV), lambda bb, h, pt, sl: (bb, 0, 0))]),
            out_specs=pl.BlockSpec((1, hps, 1, HEAD_DIM), lambda bb, h, pt, sl: (bb, h, 0, 0)),
            scratch_shapes=[pltpu.VMEM((n_pages, prow, HEAD_DIM), F32)],
        ),
        compiler_params=_cparams(("arbitrary", "arbitrary")),
        name="sample_attention",
    )(page_table, sel, q_s, *([cache_k3] * (hps * n_pages)), *([cache_v3] * (hps * n_pages)), k_new, v_new)
    return out.reshape(db, D_ATTN)


def _lru_gates(xc, wa_ref, wx_ref, ba, bx, lam):
    nh = xc.shape[1] // LRU_BLOCK
    xb = xc.astype(BF16)
    ra, rx = [], []
    for hh in range(nh):
        xs = xb[:, hh * LRU_BLOCK:(hh + 1) * LRU_BLOCK]
        ra.append(jnp.dot(xs, wa_ref[hh], preferred_element_type=F32))
        rx.append(jnp.dot(xs, wx_ref[hh], preferred_element_type=F32))
    r = jax.nn.sigmoid(jnp.concatenate(ra, axis=1) + ba)
    gi = jax.nn.sigmoid(jnp.concatenate(rx, axis=1) + bx)
    log_a = (-LRU_C * r) * jax.nn.softplus(-lam)
    a = jnp.exp(log_a)
    bxs = jnp.sqrt(1.0 - jnp.exp(2.0 * log_a)) * (gi * xc)
    return a, bxs


def _lru_kernel(xl_ref, gl_ref, cprev_ref, h0_ref, cw_ref, cb_ref, wa_ref, wx_ref, ba_ref, bx_ref, lam_ref,
                o_ref, hl_ref, xprev_ref, hc_ref, a_ref, b_ref):
    ti = pl.program_id(2)
    tt = xl_ref.shape[1]
    sub = VREG_SUBLANES

    @pl.when(ti == 0)
    def _():
        xprev_ref[...] = cprev_ref[0]
        hc_ref[...] = jnp.broadcast_to(h0_ref[0], hc_ref.shape)

    x = xl_ref[0]
    cw = cw_ref[...]
    xc = cb_ref[...] + x * cw[CONV_W - 1:CONV_W]
    first = jnp.concatenate([xprev_ref[...], x[0:sub]], axis=0)
    xc_first = cb_ref[...] + x[0:sub] * cw[CONV_W - 1:CONV_W]
    for d in range(1, CONV_W):
        wj = cw[CONV_W - 1 - d:CONV_W - d]
        xc = xc + pltpu.roll(x, d, 0) * wj
        xc_first = xc_first + pltpu.roll(first, d, 0)[sub:2 * sub] * wj
    xc = jnp.concatenate([xc_first, xc[sub:]], axis=0)
    xprev_ref[...] = x[tt - sub:tt]

    a, bxs = _lru_gates(xc, wa_ref, wx_ref, ba_ref[...], bx_ref[...], lam_ref[...])

    rowm = lax.broadcasted_iota(jnp.int32, a.shape, 0) % sub
    for d in (1, 2, 4):
        ok = rowm >= d
        a_sh = pltpu.roll(a, d, 0)
        b_sh = pltpu.roll(bxs, d, 0)
        bxs = jnp.where(ok, a * b_sh + bxs, bxs)
        a = jnp.where(ok, a * a_sh, a)
    a_ref[...] = a
    b_ref[...] = bxs

    def grp(gidx, hprev):
        st = pl.multiple_of(gidx * sub, sub)
        hg = a_ref[pl.ds(st, sub), :] * hprev + b_ref[pl.ds(st, sub), :]
        b_ref[pl.ds(st, sub), :] = hg
        return jnp.broadcast_to(hg[sub - 1:sub], hprev.shape)

    hlast = lax.fori_loop(0, tt // sub, grp, hc_ref[...])
    hc_ref[...] = hlast
    o_ref[0] = (b_ref[...] * _gelu(gl_ref[0])).astype(o_ref.dtype)

    @pl.when(ti == pl.num_programs(2) - 1)
    def _():
        hl_ref[0] = hlast[0:1]


def lru_prompt(proj, conv_prev8, h0, conv_w, conv_b, wa_bf, wx_bf, ba, bx, lam):
    b, s, _ = proj.shape
    c = D_LRU
    tc = 512
    tt = min(512, s)
    assert s % tt == 0
    hpt = tc // LRU_BLOCK
    vec = lambda: pl.BlockSpec((1, tc), lambda bb, ci, ti: (0, ci))
    return pl.pallas_call(
        _lru_kernel,
        out_shape=[jax.ShapeDtypeStruct((b, s, c), BF16), jax.ShapeDtypeStruct((b, 1, c), F32)],
        grid=(b, c // tc, s // tt),
        in_specs=[pl.BlockSpec((1, tt, tc), lambda bb, ci, ti: (bb, ti, COL_XL // tc + ci)),
                  pl.BlockSpec((1, tt, tc), lambda bb, ci, ti: (bb, ti, COL_GL // tc + ci)),
                  pl.BlockSpec((1, VREG_SUBLANES, tc), lambda bb, ci, ti: (bb, 0, ci)),
                  pl.BlockSpec((1, 1, tc), lambda bb, ci, ti: (bb, 0, ci)),
                  pl.BlockSpec((CONV_W, tc), lambda bb, ci, ti: (0, ci)),
                  vec(),
                  pl.BlockSpec((hpt, LRU_BLOCK, LRU_BLOCK), lambda bb, ci, ti: (ci, 0, 0)),
                  pl.BlockSpec((hpt, LRU_BLOCK, LRU_BLOCK), lambda bb, ci, ti: (ci, 0, 0)),
                  vec(), vec(), vec()],
        out_specs=[pl.BlockSpec((1, tt, tc), lambda bb, ci, ti: (bb, ti, ci)),
                   pl.BlockSpec((1, 1, tc), lambda bb, ci, ti: (bb, 0, ci))],
        scratch_shapes=[pltpu.VMEM((VREG_SUBLANES, tc), F32), pltpu.VMEM((VREG_SUBLANES, tc), F32),
                        pltpu.VMEM((tt, tc), F32), pltpu.VMEM((tt, tc), F32)],
        compiler_params=_cparams(("arbitrary", "arbitrary", "arbitrary")),
        name="lru_prompt",
    )(proj, proj, conv_prev8, h0, conv_w, conv_b.reshape(1, c), wa_bf, wx_bf,
      ba.reshape(1, c), bx.reshape(1, c), lam.reshape(1, c))


def _lru_step_kernel(xl_ref, gl_ref, cprev_ref, h0_ref, cw_ref, cb_ref, wa_ref, wx_ref, ba_ref, bx_ref, lam_ref,
                     o_ref, hl_ref):
    x = xl_ref[0]
    cw = cw_ref[...]
    xc = cb_ref[...] + x * cw[CONV_W - 1:CONV_W]
    for j in range(CONV_W - 1):
        xc = xc + cprev_ref[j] * cw[j:j + 1]
    a, bxs = _lru_gates(xc, wa_ref, wx_ref, ba_ref[...], bx_ref[...], lam_ref[...])
    h = a * h0_ref[...] + bxs
    hl_ref[...] = h
    o_ref[...] = (h * _gelu(gl_ref[0])).astype(o_ref.dtype)


def lru_step(proj_s, conv_prev_t, h0, conv_w, conv_b, wa_bf, wx_bf, ba, bx, lam):
    _, db, _ = proj_s.shape
    c = D_LRU
    tc = 1024
    hpt = tc // LRU_BLOCK
    vec = lambda: pl.BlockSpec((1, tc), lambda ci: (0, ci))
    return pl.pallas_call(
        _lru_step_kernel,
        out_shape=[jax.ShapeDtypeStruct((db, c), BF16), jax.ShapeDtypeStruct((db, c), F32)],
        grid=(c // tc,),
        in_specs=[pl.BlockSpec((1, db, tc), lambda ci: (0, 0, COL_XL // tc + ci)),
                  pl.BlockSpec((1, db, tc), lambda ci: (0, 0, COL_GL // tc + ci)),
                  pl.BlockSpec((CONV_W - 1, db, tc), lambda ci: (0, 0, ci)),
                  pl.BlockSpec((db, tc), lambda ci: (0, ci)),
                  pl.BlockSpec((CONV_W, tc), lambda ci: (0, ci)),
                  vec(),
                  pl.BlockSpec((hpt, LRU_BLOCK, LRU_BLOCK), lambda ci: (ci, 0, 0)),
                  pl.BlockSpec((hpt, LRU_BLOCK, LRU_BLOCK), lambda ci: (ci, 0, 0)),
                  vec(), vec(), vec()],
        out_specs=[pl.BlockSpec((db, tc), lambda ci: (0, ci)), pl.BlockSpec((db, tc), lambda ci: (0, ci))],
        compiler_params=_cparams(("arbitrary",)),
        name="lru_step",
    )(proj_s, proj_s, conv_prev_t, h0, conv_w, conv_b.reshape(1, c), wa_bf, wx_bf,
      ba.reshape(1, c), bx.reshape(1, c), lam.reshape(1, c))


def _merge_kernel(ol_ref, oa_ref, g0_ref, g1_ref, wl_ref, wa_ref, o_ref):
    yl = jnp.dot(ol_ref[0], wl_ref[...], preferred_element_type=F32)
    ya = jnp.dot(oa_ref[0], wa_ref[...], preferred_element_type=F32)
    o_ref[0] = (jax.nn.sigmoid(g0_ref[0]) * yl + jax.nn.sigmoid(g1_ref[0]) * ya).astype(o_ref.dtype)


def branch_merge(o_lru, o_attn, proj, wl_bf, wa_bf):
    b, s, d = o_lru.shape
    tm = min(1024, s)
    tn = 512
    return pl.pallas_call(
        _merge_kernel,
        out_shape=jax.ShapeDtypeStruct((b, s, d), BF16),
        grid=(b, s // tm, d // tn),
        in_specs=[pl.BlockSpec((1, tm, d), lambda bb, i, j: (bb, i, 0)),
                  pl.BlockSpec((1, tm, d), lambda bb, i, j: (bb, i, 0)),
                  pl.BlockSpec((1, tm, tn), lambda bb, i, j: (bb, i, COL_GB // tn + j)),
                  pl.BlockSpec((1, tm, tn), lambda bb, i, j: (bb, i, (COL_GB + D_MODEL) // tn + j)),
                  pl.BlockSpec((d, tn), lambda bb, i, j: (0, j)),
                  pl.BlockSpec((d, tn), lambda bb, i, j: (0, j))],
        out_specs=pl.BlockSpec((1, tm, tn), lambda bb, i, j: (bb, i, j)),
        compiler_params=_cparams(("arbitrary", "arbitrary", "arbitrary")),
        name="branch_merge",
    )(o_lru, o_attn, proj, proj, wl_bf, wa_bf)


def _out_proj_kernel(m_ref, x_ref, gt_ref, w_ref, o_ref):
    o_ref[0] = x_ref[0] + gt_ref[0] * jnp.dot(m_ref[0], w_ref[...], preferred_element_type=F32)


def out_project(merged, x, gt, w_bf):
    b, s, d = x.shape
    tm = min(1024, s)
    tn = 512
    return pl.pallas_call(
        _out_proj_kernel,
        out_shape=jax.ShapeDtypeStruct((b, s, d), F32),
        grid=(b, s // tm, d // tn),
        in_specs=[pl.BlockSpec((1, tm, d), lambda bb, i, j: (bb, i, 0)),
                  pl.BlockSpec((1, tm, tn), lambda bb, i, j: (bb, i, j)),
                  _mod_spec(gt, tm, tn),
                  pl.BlockSpec((d, tn), lambda bb, i, j: (0, j))],
        out_specs=pl.BlockSpec((1, tm, tn), lambda bb, i, j: (bb, i, j)),
        compiler_params=_cparams(("arbitrary", "arbitrary", "arbitrary")),
        name="out_project",
    )(merged, x, gt, w_bf)


def _topk_rows(s, k, n):
    row = lax.broadcasted_iota(jnp.int32, s.shape, 0)
    slot = lax.broadcasted_iota(jnp.int32, (k, s.shape[1]), 0)
    vals = jnp.zeros((k, s.shape[1]), F32)
    idxs = jnp.zeros((k, s.shape[1]), jnp.int32)
    for r in range(k):
        m = jnp.max(s, axis=0, keepdims=True)
        am = jnp.min(jnp.where(s == m, row, n), axis=0, keepdims=True)
        vals = jnp.where(slot == r, m, vals)
        idxs = jnp.where(slot == r, am, idxs)
        s = jnp.where(row == am, -jnp.inf, s)
    return vals, idxs


def _peer_topk_kernel(q_ref, sk_ref, e_ref, g_ref):
    tt = q_ref.shape[0]
    kk = PEER_TOPK
    for hh in range(PEER_HEADS):
        sv, si = [], []
        for p in range(2):
            c0 = (hh * 2 + p) * PEER_HALF
            qc = q_ref[:, c0:c0 + PEER_HALF]
            s = lax.dot_general(sk_ref[hh, p], qc, (((1,), (1,)), ((), ())), precision=lax.Precision.HIGHEST,
                                preferred_element_type=F32)
            v, i = _topk_rows(s, kk, PEER_N_KEYS)
            sv.append(v)
            si.append(i)
        half = kk // 2
        cand = jnp.concatenate([sv[0][0:1] + sv[1]]
                               + [sv[0][a:a + 1] + sv[1][0:half] for a in range(1, half)]
                               + [sv[0][half:kk] + sv[1][0:1]], axis=0)
        cidx = jnp.concatenate([si[0][0:1] * PEER_N_KEYS + si[1]]
                               + [si[0][a:a + 1] * PEER_N_KEYS + si[1][0:half] for a in range(1, half)]
                               + [si[0][half:kk] * PEER_N_KEYS + si[1][0:1]], axis=0)
        ncand = cand.shape[0]
        row = lax.broadcasted_iota(jnp.int32, cand.shape, 0)
        slot = lax.broadcasted_iota(jnp.int32, (kk, tt), 0)
        fv = jnp.zeros((kk, tt), F32)
        eid = jnp.zeros((kk, tt), jnp.int32)
        for r in range(kk):
            m = jnp.max(cand, axis=0, keepdims=True)
            am = jnp.min(jnp.where(cand == m, row, ncand), axis=0, keepdims=True)
            pick = row == am
            fv = jnp.where(slot == r, m, fv)
            eid = jnp.where(slot == r, jnp.max(jnp.where(pick, cidx, -1), axis=0, keepdims=True), eid)
            cand = jnp.where(pick, -jnp.inf, cand)
        ex = jnp.exp(fv - fv[0:1])
        g_ref[hh * kk:(hh + 1) * kk, :] = ex / jnp.sum(ex, axis=0, keepdims=True)
        e_ref[hh * kk:(hh + 1) * kk, :] = eid


def peer_topk(qp, sub_keys):
    n = qp.shape[0]
    tt = 256 if n % 256 == 0 else n
    return pl.pallas_call(
        _peer_topk_kernel,
        out_shape=[jax.ShapeDtypeStruct((PEER_SEL, n), jnp.int32), jax.ShapeDtypeStruct((PEER_SEL, n), F32)],
        grid=(n // tt,),
        in_specs=[pl.BlockSpec((tt, qp.shape[1]), lambda i: (i, 0)),
                  pl.BlockSpec(sub_keys.shape, lambda i: (0, 0, 0, 0))],
        out_specs=[pl.BlockSpec((PEER_SEL, tt), lambda i: (0, i)), pl.BlockSpec((PEER_SEL, tt), lambda i: (0, i))],
        compiler_params=_cparams(("arbitrary",)),
        name="peer_topk",
    )(qp, sub_keys)


PEER_TOK_TILE = 8


def _peer_gather_kernel(e_ref, e1_ref, e2_ref, x_ref, sh_ref, sc_ref, gt_ref, g2_ref, gf_ref, gate_ref, uv_hbm, o_ref,
                        buf0, buf1, buf2, sem, *, n):
    i = pl.program_id(0)
    nrow = PEER_TOK_TILE * PEER_SEL
    bufs = (buf0, buf1, buf2)
    nbuf = len(bufs)

    def row_copy(idx_ref, r, dst):
        return pltpu.make_async_copy(uv_hbm.at[idx_ref[r]], bufs[dst].at[r], sem.at[dst])

    def tile_wait(which):
        pltpu.make_async_copy(uv_hbm.at[pl.ds(0, nrow)], bufs[which], sem.at[which]).wait()

    @pl.when(i == 0)
    def _():
        def first(r, c):
            row_copy(e_ref, r, 0).start()
            row_copy(e1_ref, r, 1).start()
            return c
        lax.fori_loop(0, nrow, first, 0, unroll=8)

    sub_id = lax.broadcasted_iota(jnp.int32, (ROW_SUB, LANES), 0)
    lane_id = lax.broadcasted_iota(jnp.int32, (ROW_SUB, LANES), 1)
    diag = (lane_id % ROW_SUB) == sub_id
    mat_r = lax.broadcasted_iota(jnp.int32, (LANES, LANES), 0)
    mat_c = lax.broadcasted_iota(jnp.int32, (LANES, LANES), 1)
    group_sum = (mat_r // ROW_SUB == mat_c // ROW_SUB).astype(F32)
    spread = (mat_r % VREG_SUBLANES == mat_c // ROW_SUB).astype(F32)

    def hi_lo(x):
        hi = x.astype(BF16)
        return jnp.concatenate([hi, (x - hi.astype(F32)).astype(BF16)], axis=0)

    def fold(x):
        return x[0:ROW_SUB] + x[ROW_SUB:2 * ROW_SUB]

    def tile(cur):
        ahead = (cur + 2) % nbuf
        buf = bufs[cur]
        tile_wait(cur)
        toks = range(PEER_TOK_TILE)

        def issue(t):
            for k in range(PEER_SEL):
                row_copy(e2_ref, t * PEER_SEL + k, ahead).start(priority=k % 2)

        def rms(v, g):
            return v * lax.rsqrt(jnp.mean(v * v, axis=-1, keepdims=True) + EPS) * g

        h_all = rms(x_ref[...], g2_ref[...]) * (1.0 + sc_ref[0]) + sh_ref[0]

        zs, gx = [], []
        for t in toks:
            issue(t)
            h = jnp.concatenate([h_all[t:t + 1, r * LANES:(r + 1) * LANES] for r in range(ROW_SUB)], axis=0)
            ub = buf[pl.ds(t * PEER_SEL, PEER_SEL), pl.ds(0, ROW_SUB), :].reshape(PEER_SEL * ROW_SUB, LANES)
            y = fold(lax.dot_general(hi_lo(h), ub, (((1,), (1,)), ((), ())), preferred_element_type=F32))
            zs.append(jnp.concatenate(
                [jnp.sum(jnp.where(diag, y[:, r * LANES:(r + 1) * LANES], 0.0), axis=0, keepdims=True)
                 for r in range(ROW_SUB)], axis=0))
            gx.append(jnp.where(lane_id // VREG_SUBLANES == sub_id, gate_ref[t:t + 1, :], 0.0))
        act = jnp.dot(jnp.concatenate(zs, axis=0), group_sum, precision=lax.Precision.HIGHEST,
                      preferred_element_type=F32)
        gex = jnp.dot(jnp.concatenate(gx, axis=0), spread, precision=lax.Precision.HIGHEST,
                      preferred_element_type=F32)
        w_all = gex * _gelu(act)
        for t in toks:
            whl = hi_lo(w_all[t * ROW_SUB:(t + 1) * ROW_SUB]).astype(F32)
            wexp = jnp.concatenate(
                [jnp.concatenate([jnp.where(diag, whl[r:r + 1], 0.0),
                                  jnp.where(diag, whl[ROW_SUB + r:ROW_SUB + r + 1], 0.0)], axis=0)
                 for r in range(ROW_SUB)], axis=1).astype(BF16)
            vb = buf[pl.ds(t * PEER_SEL, PEER_SEL), pl.ds(ROW_SUB, ROW_SUB), :].reshape(PEER_SEL * ROW_SUB, LANES)
            ff = fold(jnp.dot(wexp, vb, preferred_element_type=F32))
            for s in range(ROW_SUB):
                o_ref[t:t + 1, s * LANES:(s + 1) * LANES] = ff[s:s + 1]
        o_ref[...] = rms(x_ref[...] + gt_ref[0] * o_ref[...], gf_ref[...])

        @pl.when(i == n - 1)
        def _():
            tile_wait((cur + 1) % nbuf)
            tile_wait(ahead)

    for c in range(nbuf):
        @pl.when(i % nbuf == c)
        def _(c=c):
            tile(c)


def peer_gather(eidx, gates, x1, sh, sc, gt, norm_g, final_g, uv):
    b, s, d = x1.shape
    n = b * s
    tk = PEER_TOK_TILE
    nrow = tk * PEER_SEL
    nt = n // tk
    per_row = sh.shape[1] != 1
    assert s % tk == 0
    tiles_per_batch = s // tk
    if per_row:
        mod_spec = pl.BlockSpec((1, tk, d), lambda i: (0, i, 0))
    else:
        mod_spec = pl.BlockSpec((1, 1, d), lambda i: (i // tiles_per_batch, 0, 0))
    vec_spec = pl.BlockSpec((1, d), lambda i: (0, 0))
    out = pl.pallas_call(
        functools.partial(_peer_gather_kernel, n=nt),
        out_shape=jax.ShapeDtypeStruct((n, d), F32),
        grid=(nt,),
        in_specs=[pl.BlockSpec((nrow,), lambda i: (0,), memory_space=pltpu.SMEM),
                  pl.BlockSpec((nrow,), lambda i: (min(1, nt - 1),), memory_space=pltpu.SMEM),
                  pl.BlockSpec((nrow,), lambda i: (jnp.minimum(i + 2, nt - 1),), memory_space=pltpu.SMEM),
                  pl.BlockSpec((tk, d), lambda i: (i, 0)),
                  mod_spec, mod_spec, mod_spec, vec_spec, vec_spec,
                  pl.BlockSpec((tk, PEER_SEL), lambda i: (i, 0)),
                  pl.BlockSpec(memory_space=pl.ANY)],
        out_specs=pl.BlockSpec((tk, d), lambda i: (i, 0)),
        scratch_shapes=[pltpu.VMEM((nrow, 2 * ROW_SUB, LANES), BF16)] * 3 + [pltpu.SemaphoreType.DMA((3,))],
        compiler_params=_cparams(("arbitrary",)),
        name="peer_gather",
    )(eidx.reshape(-1), eidx.reshape(-1), eidx.reshape(-1), x1.reshape(n, d), sh, sc, gt, norm_g.reshape(1, d),
      final_g.reshape(1, d), gates, uv)
    return out.reshape(b, s, d)


def _peer_block(x1, sh2, sc2, gt2, norm2_g, wq, sub_keys, uv, final_g, cos, sin):
    b, s, d = x1.shape
    n = b * s
    qp = norm_proj(x1, sh2, sc2, norm2_g, wq, cos, sin, rope_cols=0)
    qp2 = qp.reshape(n, -1)
    n_pad = -(-n // LANES) * LANES
    if n_pad != n:
        qp2 = jnp.pad(qp2, ((0, n_pad - n), (0, 0)))
    e_t, g_t = peer_topk(qp2, sub_keys)
    return peer_gather(e_t.T[:n], g_t.T[:n], x1, sh2, sc2, gt2, norm2_g, final_g, uv)


def kernel(x_prompt, x_sample, c_prompt, c_sample, cache_k, cache_v, state_lru, state_conv, page_table, norm1_g, w_ada, b_ada, w_in, conv_w, conv_b, lru_wa, lru_ba, lru_wx, lru_bx, lru_lam, w_br_lru, w_br_attn, w_out, norm2_g, peer_wq, peer_subkeys, peer_u, peer_v, final_g):
    assert w_ada.shape[0] == 1, "single layer"
    b, s, d = x_prompt.shape
    db = x_sample.shape[0]
    n_pages = page_table.shape[1]
    past_len = n_pages * PAGE_SIZE
    assert s % MOBA_BLOCK == 0 and past_len % MOBA_BLOCK == 0 and x_sample.shape[1] == 1

    wl_bf, wa_bf, wo_bf = w_br_lru[0].astype(BF16), w_br_attn[0].astype(BF16), w_out[0].astype(BF16)
    lwa_bf, lwx_bf = lru_wa[0].astype(BF16), lru_wx[0].astype(BF16)
    uv = jnp.concatenate([peer_u[0].reshape(-1, ROW_SUB, LANES), peer_v[0].reshape(-1, ROW_SUB, LANES)],
                         axis=1).astype(BF16)

    mod = ada_project(jnp.concatenate([c_prompt, c_sample], axis=0), w_ada[0], b_ada[0])
    mod_p = [m.reshape(b, 1, d) for m in jnp.split(mod[:b], N_MOD, axis=-1)]
    mod_s = [m.reshape(1, db, d) for m in jnp.split(mod[b:], N_MOD, axis=-1)]

    cos_p, sin_p = rope_tables(jnp.arange(s, dtype=jnp.int32))
    proj = norm_proj(x_prompt, mod_p[0], mod_p[1], norm1_g[0], w_in[0], cos_p, sin_p, rope_cols=D_ATTN + D_KV)
    kmean = moba_block_means(proj)
    o_attn = moba_prompt_attention(proj, kmean)
    o_lru, h_last_p = lru_prompt(proj, jnp.zeros((b, VREG_SUBLANES, D_LRU), F32), jnp.zeros((b, 1, D_LRU), F32),
                                 conv_w[0], conv_b[0], lwa_bf, lwx_bf, lru_ba[0], lru_bx[0], lru_lam[0])
    merged = branch_merge(o_lru, o_attn, proj, wl_bf, wa_bf)
    x1 = out_project(merged, x_prompt, mod_p[2], wo_bf)
    y_prompt = _peer_block(x1, mod_p[3], mod_p[4], mod_p[5], norm2_g[0], peer_wq[0], peer_subkeys[0], uv,
                           final_g, cos_p, sin_p)
    k_prompt = proj[:, :, COL_K:COL_K + D_KV].reshape(1, b, s, N_KV_HEADS, HEAD_DIM)
    v_prompt = proj[:, :, COL_V:COL_V + D_KV].reshape(1, b, s, N_KV_HEADS, HEAD_DIM)
    conv_prompt = proj[:, s - (CONV_W - 1):, COL_XL:COL_XL + D_LRU].reshape(1, b, CONV_W - 1, D_LRU)

    xs = x_sample.reshape(1, db, d)
    cos_s, sin_s = rope_tables(jnp.full((db,), past_len, jnp.int32))
    proj_s = norm_proj(xs, mod_s[0], mod_s[1], norm1_g[0], w_in[0], cos_s, sin_s, rope_cols=D_ATTN + D_KV)
    q_s = proj_s[0, :, COL_Q:COL_Q + D_ATTN].reshape(db, N_HEADS, HEAD_DIM)
    k_new = proj_s[0, :, COL_K:COL_K + D_KV].reshape(db, 1, D_KV)
    v_new = proj_s[0, :, COL_V:COL_V + D_KV].reshape(db, 1, D_KV)
    n_pool = cache_k.shape[1]
    cache_k3 = cache_k[0].reshape(n_pool, PAGE_SIZE * N_KV_HEADS, HEAD_DIM)
    cache_v3 = cache_v[0].reshape(n_pool, PAGE_SIZE * N_KV_HEADS, HEAD_DIM)
    kmean_s = sample_block_means(cache_k3, page_table)
    sel = sample_select(q_s, kmean_s)[:, :, :MOBA_TOPK].reshape(db, N_HEADS * MOBA_TOPK)
    o_attn_s = sample_attention(q_s, k_new, v_new, cache_k3, cache_v3, page_table, sel)
    o_lru_s, h_last_s = lru_step(proj_s, jnp.transpose(state_conv[0], (1, 0, 2)), state_lru[0],
                                 conv_w[0], conv_b[0], lwa_bf, lwx_bf, lru_ba[0], lru_bx[0], lru_lam[0])
    merged_s = branch_merge(o_lru_s.reshape(1, db, d), o_attn_s.astype(BF16).reshape(1, db, d), proj_s, wl_bf, wa_bf)
    x1_s = out_project(merged_s, xs, mod_s[2], wo_bf)
    y_sample = _peer_block(x1_s, mod_s[3], mod_s[4], mod_s[5], norm2_g[0], peer_wq[0], peer_subkeys[0], uv,
                           final_g, cos_s, sin_s)
    xl_s = proj_s[0, :, COL_XL:COL_XL + D_LRU]
    conv_sample = jnp.concatenate([state_conv[0][:, 1:], xl_s[:, None, :]], axis=1)[None]

    return (y_prompt, y_sample.reshape(db, 1, d), k_prompt, v_prompt,
            h_last_p.reshape(1, b, D_LRU), conv_prompt,
            k_new.reshape(1, db, 1, N_KV_HEADS, HEAD_DIM), v_new.reshape(1, db, 1, N_KV_HEADS, HEAD_DIM),
            h_last_s.reshape(1, db, D_LRU), conv_sample)
```
